```python
import jax
import jax.numpy as jnp
from jax import lax
import numpy as np

D_MODEL = 1024
BATCH = 2
SEQ = 16384
DEPTH = 2

GRID_W = 64
CTX_LEN = 256
N_MIXERS = 2
N_POOL_LAYERS = (DEPTH + N_MIXERS - 1) // N_MIXERS
N_RWKV_LAYERS = DEPTH // N_MIXERS
N_MOD = 6
NORM_EPS = 1e-6
POOL_WINDOWS = (2, 4, 8, 16)
POOL_GROUP = D_MODEL // len(POOL_WINDOWS)
RWKV_HEAD = 64
RWKV_HEADS = D_MODEL // RWKV_HEAD
DECAY_LORA = 64
ICL_LORA = 64
GATE_LORA = 128
GN_EPS = 64e-5
N_EXPERTS = 32
TOP_K = 4
D_EXPERT = D_MODEL
SWIGLU_ALPHA = 1.702
SWIGLU_LIMIT = 7.0
MOE_BLOCK = 256

kernel_name = 'hybrid_pool_rwkv7_moe_dit'


def rms_norm(x, g):
    xf = x.astype(jnp.float32)
    y = xf * lax.rsqrt(jnp.mean(xf * xf, axis=-1, keepdims=True) + NORM_EPS)
    return (y * g.astype(jnp.float32)).astype(x.dtype)


def ada_params(cond, w, b):
    m = jax.nn.silu(cond) @ w + b
    return m.reshape(m.shape[:-1] + (N_MOD, D_MODEL))


def pre(stream, g, shift, scale):
    return rms_norm(stream, g) * (1 + scale) + shift


def pool_mixer(h, w_pool, ls):
    L = h.shape[2]
    hf = h.astype(jnp.float32)
    prefix = jnp.pad(jnp.cumsum(hf, axis=2), ((0, 0), (0, 0), (1, 0), (0, 0)))
    pos = jnp.arange(L)
    diffs = []
    for gi, win in enumerate(POOL_WINDOWS):
        lo = jnp.clip(pos - win // 2, 0, L - 1)
        hi = jnp.clip(pos + win // 2 - 1, 0, L - 1)
        cnt = (hi - lo + 1).astype(jnp.float32)[:, None]
        sl = slice(gi * POOL_GROUP, (gi + 1) * POOL_GROUP)
        p = prefix[..., sl]
        diffs.append((p[:, :, hi + 1] - p[:, :, lo]) / cnt - hf[..., sl])
    d = jnp.stack(diffs, axis=-2).astype(h.dtype)
    y = jnp.einsum('brlgc,gce->brlge', d, w_pool).reshape(h.shape)
    return y * ls


def centred_shift(h):
    hp = jnp.pad(h, ((0, 0), (1, 1), (0, 0)))
    return 0.5 * (hp[:, :-2] + hp[:, 2:])


def to_heads_tm(a):
    B, T, _ = a.shape
    return a.reshape(B, T, RWKV_HEADS, RWKV_HEAD).transpose(1, 0, 2, 3)


def rwkv_features(h, mu, w_rkv, w0, w1, w2, a0, a1, a2, g1, g2, k_k, k_a, readout):
    B, T, D = h.shape
    f32 = jnp.float32
    xx = centred_shift(h) - h
    k = ((h + xx * mu[2]) @ w_rkv[1]).astype(f32)
    v = ((h + xx * mu[3]) @ w_rkv[2]).astype(f32)
    lw = jnp.tanh(jnp.einsum('btc,dcr->dbtr', h + xx * mu[1], w1))
    zw = (w0[:, None, None, :] + jnp.einsum('dbtr,drc->dbtc', lw, w2)).astype(f32)
    decay = jnp.exp(-jnp.exp(-jax.nn.softplus(-zw) - 0.5))
    la = jnp.einsum('btc,dcr->dbtr', h + xx * mu[4], a1)
    icl = jax.nn.sigmoid((a0[:, None, None, :] + jnp.einsum('dbtr,drc->dbtc', la, a2)).astype(f32))
    kk = (k * k_k.astype(f32)).reshape(B, T, RWKV_HEADS, RWKV_HEAD)
    kk = (kk / jnp.maximum(jnp.linalg.norm(kk, axis=-1, keepdims=True), 1e-12)).reshape(B, T, D)
    k_dir = k * (1.0 + (icl - 1.0) * k_a.astype(f32))
    if readout:
        r = ((h + xx * mu[0]) @ w_rkv[0]).astype(f32)
        g = jax.nn.sigmoid((h + xx * mu[5]) @ g1) @ g2
        return (decay, icl, k_dir, v, kk, r, g)
    return (decay, icl, k_dir, v, kk)


def scan_inputs(feat, d):
    decay, icl, k_dir, v, kk = feat[:5]
    xs = (to_heads_tm(decay[d]), to_heads_tm(k_dir[d]), to_heads_tm(v),
          to_heads_tm(kk), to_heads_tm(kk * icl[d]))
    if len(feat) > 5:
        xs = xs + (to_heads_tm(feat[5]),)
    return xs


def wkv_scan(s0, xs, reverse):
    emit = len(xs) == 6

    def step(S, inp):
        w_t, k_t, v_t, kk_t, b_t = inp[:5]
        S = (S * w_t[:, :, None, :]
             - jnp.einsum('bhvk,bhk->bhv', S, kk_t)[..., None] * b_t[:, :, None, :]
             + v_t[..., None] * k_t[:, :, None, :])
        return S, (jnp.einsum('bhvk,bhk->bhv', S, inp[5]) if emit else None)

    return lax.scan(step, s0, xs, reverse=reverse)


def rwkv_readout(y_tm, feat, r_k, ln_g, ln_b, w_o, dtype):
    _, _, k_dir, v, _, r, g = feat
    T, B, H, K = y_tm.shape
    y = y_tm.transpose(1, 0, 2, 3)
    m = jnp.mean(y, axis=-1, keepdims=True)
    var = jnp.mean(jnp.square(y - m), axis=-1, keepdims=True)
    yn = ((y - m) * lax.rsqrt(var + GN_EPS)).reshape(B, T, H * K) * ln_g + ln_b
    hd = lambda a: a.reshape(B, T, H, K)
    bonus = jnp.sum(hd(r) * hd(k_dir[0] + k_dir[1]) * r_k, axis=-1, keepdims=True) * hd(v)
    out = (yn + bonus.reshape(B, T, H * K)) * g
    return out.astype(dtype) @ w_o


def rwkv_mixer(h_lat, h_ctx, mu, w_rkv, w0, w1, w2, a0, a1, a2, g1, g2, k_k, k_a,
               r_k, ln_g, ln_b, w_o, ctx_readout):
    par = (mu, w_rkv, w0, w1, w2, a0, a1, a2, g1, g2, k_k, k_a)
    fc = rwkv_features(h_ctx, *par, readout=ctx_readout)
    fl = rwkv_features(h_lat, *par, readout=True)
    s0 = jnp.zeros((h_lat.shape[0], RWKV_HEADS, RWKV_HEAD, RWKV_HEAD), jnp.float32)
    ys_lat, ys_ctx = [], []
    for d, rev in enumerate((False, True)):
        s_ctx, yc = wkv_scan(s0, scan_inputs(fc, d), rev)
        _, yl = wkv_scan(s_ctx, scan_inputs(fl, d), rev)
        ys_lat.append(yl)
        ys_ctx.append(yc)
    out_lat = rwkv_readout(ys_lat[0] + ys_lat[1], fl, r_k, ln_g, ln_b, w_o, h_lat.dtype)
    out_ctx = (rwkv_readout(ys_ctx[0] + ys_ctx[1], fc, r_k, ln_g, ln_b, w_o, h_ctx.dtype)
               if ctx_readout else None)
    return out_lat, out_ctx


def moe(h, router_w, router_b, w_gu, b_gu, w_down, b_down):
    B, T, D = h.shape
    N = B * T
    A = N * TOP_K
    t = h.reshape(N, D)
    logits = (t @ router_w).astype(jnp.float32) + router_b.astype(jnp.float32)
    top_v, top_e = lax.top_k(logits, TOP_K)
    gates = jax.nn.softmax(top_v, axis=-1)
    e_flat = top_e.reshape(A)
    tok_flat = jnp.arange(A, dtype=jnp.int32) // TOP_K
    order = jnp.argsort(e_flat)
    e_s, tok_s, g_s = e_flat[order], tok_flat[order], gates.reshape(A)[order]
    counts = jnp.bincount(e_flat, length=N_EXPERTS)
    start = jnp.cumsum(counts) - counts
    padded = (counts + MOE_BLOCK - 1) // MOE_BLOCK * MOE_BLOCK
    pend = jnp.cumsum(padded)
    pstart = pend - padded
    dest = pstart[e_s] + jnp.arange(A) - start[e_s]
    nb = -(-A // MOE_BLOCK) + N_EXPERTS
    R = nb * MOE_BLOCK
    tok_buf = jnp.full((R,), N, jnp.int32).at[dest].set(tok_s)
    g_buf = jnp.zeros((R,), jnp.float32).at[dest].set(g_s)
    blk_e = jnp.minimum(jnp.searchsorted(pend, jnp.arange(nb) * MOE_BLOCK, side='right'), N_EXPERTS - 1)
    t_pad = jnp.concatenate([t, jnp.zeros((1, D), t.dtype)], axis=0)
    xb = t_pad[tok_buf].reshape(nb, MOE_BLOCK, D)

    def expert_block(args):
        xblk, e = args
        gu = xblk @ w_gu[e] + b_gu[e]
        gate, up = gu[:, :D_EXPERT], gu[:, D_EXPERT:]
        gate = jnp.minimum(gate, SWIGLU_LIMIT)
        up = jnp.clip(up, -SWIGLU_LIMIT, SWIGLU_LIMIT)
        glu = gate * jax.nn.sigmoid(SWIGLU_ALPHA * gate)
        return ((up + 1) * glu) @ w_down[e] + b_down[e]

    yb = lax.map(expert_block, (xb, blk_e))
    out = jax.ops.segment_sum(yb.reshape(R, D).astype(jnp.float32) * g_buf[:, None],
                              tok_buf, num_segments=N + 1)[:N]
    return out.reshape(B, T, D).astype(h.dtype)


def setup_inputs(seed: int = 0) -> dict:
    key = jax.random.key(seed)
    ks = iter(jax.random.split(key, 40))
    nrm = lambda shape, s: jax.random.normal(next(ks), shape, jnp.float32) * s
    uni = lambda shape: jax.random.uniform(next(ks), shape, jnp.float32)
    D, F, E = D_MODEL, D_EXPERT, N_EXPERTS
    NP, NR = N_POOL_LAYERS, N_RWKV_LAYERS
    return {
        'x': nrm((BATCH, SEQ, D), 1.0),
        'c': nrm((BATCH, D), 1.0),
        'ctx': nrm((BATCH, CTX_LEN, D), 1.0),
        'c_ctx': nrm((D,), 1.0),
        'norm_g': 1.0 + nrm((DEPTH, 2, D), 0.02),
        'ada_w': nrm((DEPTH, D, N_MOD * D), 0.5 * D ** -0.5),
        'ada_b': nrm((DEPTH, N_MOD * D), 0.02),
        'pool_w': nrm((NP, len(POOL_WINDOWS), POOL_GROUP, POOL_GROUP), POOL_GROUP ** -0.5),
        'pool_ls': 1.0 + nrm((NP, D), 0.1),
        'rwkv_mu': uni((NR, 6, D)),
        'rwkv_w_rkv': nrm((NR, 3, D, D), D ** -0.5),
        'rwkv_w0': -6.5 + 5.0 * uni((NR, 2, D)),
        'rwkv_w1': nrm((NR, 2, D, DECAY_LORA), D ** -0.5),
        'rwkv_w2': nrm((NR, 2, DECAY_LORA, D), 0.5 * DECAY_LORA ** -0.5),
        'rwkv_a0': nrm((NR, 2, D), 0.1),
        'rwkv_a1': nrm((NR, 2, D, ICL_LORA), D ** -0.5),
        'rwkv_a2': nrm((NR, 2, ICL_LORA, D), 0.5 * ICL_LORA ** -0.5),
        'rwkv_g1': nrm((NR, D, GATE_LORA), D ** -0.5),
        'rwkv_g2': nrm((NR, GATE_LORA, D), GATE_LORA ** -0.5),
        'rwkv_k_k': 0.85 + nrm((NR, D), 0.05),
        'rwkv_k_a': 1.0 + nrm((NR, D), 0.05),
        'rwkv_r_k': nrm((NR, RWKV_HEADS, RWKV_HEAD), 0.05),
        'rwkv_ln_g': 1.0 + nrm((NR, D), 0.02),
        'rwkv_ln_b': nrm((NR, D), 0.02),
        'rwkv_w_o': nrm((NR, D, D), D ** -0.5),
        'moe_router_w': nrm((DEPTH, D, E), D ** -0.5),
        'moe_router_b': nrm((DEPTH, E), 0.01),
        'moe_w_gu': nrm((DEPTH, E, D, 2 * F), D ** -0.5),
        'moe_b_gu': nrm((DEPTH, E, 2 * F), 0.02),
        'moe_w_down': nrm((DEPTH, E, F, D), F ** -0.5),
        'moe_b_down': nrm((DEPTH, E, D), 0.02),
        'final_g': 1.0 + nrm((D,), 0.02),
    }


def reference(x, c, ctx, c_ctx, norm_g, ada_w, ada_b, pool_w, pool_ls,
              rwkv_mu, rwkv_w_rkv, rwkv_w0, rwkv_w1, rwkv_w2, rwkv_a0, rwkv_a1, rwkv_a2,
              rwkv_g1, rwkv_g2, rwkv_k_k, rwkv_k_a, rwkv_r_k, rwkv_ln_g, rwkv_ln_b, rwkv_w_o,
              moe_router_w, moe_router_b, moe_w_gu, moe_b_gu, moe_w_down, moe_b_down, final_g):
    B, L, D = x.shape
    rows = L // GRID_W
    n_ctx = ctx.shape[1]
    for i in range(DEPTH):
        last = i == DEPTH - 1
        j = i // N_MIXERS
        m_lat = ada_params(c, ada_w[i], ada_b[i])[:, :, None, :]
        m_ctx = ada_params(c_ctx, ada_w[i], ada_b[i])
        h_lat = pre(x, norm_g[i, 0], m_lat[:, 0], m_lat[:, 1])
        y_ctx = None
        if i % N_MIXERS == 0:
            y_lat = pool_mixer(h_lat.reshape(B, rows, GRID_W, D), pool_w[j], pool_ls[j]).reshape(B, L, D)
            if not last:
                h_ctx = pre(ctx, norm_g[i, 0], m_ctx[0], m_ctx[1])
                y_ctx = pool_mixer(h_ctx[:, None], pool_w[j], pool_ls[j])[:, 0]
        else:
            h_ctx = pre(ctx, norm_g[i, 0], m_ctx[0], m_ctx[1])
            y_lat, y_ctx = rwkv_mixer(h_lat, h_ctx, rwkv_mu[j], rwkv_w_rkv[j], rwkv_w0[j], rwkv_w1[j],
                                      rwkv_w2[j], rwkv_a0[j], rwkv_a1[j], rwkv_a2[j], rwkv_g1[j],
                                      rwkv_g2[j], rwkv_k_k[j], rwkv_k_a[j], rwkv_r_k[j], rwkv_ln_g[j],
                                      rwkv_ln_b[j], rwkv_w_o[j], ctx_readout=not last)
        x = x + m_lat[:, 2] * y_lat
        h2_lat = pre(x, norm_g[i, 1], m_lat[:, 3], m_lat[:, 4])
        moe_w = (moe_router_w[i], moe_router_b[i], moe_w_gu[i], moe_b_gu[i], moe_w_down[i], moe_b_down[i])
        if last:
            x = x + m_lat[:, 5] * moe(h2_lat, *moe_w)
        else:
            ctx = ctx + m_ctx[2] * y_ctx
            h2_ctx = pre(ctx, norm_g[i, 1], m_ctx[3], m_ctx[4])
            f = moe(jnp.concatenate([h2_ctx, h2_lat], axis=1), *moe_w)
            ctx = ctx + m_ctx[5] * f[:, :n_ctx]
            x = x + m_lat[:, 5] * f[:, n_ctx:]
    return rms_norm(x, final_g)
```

```python
import functools

import numpy as np
import jax
import jax.numpy as jnp
from jax import lax
from jax.experimental import pallas as pl
from jax.experimental.pallas import tpu as pltpu

F32 = jnp.float32
BF16 = jnp.bfloat16
NN = (((1,), (0,)), ((), ()))
NT = (((1,), (1,)), ((), ()))
TN = (((0,), (0,)), ((), ()))

N_MOD = 6
NORM_EPS = 1e-6
GRID_W = 64
POOL_WINDOWS = (2, 4, 8, 16)
RWKV_HEAD = 64
GN_EPS = 64e-5
TOP_K = 4
SWIGLU_ALPHA = 1.702
SWIGLU_LIMIT = 7.0

TOK_TILE = 256
MOE_TILE = 2048
MOE_ROWS = 128
CHUNK = 64
QUAD = 4 * RWKV_HEAD
HEAD_PAD = 128
VMEM_LIMIT_BYTES = 56 * 1024 * 1024


def _dotx(a, b, dims=NN):
    return lax.dot_general(a, b, dims, precision=lax.Precision.HIGHEST, preferred_element_type=F32)


def _dotb(a, b, dims=NN):
    return lax.dot_general(a.astype(BF16), b.astype(BF16), dims, preferred_element_type=F32)


def _params(*sem):
    return pltpu.CompilerParams(dimension_semantics=sem, vmem_limit_bytes=VMEM_LIMIT_BYTES)


def _rms(x):
    return x * lax.rsqrt(jnp.mean(x * x, axis=-1, keepdims=True) + NORM_EPS)


def _sigmoid(x):
    return 1.0 / (1.0 + jnp.exp(-x))


def _ada_kernel(c_ref, w_ref, b_ref, o_ref):
    c = c_ref[...]
    o_ref[0] = _dotx(c * _sigmoid(c), w_ref[0]) + b_ref[0]


def _ada(cond8, ada_w, ada_b):
    depth, d, nd = ada_w.shape
    return pl.pallas_call(
        _ada_kernel,
        grid=(depth, nd // d),
        in_specs=[pl.BlockSpec((8, d), lambda i, n: (0, 0)),
                  pl.BlockSpec((1, d, d), lambda i, n: (i, 0, n)),
                  pl.BlockSpec((1, 1, d), lambda i, n: (i, 0, n))],
        out_specs=pl.BlockSpec((1, 8, d), lambda i, n: (i, 0, n)),
        out_shape=jax.ShapeDtypeStruct((depth, 8, nd), F32),
        compiler_params=_params("parallel", "parallel"),
        name="ada",
    )(cond8, ada_w, ada_b.reshape(depth, 1, nd))


def _route(x_new, mod, g2, rwt_ref, rb_ref, u_ref, off_ref, h2_ref, pos_ref, gd_ref, reset):
    h2 = _rms(x_new) * g2 * (1.0 + mod[4:5]) + mod[3:4]
    h2_ref[...] = h2.astype(BF16)
    logits = _dotx(rwt_ref[...], h2, NT) + rb_ref[...]
    n_e = logits.shape[0]
    eio = lax.broadcasted_iota(jnp.int32, logits.shape, 0).astype(F32)
    live = logits
    sels, vals = [], []
    for _ in range(TOP_K):
        m = jnp.max(live, axis=0, keepdims=True)
        idx = jnp.min(jnp.where(live == m, eio, float(n_e)), axis=0, keepdims=True)
        sel = eio == idx
        sels.append(sel)
        vals.append(m)
        live = jnp.where(sel, -jnp.inf, live)
    exps = [jnp.exp(v - vals[0]) for v in vals]
    inv = 1.0 / (exps[0] + exps[1] + exps[2] + exps[3])
    gd = jnp.zeros_like(logits)
    maskf = jnp.zeros_like(logits)
    for k in range(TOP_K):
        gd = jnp.where(sels[k], exps[k] * inv, gd)
        maskf = jnp.where(sels[k], 1.0, maskf)

    @pl.when(reset)
    def _():
        off_ref[...] = jnp.zeros_like(off_ref)

    incl = _dotb(maskf, u_ref[...])
    off = off_ref[:, 0:1]
    pos_ref[...] = jnp.where(maskf > 0.5, off + incl - 1.0, -1.0).astype(jnp.int32)
    gd_ref[...] = gd
    off_ref[...] = off_ref[...] + jnp.sum(maskf, axis=1, keepdims=True)


def _route_consts(tile):
    s = np.arange(tile)
    return jnp.asarray((s[:, None] <= s[None, :]).astype(np.float32), BF16)


def _pool_kernel(x_ref, mod_ref, vec_ref, pm_ref, pw_ref, rwt_ref, rb_ref, u_ref,
                 x1_ref, h2_ref, pos_ref, gd_ref, off_ref, *, tiles_per_moe):
    i = pl.program_id(0)
    x = x_ref[...]
    mod = mod_ref[0]
    vec = vec_ref[...]
    h = _rms(x) * vec[0:1] * (1.0 + mod[1:2]) + mod[0:1]
    gw = pw_ref.shape[1]
    ys = []
    for g in range(len(POOL_WINDOWS)):
        d = _dotx(pm_ref[g], h[:, g * gw:(g + 1) * gw])
        ys.append(_dotb(d, pw_ref[g]))
    y = jnp.concatenate(ys, axis=1) * vec[2:3]
    x1 = x + mod[2:3] * y
    x1_ref[...] = x1
    _route(x1, mod, vec[1:2], rwt_ref, rb_ref, u_ref, off_ref, h2_ref, pos_ref, gd_ref,
           i % tiles_per_moe == 0)


def _pool_matrices(tile, row_len):
    p = np.arange(tile)
    pp = p % row_len
    out = []
    for win in POOL_WINDOWS:
        lo = np.clip(pp - win // 2, 0, row_len - 1)
        hi = np.clip(pp + win // 2 - 1, 0, row_len - 1)
        cnt = (hi - lo + 1).astype(np.float64)
        same = (p[:, None] // row_len) == (p[None, :] // row_len)
        inwin = same & (pp[None, :] >= lo[:, None]) & (pp[None, :] <= hi[:, None])
        out.append(inwin / cnt[:, None] - np.eye(tile))
    return jnp.asarray(np.stack(out).astype(np.float32))


def _pool_layer(x2d, mod, vec, pool_w, rwt, rb, row_len, moe_tile, tokens_per_batch):
    ntok, d = x2d.shape
    t = TOK_TILE
    n_e = rwt.shape[0]
    tiles_per_batch = tokens_per_batch // t
    pm = _pool_matrices(t, row_len)
    kern = functools.partial(_pool_kernel, tiles_per_moe=moe_tile // t)
    const2 = lambda i: (0, 0)
    const3 = lambda i: (0, 0, 0)
    return pl.pallas_call(
        kern,
        grid=(ntok // t,),
        in_specs=[pl.BlockSpec((t, d), lambda i: (i, 0)),
                  pl.BlockSpec((1, 8, d), lambda i: (i // tiles_per_batch, 0, 0)),
                  pl.BlockSpec((8, d), const2),
                  pl.BlockSpec(pm.shape, const3),
                  pl.BlockSpec(pool_w.shape, const3),
                  pl.BlockSpec(rwt.shape, const2),
                  pl.BlockSpec(rb.shape, const2),
                  pl.BlockSpec((t, t), const2)],
        out_specs=[pl.BlockSpec((t, d), lambda i: (i, 0)),
                   pl.BlockSpec((t, d), lambda i: (i, 0)),
                   pl.BlockSpec((n_e, t), lambda i: (0, i)),
                   pl.BlockSpec((n_e, t), lambda i: (0, i))],
        out_shape=[jax.ShapeDtypeStruct((ntok, d), F32),
                   jax.ShapeDtypeStruct((ntok, d), BF16),
                   jax.ShapeDtypeStruct((n_e, ntok), jnp.int32),
                   jax.ShapeDtypeStruct((n_e, ntok), F32)],
        scratch_shapes=[pltpu.VMEM((n_e, 128), F32)],
        compiler_params=_params("arbitrary"),
        name="pool",
    )(x2d, mod, vec, pm, pool_w, rwt, rb, _route_consts(t))


def _moe_kernel(nsub_ref, pos_ref, gd_ref, x_ref, wgu_ref, bgu_ref, wdt_ref, bdt_ref, o_ref):
    j = pl.program_id(0)
    e = pl.program_id(1)
    n_e = pl.num_programs(1)
    ts = x_ref.shape[0]
    f = wdt_ref.shape[2]

    @pl.when(e == 0)
    def _():
        o_ref[...] = _dotx(bdt_ref[...], gd_ref[...])

    prow = pos_ref[pl.ds(e, 1), :]
    grow = gd_ref[pl.ds(e, 1), :]
    g_hi = grow.astype(BF16)
    g_mid = (grow - g_hi.astype(F32)).astype(BF16)
    g_lo = (grow - g_hi.astype(F32) - g_mid.astype(F32)).astype(BF16)
    g3 = jnp.concatenate([g_hi, g_mid, g_lo, jnp.zeros((5, ts), BF16)], axis=0)

    def body(s, carry):
        rows = lax.broadcasted_iota(jnp.int32, (MOE_ROWS, ts), 0) + s * MOE_ROWS
        oh = jnp.where(prow == rows, 1.0, 0.0).astype(BF16)
        xg = jnp.dot(oh, x_ref[...], preferred_element_type=F32).astype(BF16)
        gu = jnp.dot(xg, wgu_ref[0], preferred_element_type=F32) + bgu_ref[0]
        gate = jnp.minimum(gu[:, :f], SWIGLU_LIMIT)
        up = jnp.clip(gu[:, f:], -SWIGLU_LIMIT, SWIGLU_LIMIT)
        act = (up + 1.0) * (gate * _sigmoid(SWIGLU_ALPHA * gate))
        yt = lax.dot_general(wdt_ref[0], act.astype(BF16), NT, preferred_element_type=F32)
        g8 = lax.dot_general(g3, oh, NT, preferred_element_type=F32)
        gr = g8[0:1] + g8[1:2] + g8[2:3]
        o_ref[...] += jnp.dot((yt * gr).astype(BF16), oh, preferred_element_type=F32)
        return carry

    lax.fori_loop(0, nsub_ref[j * n_e + e], body, 0)


def _moe(h2, pos, gd, wgu, bgu, wdt, bdt, ts):
    ntok, d = h2.shape
    n_e = pos.shape[0]
    f2 = wgu.shape[2]
    nt = ntok // ts
    cnt = jnp.sum((pos.reshape(n_e, nt, ts) >= 0).astype(jnp.int32), axis=-1)
    nsub = ((cnt + MOE_ROWS - 1) // MOE_ROWS).T.reshape(-1)
    return pl.pallas_call(
        _moe_kernel,
        grid_spec=pltpu.PrefetchScalarGridSpec(
            num_scalar_prefetch=1,
            grid=(nt, n_e),
            in_specs=[pl.BlockSpec((n_e, ts), lambda j, e, ns: (0, j)),
                      pl.BlockSpec((n_e, ts), lambda j, e, ns: (0, j)),
                      pl.BlockSpec((ts, d), lambda j, e, ns: (j, 0)),
                      pl.BlockSpec((1, d, f2), lambda j, e, ns: (e, 0, 0)),
                      pl.BlockSpec((1, 1, f2), lambda j, e, ns: (e, 0, 0)),
                      pl.BlockSpec((1, d, f2 // 2), lambda j, e, ns: (e, 0, 0)),
                      pl.BlockSpec((d, n_e), lambda j, e, ns: (0, 0))],
            out_specs=pl.BlockSpec((d, ts), lambda j, e, ns: (0, j), pipeline_mode=pl.Buffered(1)),
        ),
        out_shape=jax.ShapeDtypeStruct((d, ntok), F32),
        compiler_params=_params("arbitrary", "arbitrary"),
        name="moe",
    )(nsub, pos, gd, h2, wgu, bgu, wdt, bdt)


def _respre_kernel(x_ref, ft_ref, modp_ref, modc_ref, vec_ref, *rest):
    x2_ref, h_ref = rest[-2], rest[-1]
    x2 = x_ref[...] + modp_ref[0][5:6] * jnp.transpose(ft_ref[...])
    x2_ref[...] = x2
    modc = modc_ref[0]
    h_ref[0] = _rms(x2) * vec_ref[0:1] * (1.0 + modc[1:2]) + modc[0:1]


def _respre(x2d, ft, modp, modc, vec, h_all, tokens_per_batch, tile_offset, seq_total):
    ntok, d = x2d.shape
    t = TOK_TILE
    nb = modp.shape[0]
    tpb = tokens_per_batch // t
    in_specs = [pl.BlockSpec((t, d), lambda i: (i, 0)),
                pl.BlockSpec((d, t), lambda i: (0, i)),
                pl.BlockSpec((1, 8, d), lambda i: (i // tpb, 0, 0)),
                pl.BlockSpec((1, 8, d), lambda i: (i // tpb, 0, 0)),
                pl.BlockSpec((8, d), lambda i: (0, 0))]
    args = [x2d, ft, modp, modc, vec]
    aliases = {}
    if h_all is not None:
        in_specs.append(pl.BlockSpec(memory_space=pl.ANY))
        args.append(h_all)
        aliases = {5: 1}
    return pl.pallas_call(
        _respre_kernel,
        grid=(ntok // t,),
        in_specs=in_specs,
        out_specs=[pl.BlockSpec((t, d), lambda i: (i, 0)),
                   pl.BlockSpec((1, t, d), lambda i: (i // tpb, tile_offset + i % tpb, 0))],
        out_shape=[jax.ShapeDtypeStruct((ntok, d), F32),
                   jax.ShapeDtypeStruct((nb, seq_total, d), F32)],
        input_output_aliases=aliases,
        compiler_params=_params("parallel"),
        name="respre",
    )(*args)


def _feat_kernel(h_ref, hp_ref, hn_ref, fv_ref, wrkv_ref, w1_ref, w2_ref, a1_ref, a2_ref, g1_ref, g2_ref,
                 hd_ref, hdt_ref,
                 lw0_ref, lw1_ref, kd0_ref, kd1_ref, b0_ref, b1_ref, v_ref, kk_ref, r_ref, g_ref,
                 *, n_ctx_tiles, n_tiles):
    i = pl.program_id(1)
    h = h_ref[0]
    t = h.shape[0]
    fv = fv_ref[...]
    first = jnp.logical_or(i == 0, i == n_ctx_tiles)
    last = jnp.logical_or(i == n_ctx_tiles - 1, i == n_tiles - 1)
    prow = jnp.where(first, 0.0, hp_ref[0][7:8])
    nrow = jnp.where(last, 0.0, hn_ref[0][0:1])
    rio = lax.broadcasted_iota(jnp.int32, h.shape, 0)
    hdn = jnp.where(rio == 0, prow, pltpu.roll(h, 1, 0))
    hup = jnp.where(rio == t - 1, nrow, pltpu.roll(h, t - 1, 0))
    xx = 0.5 * (hdn + hup) - h

    mix = lambda m: h + xx * fv[m:m + 1]
    r = _dotb(mix(0), wrkv_ref[0])
    k = _dotb(mix(2), wrkv_ref[1])
    v = _dotb(mix(3), wrkv_ref[2])
    tw = jnp.tanh(_dotb(mix(1), w1_ref[...]))
    la = _dotb(mix(4), a1_ref[...])
    g = _dotb(_sigmoid(_dotb(mix(5), g1_ref[...])), g2_ref[...])

    kkraw = k * fv[10:11]
    ss = _dotx(_dotx(kkraw * kkraw, hd_ref[...]), hdt_ref[...])
    kk = kkraw / jnp.maximum(jnp.sqrt(ss), 1e-12)
    k_a = fv[11:12]
    decay_scale = float(np.exp(-0.5))
    for d, (lw_ref, kd_ref, b_ref) in enumerate(((lw0_ref, kd0_ref, b0_ref), (lw1_ref, kd1_ref, b1_ref))):
        zw = fv[6 + d:7 + d] + _dotb(tw, w2_ref[d])
        lw_ref[0] = -decay_scale * _sigmoid(zw)
        icl = _sigmoid(fv[8 + d:9 + d] + _dotb(la, a2_ref[d]))
        kd_ref[0] = k * (1.0 + (icl - 1.0) * k_a)
        b_ref[0] = kk * icl
    v_ref[0] = v
    kk_ref[0] = kk
    r_ref[0] = r
    g_ref[0] = g


def _head_indicators(d):
    hd = np.zeros((d, HEAD_PAD), np.float32)
    hd[np.arange(d), np.arange(d) // RWKV_HEAD] = 1.0
    return jnp.asarray(hd), jnp.asarray(hd.T.copy())


def _features(h_all, fv, wrkv, w1c, w2p, a1c, a2p, g1, g2, hd, hdt, n_ctx_tiles):
    nb, s, d = h_all.shape
    t = TOK_TILE
    nt = s // t
    r8 = t // 8
    kern = functools.partial(_feat_kernel, n_ctx_tiles=n_ctx_tiles, n_tiles=nt)
    full = lambda a: pl.BlockSpec(a.shape, lambda b, i: (0,) * a.ndim)
    tok = pl.BlockSpec((1, t, d), lambda b, i: (b, i, 0))
    return pl.pallas_call(
        kern,
        grid=(nb, nt),
        in_specs=[tok,
                  pl.BlockSpec((1, 8, d), lambda b, i: (b, jnp.maximum(i * r8 - 1, 0), 0)),
                  pl.BlockSpec((1, 8, d), lambda b, i: (b, jnp.minimum((i + 1) * r8, s // 8 - 1), 0)),
                  full(fv), full(wrkv), full(w1c), full(w2p), full(a1c), full(a2p), full(g1), full(g2),
                  full(hd), full(hdt)],
        out_specs=[tok] * 10,
        out_shape=[jax.ShapeDtypeStruct((nb, s, d), F32)] * 10,
        compiler_params=_params("parallel", "parallel"),
        name="feat",
    )(h_all, h_all, h_all, fv, wrkv, w1c, w2p, a1c, a2p, g1, g2, hd, hdt)


def _scan_masks():
    n = 4 * CHUNK
    rr = np.arange(n)[:, None]
    cc = np.arange(n)[None, :]
    bd = (rr // CHUNK) == (cc // CHUNK)
    r_, c_ = rr % CHUNK, cc % CHUNK
    out = []
    for rev in (False, True):
        strict = bd & ((c_ > r_) if rev else (c_ < r_))
        incl = bd & ((c_ >= r_) if rev else (c_ <= r_))
        d16 = strict & ((r_ // 16) == (c_ // 16))
        l1 = strict & ((r_ // 32) == (c_ // 32)) & ((r_ // 16) != (c_ // 16))
        l2 = strict & ((r_ // 32) != (c_ // 32))
        out.append([strict, incl, d16, l1, l2])
    masks = [out[0][m] for m in range(5)] + [out[1][m] for m in range(5)] + [bd, rr == cc]
    hm = np.zeros((8, QUAD), np.float32)
    for j in range(4):
        hm[j, j * RWKV_HEAD:(j + 1) * RWKV_HEAD] = 1.0
    return jnp.asarray(np.stack(masks).astype(np.float32)), jnp.asarray(hm)


def _chunk_terms(lw, kd, b, v, kk, r, msk_ref, hm, rev):
    c_len = lw.shape[0]
    base = 5 if rev else 0
    strict, incl = msk_ref[base + 0], msk_ref[base + 1]
    d16, l1, l2 = msk_ref[base + 2], msk_ref[base + 3], msk_ref[base + 4]
    bd, eye = msk_ref[10], msk_ref[11]
    stack4 = lambda z: jnp.concatenate([z * hm[j:j + 1] for j in range(4)], axis=0)
    tile4 = lambda z: jnp.concatenate([z, z, z, z], axis=0)
    fold4 = lambda z: z[0:c_len] + z[c_len:2 * c_len] + z[2 * c_len:3 * c_len] + z[3 * c_len:4 * c_len]

    c = _dotx(incl[0:c_len, 0:c_len], lw)
    cex = c - lw
    ctot = c[0:1] if rev else c[c_len - 1:c_len]
    en = jnp.exp(-c)
    khat = kk * jnp.exp(cex)
    rhat = r * jnp.exp(c)
    bch = b * en
    kch = kd * en
    et = jnp.exp(ctot)
    btil = bch * et
    ktil = kch * et
    kh4, rh4, v4 = stack4(khat), stack4(rhat), stack4(v)
    n4 = 4 * c_len
    sc = _dotb(jnp.concatenate([kh4, rh4], axis=0),
               jnp.concatenate([tile4(bch), tile4(kch)], axis=0), NT)
    a_bk = sc[0:n4, 0:n4] * strict
    a_kk = sc[0:n4, n4:2 * n4] * strict
    b_br = sc[n4:2 * n4, 0:n4] * incl
    b_kr = sc[n4:2 * n4, n4:2 * n4] * incl
    ad = a_bk * d16
    a2 = _dotb(ad, ad)
    a4 = _dotb(a2, a2)
    a8 = _dotb(a4, a4)
    dinv = _dotb(_dotb(_dotb(eye - ad, eye + a2), eye + a4), eye + a8)
    b1 = dinv - _dotb(_dotb(dinv, a_bk * l1), dinv)
    tinv = b1 - _dotb(_dotb(b1, a_bk * l2), b1)
    av4 = _dotb(a_kk, v4)
    wu = _dotb(tinv, jnp.concatenate([kh4, av4], axis=1))
    p1 = _dotb(b_br, wu)
    p2 = _dotb(b_kr, v4)
    q = wu.shape[1] // 2
    wy = fold4(rh4 - p1[:, 0:q])
    uy = fold4(p2 - p1[:, q:2 * q])
    w1 = fold4(wu[:, 0:q])
    u = fold4(wu[:, q:2 * q])
    m = eye * et - bd * _dotb(btil, w1, TN)
    o = bd * (_dotb(ktil, v, TN) - _dotb(btil, u, TN))
    return wy, uy, m, o


def _scan_kernel(lwf, kdf, bf, vf, kkf, rf, lwb, kdb, bb, vb, kkb, rb, msk_ref, hm_ref,
                 yf_ref, yb_ref, st_ref):
    @pl.when(pl.program_id(2) == 0)
    def _():
        st_ref[...] = jnp.zeros_like(st_ref)

    hm = hm_ref[...]
    n_chunks = lwf.shape[1] // CHUNK
    dirs = ((0, False, (lwf, kdf, bf, vf, kkf, rf), yf_ref), (1, True, (lwb, kdb, bb, vb, kkb, rb), yb_ref))
    for d, rev, refs, y_ref in dirs:
        st = st_ref[d]
        for ci in (reversed(range(n_chunks)) if rev else range(n_chunks)):
            sl = slice(ci * CHUNK, (ci + 1) * CHUNK)
            wy, uy, m, o = _chunk_terms(*[ref[0, sl, :] for ref in refs], msk_ref, hm, rev)
            y_ref[0, sl, :] = _dotx(wy, st) + uy
            st = _dotx(m, st) + o
        st_ref[d] = st


def _scan(feats, n_ctx_tiles):
    lw0, lw1, kd0, kd1, b0, b1, v, kk, r = feats
    nb, s, d = v.shape
    t = TOK_TILE
    nt = s // t
    msk, hm = _scan_masks()
    fwd = pl.BlockSpec((1, t, QUAD), lambda b, q, i: (b, i, q))
    bwd_idx = lambda i: jnp.where(i < n_ctx_tiles, n_ctx_tiles - 1 - i, nt - 1 - (i - n_ctx_tiles))
    bwd = pl.BlockSpec((1, t, QUAD), lambda b, q, i: (b, bwd_idx(i), q))
    return pl.pallas_call(
        _scan_kernel,
        grid=(nb, d // QUAD, nt),
        in_specs=[fwd] * 6 + [bwd] * 6 + [pl.BlockSpec(msk.shape, lambda b, q, i: (0, 0, 0)),
                                          pl.BlockSpec(hm.shape, lambda b, q, i: (0, 0))],
        out_specs=[fwd, bwd],
        out_shape=[jax.ShapeDtypeStruct((nb, s, d), F32)] * 2,
        scratch_shapes=[pltpu.VMEM((2, QUAD, QUAD), F32)],
        compiler_params=_params("parallel", "parallel", "arbitrary"),
        name="scan",
    )(lw0, kd0, b0, v, kk, r, lw1, kd1, b1, v, kk, r, msk, hm)


def _readout_kernel(yf_ref, yb_ref, r_ref, kd0_ref, kd1_ref, v_ref, g_ref, x_ref, mod_ref, vec_ref,
                    wo_ref, hd_ref, hdt_ref, rwt_ref, rb_ref, u_ref,
                    x3_ref, h2_ref, pos_ref, gd_ref, off_ref, *, tiles_per_moe):
    i = pl.program_id(0)
    vec = vec_ref[...]
    mod = mod_ref[0]
    headsum = lambda z: _dotx(_dotx(z, hd_ref[...]), hdt_ref[...])
    inv_k = 1.0 / RWKV_HEAD
    y = yf_ref[0] + yb_ref[0]
    yc = y - headsum(y) * inv_k
    var = headsum(yc * yc) * inv_k
    yn = yc * lax.rsqrt(var + GN_EPS) * vec[0:1] + vec[1:2]
    bonus = headsum(r_ref[0] * (kd0_ref[0] + kd1_ref[0]) * vec[2:3]) * v_ref[0]
    out = (yn + bonus) * g_ref[0]
    x3 = x_ref[...] + mod[2:3] * _dotb(out, wo_ref[...])
    x3_ref[...] = x3
    _route(x3, mod, vec[3:4], rwt_ref, rb_ref, u_ref, off_ref, h2_ref, pos_ref, gd_ref,
           i % tiles_per_moe == 0)


def _readout(yf, yb, r, kd0, kd1, v, g, x2d, mod, vec, wo, hd, hdt, rwt, rb, n_ctx_tiles, moe_tile,
             tokens_per_batch):
    ntok, d = x2d.shape
    t = TOK_TILE
    n_e = rwt.shape[0]
    tpb = tokens_per_batch // t
    seq = pl.BlockSpec((1, t, d), lambda i: (i // tpb, n_ctx_tiles + i % tpb, 0))
    tokspec = pl.BlockSpec((t, d), lambda i: (i, 0))
    full = lambda a: pl.BlockSpec(a.shape, lambda i: (0,) * a.ndim)
    u = _route_consts(t)
    kern = functools.partial(_readout_kernel, tiles_per_moe=moe_tile // t)
    return pl.pallas_call(
        kern,
        grid=(ntok // t,),
        in_specs=[seq] * 7 + [tokspec, pl.BlockSpec((1, 8, d), lambda i: (i // tpb, 0, 0)),
                              full(vec), full(wo), full(hd), full(hdt), full(rwt), full(rb), full(u)],
        out_specs=[tokspec, tokspec,
                   pl.BlockSpec((n_e, t), lambda i: (0, i)),
                   pl.BlockSpec((n_e, t), lambda i: (0, i))],
        out_shape=[jax.ShapeDtypeStruct((ntok, d), F32),
                   jax.ShapeDtypeStruct((ntok, d), BF16),
                   jax.ShapeDtypeStruct((n_e, ntok), jnp.int32),
                   jax.ShapeDtypeStruct((n_e, ntok), F32)],
        scratch_shapes=[pltpu.VMEM((n_e, 128), F32)],
        compiler_params=_params("arbitrary"),
        name="readout",
    )(yf, yb, r, kd0, kd1, v, g, x2d, mod, vec, wo, hd, hdt, rwt, rb, u)


def _final_kernel(x_ref, ft_ref, mod_ref, g_ref, o_ref):
    x = x_ref[...] + mod_ref[0][5:6] * jnp.transpose(ft_ref[...])
    o_ref[...] = _rms(x) * g_ref[0:1]


def _final(x2d, ft, mod, gvec, tokens_per_batch):
    ntok, d = x2d.shape
    t = TOK_TILE
    tpb = tokens_per_batch // t
    return pl.pallas_call(
        _final_kernel,
        grid=(ntok // t,),
        in_specs=[pl.BlockSpec((t, d), lambda i: (i, 0)),
                  pl.BlockSpec((d, t), lambda i: (0, i)),
                  pl.BlockSpec((1, 8, d), lambda i: (i // tpb, 0, 0)),
                  pl.BlockSpec((8, d), lambda i: (0, 0))],
        out_specs=pl.BlockSpec((t, d), lambda i: (i, 0)),
        out_shape=jax.ShapeDtypeStruct((ntok, d), F32),
        compiler_params=_params("parallel"),
        name="final",
    )(x2d, ft, mod, gvec)


def _rows8(rows, d):
    n = -(-len(rows) // 8) * 8
    out = jnp.zeros((n, d), F32)
    return out.at[:len(rows)].set(jnp.stack([jnp.asarray(r, F32) for r in rows]))


def _pad_lora(w, total):
    two, r, d = w.shape
    out = jnp.zeros((two, total, d), w.dtype)
    for dd in range(two):
        out = out.at[dd, dd * r:(dd + 1) * r].set(w[dd])
    return out


def kernel(x, c, ctx, c_ctx, norm_g, ada_w, ada_b, pool_w, pool_ls, rwkv_mu, rwkv_w_rkv, rwkv_w0, rwkv_w1, rwkv_w2, rwkv_a0, rwkv_a1, rwkv_a2, rwkv_g1, rwkv_g2, rwkv_k_k, rwkv_k_a, rwkv_r_k, rwkv_ln_g, rwkv_ln_b, rwkv_w_o, moe_router_w, moe_router_b, moe_w_gu, moe_b_gu, moe_w_down, moe_b_down, final_g):
    nb, seq, d = x.shape
    n_ctx = ctx.shape[1]
    depth = norm_g.shape[0]
    n_e = moe_router_w.shape[2]
    t = TOK_TILE
    assert depth == 2 and nb <= 7 and seq % t == 0 and n_ctx % t == 0 and t % GRID_W == 0 and d % QUAD == 0
    n_lat, n_ctxtok = nb * seq, nb * n_ctx
    moe_tile_lat = min(MOE_TILE, n_lat)
    moe_tile_ctx = min(MOE_TILE, n_ctxtok)
    assert n_lat % moe_tile_lat == 0 and n_ctxtok % moe_tile_ctx == 0
    n_ctx_tiles = n_ctx // t
    seq_total = n_ctx + seq

    cond8 = jnp.zeros((8, d), F32).at[:nb].set(c).at[nb].set(c_ctx)
    ada = _ada(cond8, ada_w, ada_b).reshape(depth, 8, N_MOD, d)
    pad8 = lambda m: jnp.pad(m, ((0, 0), (0, 8 - N_MOD), (0, 0)))
    mod_lat = [pad8(ada[i, :nb]) for i in range(depth)]
    mod_ctx = [pad8(jnp.broadcast_to(ada[i, nb][None], (nb, N_MOD, d))) for i in range(depth)]

    def expert_params(i):
        return (moe_w_gu[i].astype(BF16), moe_b_gu[i].reshape(n_e, 1, -1),
                jnp.swapaxes(moe_w_down[i], 1, 2).astype(BF16), moe_b_down[i].T)

    def router_params(i):
        return moe_router_w[i].T, moe_router_b[i].reshape(n_e, 1)

    x2d = x.reshape(n_lat, d)
    ctx2d = ctx.reshape(n_ctxtok, d)

    vec0 = _rows8([norm_g[0, 0], norm_g[0, 1], pool_ls[0]], d)
    pw = pool_w[0].astype(BF16)
    rwt0, rb0 = router_params(0)
    ex0 = expert_params(0)
    x1, h2, pos, gd = _pool_layer(x2d, mod_lat[0], vec0, pw, rwt0, rb0, GRID_W, moe_tile_lat, seq)
    ft_lat = _moe(h2, pos, gd, *ex0, moe_tile_lat)
    c1, h2c, posc, gdc = _pool_layer(ctx2d, mod_ctx[0], vec0, pw, rwt0, rb0, n_ctx, moe_tile_ctx, n_ctx)
    ft_ctx = _moe(h2c, posc, gdc, *ex0, moe_tile_ctx)

    vec1 = _rows8([norm_g[1, 0]], d)
    x2, h_all = _respre(x1, ft_lat, mod_lat[0], mod_lat[1], vec1, None, seq, n_ctx_tiles, seq_total)
    _, h_all = _respre(c1, ft_ctx, mod_ctx[0], mod_ctx[1], vec1, h_all, n_ctx, 0, seq_total)

    fv = _rows8([rwkv_mu[0, m] for m in range(6)]
                + [rwkv_w0[0, 0], rwkv_w0[0, 1], rwkv_a0[0, 0], rwkv_a0[0, 1], rwkv_k_k[0], rwkv_k_a[0]], d)
    lora = rwkv_w1.shape[3]
    cat2 = lambda w: jnp.concatenate([w[0, 0], w[0, 1]], axis=1).astype(BF16)
    hd, hdt = _head_indicators(d)
    feats = _features(h_all, fv, rwkv_w_rkv[0].astype(BF16),
                      cat2(rwkv_w1), _pad_lora(rwkv_w2[0], 2 * lora).astype(BF16),
                      cat2(rwkv_a1), _pad_lora(rwkv_a2[0], 2 * rwkv_a1.shape[3]).astype(BF16),
                      rwkv_g1[0].astype(BF16), rwkv_g2[0].astype(BF16), hd, hdt, n_ctx_tiles)
    lw0, lw1, kd0, kd1, b0, b1, v, kk, r, g = feats
    yf, yb = _scan((lw0, lw1, kd0, kd1, b0, b1, v, kk, r), n_ctx_tiles)

    vec_ro = _rows8([rwkv_ln_g[0], rwkv_ln_b[0], rwkv_r_k[0].reshape(-1), norm_g[1, 1]], d)
    rwt1, rb1 = router_params(1)
    x3, h2, pos, gd = _readout(yf, yb, r, kd0, kd1, v, g, x2, mod_lat[1], vec_ro,
                               rwkv_w_o[0].astype(BF16), hd, hdt, rwt1, rb1, n_ctx_tiles, moe_tile_lat, seq)
    ft = _moe(h2, pos, gd, *expert_params(1), moe_tile_lat)
    out = _final(x3, ft, mod_lat[1], _rows8([final_g], d), seq)
    return out.reshape(nb, seq, d)
```

```python
import functools

import numpy as np
import jax
import jax.numpy as jnp
from jax import lax
from jax.experimental import pallas as pl
from jax.experimental.pallas import tpu as pltpu

F32 = jnp.float32
BF16 = jnp.bfloat16
NN = (((1,), (0,)), ((), ()))
NT = (((1,), (1,)), ((), ()))
TN = (((0,), (0,)), ((), ()))

N_MOD = 6
NORM_EPS = 1e-6
GRID_W = 64
POOL_WINDOWS = (2, 4, 8, 16)
RWKV_HEAD = 64
GN_EPS = 64e-5
TOP_K = 4
SWIGLU_ALPHA = 1.702
SWIGLU_LIMIT = 7.0

TOK_TILE = 256
MOE_TILE = 2048
MOE_ROWS = 256
MOE_CHUNK_BLOCKS = 4
CHUNK = 64
GROUP_HEADS = 2
GROUP = GROUP_HEADS * RWKV_HEAD
HEAD_PAD = 128
VMEM_LIMIT_BYTES = 56 * 1024 * 1024


def _dotx(a, b, dims=NN):
    return lax.dot_general(a, b, dims, precision=lax.Precision.HIGHEST, preferred_element_type=F32)


def _dotb(a, b, dims=NN):
    return lax.dot_general(a.astype(BF16), b.astype(BF16), dims, preferred_element_type=F32)


def _params(*sem):
    return pltpu.CompilerParams(dimension_semantics=sem, vmem_limit_bytes=VMEM_LIMIT_BYTES)


def _rms(x):
    return x * lax.rsqrt(jnp.mean(x * x, axis=-1, keepdims=True) + NORM_EPS)


def _sigmoid(x):
    return 1.0 / (1.0 + jnp.exp(-x))


def _ada_kernel(c_ref, w_ref, b_ref, o_ref):
    c = c_ref[...]
    o_ref[0] = _dotx(c * _sigmoid(c), w_ref[0]) + b_ref[0]


def _ada(cond8, ada_w, ada_b):
    depth, d, nd = ada_w.shape
    return pl.pallas_call(
        _ada_kernel,
        grid=(depth, nd // d),
        in_specs=[pl.BlockSpec((8, d), lambda i, n: (0, 0)),
                  pl.BlockSpec((1, d, d), lambda i, n: (i, 0, n)),
                  pl.BlockSpec((1, 1, d), lambda i, n: (i, 0, n))],
        out_specs=pl.BlockSpec((1, 8, d), lambda i, n: (i, 0, n)),
        out_shape=jax.ShapeDtypeStruct((depth, 8, nd), F32),
        compiler_params=_params("parallel", "parallel"),
        name="ada",
    )(cond8, ada_w, ada_b.reshape(depth, 1, nd))


def _route(x_new, mod, g2, rwt_ref, rb_ref, u_ref, off_ref, h2_ref, pos_ref, gd_ref, reset):
    h2 = _rms(x_new) * g2 * (1.0 + mod[4:5]) + mod[3:4]
    h2_ref[...] = h2.astype(BF16)
    logits = _dotx(rwt_ref[...], h2, NT) + rb_ref[...]
    n_e = logits.shape[0]
    eio = lax.broadcasted_iota(jnp.int32, logits.shape, 0).astype(F32)
    live = logits
    sels, vals = [], []
    for _ in range(TOP_K):
        m = jnp.max(live, axis=0, keepdims=True)
        idx = jnp.min(jnp.where(live == m, eio, float(n_e)), axis=0, keepdims=True)
        sel = eio == idx
        sels.append(sel)
        vals.append(m)
        live = jnp.where(sel, -jnp.inf, live)
    exps = [jnp.exp(v - vals[0]) for v in vals]
    inv = 1.0 / (exps[0] + exps[1] + exps[2] + exps[3])
    gd = jnp.zeros_like(logits)
    maskf = jnp.zeros_like(logits)
    for k in range(TOP_K):
        gd = jnp.where(sels[k], exps[k] * inv, gd)
        maskf = jnp.where(sels[k], 1.0, maskf)

    @pl.when(reset)
    def _():
        off_ref[...] = jnp.zeros_like(off_ref)

    incl = _dotb(maskf, u_ref[...])
    off = off_ref[:, 0:1]
    pos_ref[...] = jnp.where(maskf > 0.5, off + incl - 1.0, -1.0).astype(jnp.int32)
    gd_ref[...] = gd
    off_ref[...] = off_ref[...] + jnp.sum(maskf, axis=1, keepdims=True)


def _route_consts(tile):
    s = np.arange(tile)
    return jnp.asarray((s[:, None] <= s[None, :]).astype(np.float32), BF16)


def _pool_kernel(x_ref, mod_ref, vec_ref, pm_ref, pw_ref, rwt_ref, rb_ref, u_ref,
                 x1_ref, h2_ref, pos_ref, gd_ref, off_ref, *, tiles_per_moe):
    i = pl.program_id(0)
    x = x_ref[...]
    mod = mod_ref[0]
    vec = vec_ref[...]
    h = _rms(x) * vec[0:1] * (1.0 + mod[1:2]) + mod[0:1]
    gw = pw_ref.shape[1]
    ys = []
    for g in range(len(POOL_WINDOWS)):
        d = _dot3(pm_ref[g], h[:, g * gw:(g + 1) * gw])
        ys.append(_dotb(d, pw_ref[g]))
    y = jnp.concatenate(ys, axis=1) * vec[2:3]
    x1 = x + mod[2:3] * y
    x1_ref[...] = x1
    _route(x1, mod, vec[1:2], rwt_ref, rb_ref, u_ref, off_ref, h2_ref, pos_ref, gd_ref,
           i % tiles_per_moe == 0)


def _pool_matrices(tile, row_len):
    p = np.arange(tile)
    pp = p % row_len
    out = []
    for win in POOL_WINDOWS:
        lo = np.clip(pp - win // 2, 0, row_len - 1)
        hi = np.clip(pp + win // 2 - 1, 0, row_len - 1)
        cnt = (hi - lo + 1).astype(np.float64)
        same = (p[:, None] // row_len) == (p[None, :] // row_len)
        inwin = same & (pp[None, :] >= lo[:, None]) & (pp[None, :] <= hi[:, None])
        out.append(inwin / cnt[:, None] - np.eye(tile))
    return jnp.asarray(np.stack(out).astype(np.float32))


def _pool_layer(x2d, mod, vec, pool_w, rwt, rb, row_len, moe_tile, tokens_per_batch):
    ntok, d = x2d.shape
    t = TOK_TILE
    n_e = rwt.shape[0]
    tiles_per_batch = tokens_per_batch // t
    pm = _pool_matrices(t, row_len)
    kern = functools.partial(_pool_kernel, tiles_per_moe=moe_tile // t)
    const2 = lambda i: (0, 0)
    const3 = lambda i: (0, 0, 0)
    return pl.pallas_call(
        kern,
        grid=(ntok // t,),
        in_specs=[pl.BlockSpec((t, d), lambda i: (i, 0)),
                  pl.BlockSpec((1, 8, d), lambda i: (i // tiles_per_batch, 0, 0)),
                  pl.BlockSpec((8, d), const2),
                  pl.BlockSpec(pm.shape, const3),
                  pl.BlockSpec(pool_w.shape, const3),
                  pl.BlockSpec(rwt.shape, const2),
                  pl.BlockSpec(rb.shape, const2),
                  pl.BlockSpec((t, t), const2)],
        out_specs=[pl.BlockSpec((t, d), lambda i: (i, 0)),
                   pl.BlockSpec((t, d), lambda i: (i, 0)),
                   pl.BlockSpec((n_e, t), lambda i: (0, i)),
                   pl.BlockSpec((n_e, t), lambda i: (0, i))],
        out_shape=[jax.ShapeDtypeStruct((ntok, d), F32),
                   jax.ShapeDtypeStruct((ntok, d), BF16),
                   jax.ShapeDtypeStruct((n_e, ntok), jnp.int32),
                   jax.ShapeDtypeStruct((n_e, ntok), F32)],
        scratch_shapes=[pltpu.VMEM((n_e, 128), F32)],
        compiler_params=_params("arbitrary"),
        name="pool",
    )(x2d, mod, vec, pm, pool_w, rwt, rb, _route_consts(t))


def _one_hot_rows(pos_ref, e, r0, n_rows):
    prow = pos_ref[pl.ds(e, 1), :]
    rows = lax.broadcasted_iota(jnp.int32, (n_rows, prow.shape[1]), 0) + r0
    return jnp.where(prow == rows, 1.0, 0.0).astype(BF16)


def _expert_kernel(tile_ref, e_ref, r0_ref, flag_ref, oidx_ref,
                   pos_ref, gd_ref, x_ref, wgu_ref, bgu_ref, wdt_ref, o_ref):
    b = pl.program_id(0)
    flag = flag_ref[b]
    f = wdt_ref.shape[2]
    n_rows = o_ref.shape[1]

    @pl.when(flag == 2)
    def _():
        o_ref[...] = jnp.zeros_like(o_ref)

    @pl.when(flag == 1)
    def _():
        e = e_ref[b]
        oh = _one_hot_rows(pos_ref, e, r0_ref[b], n_rows)
        grow = gd_ref[pl.ds(e, 1), :]
        g_hi = grow.astype(BF16)
        g_mid = (grow - g_hi.astype(F32)).astype(BF16)
        g_lo = (grow - g_hi.astype(F32) - g_mid.astype(F32)).astype(BF16)
        g3 = jnp.concatenate([g_hi, g_mid, g_lo, jnp.zeros((5, grow.shape[1]), BF16)], axis=0)
        xg = jnp.dot(oh, x_ref[...], preferred_element_type=F32).astype(BF16)
        gu = jnp.dot(xg, wgu_ref[0], preferred_element_type=F32) + bgu_ref[0]
        gate = jnp.minimum(gu[:, :f], SWIGLU_LIMIT)
        up = jnp.clip(gu[:, f:], -SWIGLU_LIMIT, SWIGLU_LIMIT)
        act = (up + 1.0) * (gate * _sigmoid(SWIGLU_ALPHA * gate))
        yt = lax.dot_general(wdt_ref[0], act.astype(BF16), NT, preferred_element_type=F32)
        g8 = lax.dot_general(g3, oh, NT, preferred_element_type=F32)
        o_ref[...] = (yt * (g8[0:1] + g8[1:2] + g8[2:3])).astype(BF16)


def _combine_kernel(e_ref, r0_ref, cs_ref, nch_ref, pos_ref, gd_ref, yt_ref, bdt_ref, o_ref):
    j = pl.program_id(0)
    c = pl.program_id(1)
    n_rows = yt_ref.shape[1] // MOE_CHUNK_BLOCKS

    @pl.when(c == 0)
    def _():
        o_ref[...] = _dotx(bdt_ref[...], gd_ref[...])

    @pl.when(c < nch_ref[j])
    def _():
        b0 = (cs_ref[j] + c) * MOE_CHUNK_BLOCKS
        oh = jnp.concatenate([_one_hot_rows(pos_ref, e_ref[b0 + q], r0_ref[b0 + q], n_rows)
                              for q in range(MOE_CHUNK_BLOCKS)], axis=0)
        o_ref[...] += jnp.dot(yt_ref[...], oh, preferred_element_type=F32)


def _moe_tables(pos, ts):
    n_e, ntok = pos.shape
    nt = ntok // ts
    rb, cb = MOE_ROWS, MOE_CHUNK_BLOCKS
    cnt = jnp.sum((pos.reshape(n_e, nt, ts) >= 0).astype(jnp.int32), axis=-1).T
    nb_je = (cnt + rb - 1) // rb
    ce = jnp.cumsum(nb_je, axis=1)
    nb_j = ce[:, -1]
    nbp_j = (nb_j + cb - 1) // cb * cb
    ends = jnp.cumsum(nbp_j)
    bstart = ends - nbp_j
    total = ends[-1]
    cmax = (ts * TOP_K // rb + n_e + cb - 1) // cb
    nbmax = nt * cb * cmax
    b = jnp.arange(nbmax, dtype=jnp.int32)
    bc = jnp.minimum(b, total - 1)
    tile = jnp.sum((bc[:, None] >= ends[None, :]).astype(jnp.int32), axis=1)
    local = bc - bstart[tile]
    ce_t = ce[tile]
    e = jnp.minimum(jnp.sum((local[:, None] >= ce_t).astype(jnp.int32), axis=1), n_e - 1)
    real = local < nb_j[tile]
    first = jnp.take_along_axis(ce_t - nb_je[tile], e[:, None], axis=1)[:, 0]
    r0 = jnp.where(real, (local - first) * rb, 1 << 20)
    flag = jnp.where(b >= total, 0, jnp.where(real, 1, 2))
    i32 = lambda a: a.astype(jnp.int32)
    return (i32(tile), i32(e), i32(r0), i32(flag), i32(bc), i32(bstart // cb), i32(nbp_j // cb)), cmax, nbmax


def _moe(h2, pos, gd, wgu, bgu, wdt, bdt, ts):
    ntok, d = h2.shape
    n_e = pos.shape[0]
    f2 = wgu.shape[2]
    nt = ntok // ts
    rb, cb = MOE_ROWS, MOE_CHUNK_BLOCKS
    (tile, e, r0, flag, oidx, cs, nch), cmax, nbmax = _moe_tables(pos, ts)
    yt = pl.pallas_call(
        _expert_kernel,
        grid_spec=pltpu.PrefetchScalarGridSpec(
            num_scalar_prefetch=5,
            grid=(nbmax,),
            in_specs=[pl.BlockSpec((n_e, ts), lambda b, tl, ee, rr, fl, oi: (0, tl[b])),
                      pl.BlockSpec((n_e, ts), lambda b, tl, ee, rr, fl, oi: (0, tl[b])),
                      pl.BlockSpec((ts, d), lambda b, tl, ee, rr, fl, oi: (tl[b], 0)),
                      pl.BlockSpec((1, d, f2), lambda b, tl, ee, rr, fl, oi: (ee[b], 0, 0)),
                      pl.BlockSpec((1, 1, f2), lambda b, tl, ee, rr, fl, oi: (ee[b], 0, 0)),
                      pl.BlockSpec((1, d, f2 // 2), lambda b, tl, ee, rr, fl, oi: (ee[b], 0, 0))],
            out_specs=pl.BlockSpec((d, rb), lambda b, tl, ee, rr, fl, oi: (0, oi[b])),
        ),
        out_shape=jax.ShapeDtypeStruct((d, nbmax * rb), BF16),
        compiler_params=_params("arbitrary"),
        name="expert",
    )(tile, e, r0, flag, oidx, pos, gd, h2, wgu, bgu, wdt)
    chunk_idx = lambda j, c, ee, rr, cs_, nch_: (0, cs_[j] + jnp.minimum(c, nch_[j] - 1))
    return pl.pallas_call(
        _combine_kernel,
        grid_spec=pltpu.PrefetchScalarGridSpec(
            num_scalar_prefetch=4,
            grid=(nt, cmax),
            in_specs=[pl.BlockSpec((n_e, ts), lambda j, c, ee, rr, cs_, nch_: (0, j)),
                      pl.BlockSpec((n_e, ts), lambda j, c, ee, rr, cs_, nch_: (0, j)),
                      pl.BlockSpec((d, cb * rb), chunk_idx),
                      pl.BlockSpec((d, n_e), lambda j, c, ee, rr, cs_, nch_: (0, 0))],
            out_specs=pl.BlockSpec((d, ts), lambda j, c, ee, rr, cs_, nch_: (0, j),
                                   pipeline_mode=pl.Buffered(1)),
        ),
        out_shape=jax.ShapeDtypeStruct((d, ntok), F32),
        compiler_params=_params("arbitrary", "arbitrary"),
        name="combine",
    )(e, r0, cs, nch, pos, gd, yt, bdt)


def _respre_kernel(x_ref, ft_ref, modp_ref, modc_ref, vec_ref, *rest):
    x2_ref, h_ref = rest[-2], rest[-1]
    x2 = x_ref[...] + modp_ref[0][5:6] * jnp.transpose(ft_ref[...])
    x2_ref[...] = x2
    modc = modc_ref[0]
    h_ref[0] = _rms(x2) * vec_ref[0:1] * (1.0 + modc[1:2]) + modc[0:1]


def _respre(x2d, ft, modp, modc, vec, h_all, tokens_per_batch, tile_offset, seq_total):
    ntok, d = x2d.shape
    t = TOK_TILE
    nb = modp.shape[0]
    tpb = tokens_per_batch // t
    in_specs = [pl.BlockSpec((t, d), lambda i: (i, 0)),
                pl.BlockSpec((d, t), lambda i: (0, i)),
                pl.BlockSpec((1, 8, d), lambda i: (i // tpb, 0, 0)),
                pl.BlockSpec((1, 8, d), lambda i: (i // tpb, 0, 0)),
                pl.BlockSpec((8, d), lambda i: (0, 0))]
    args = [x2d, ft, modp, modc, vec]
    aliases = {}
    if h_all is not None:
        in_specs.append(pl.BlockSpec(memory_space=pl.ANY))
        args.append(h_all)
        aliases = {5: 1}
    return pl.pallas_call(
        _respre_kernel,
        grid=(ntok // t,),
        in_specs=in_specs,
        out_specs=[pl.BlockSpec((t, d), lambda i: (i, 0)),
                   pl.BlockSpec((1, t, d), lambda i: (i // tpb, tile_offset + i % tpb, 0))],
        out_shape=[jax.ShapeDtypeStruct((ntok, d), F32),
                   jax.ShapeDtypeStruct((nb, seq_total, d), F32)],
        input_output_aliases=aliases,
        compiler_params=_params("parallel"),
        name="respre",
    )(*args)


def _feat_kernel(h_ref, hp_ref, hn_ref, fv_ref, wrkv_ref, w1_ref, w2_ref, a1_ref, a2_ref, g1_ref, g2_ref,
                 hd_ref, hdt_ref,
                 lw0_ref, lw1_ref, kd0_ref, kd1_ref, b0_ref, b1_ref, v_ref, kk_ref, r_ref, g_ref,
                 *, n_ctx_tiles, n_tiles):
    i = pl.program_id(1)
    h = h_ref[0]
    t = h.shape[0]
    fv = fv_ref[...]
    first = jnp.logical_or(i == 0, i == n_ctx_tiles)
    last = jnp.logical_or(i == n_ctx_tiles - 1, i == n_tiles - 1)
    prow = jnp.where(first, 0.0, hp_ref[0][7:8])
    nrow = jnp.where(last, 0.0, hn_ref[0][0:1])
    rio = lax.broadcasted_iota(jnp.int32, h.shape, 0)
    hdn = jnp.where(rio == 0, prow, pltpu.roll(h, 1, 0))
    hup = jnp.where(rio == t - 1, nrow, pltpu.roll(h, t - 1, 0))
    xx = 0.5 * (hdn + hup) - h

    mix = lambda m: h + xx * fv[m:m + 1]
    r = _dotb(mix(0), wrkv_ref[0])
    k = _dotb(mix(2), wrkv_ref[1])
    v = _dotb(mix(3), wrkv_ref[2])
    tw = jnp.tanh(_dotb(mix(1), w1_ref[...]))
    la = _dotb(mix(4), a1_ref[...])
    g = _dotb(_sigmoid(_dotb(mix(5), g1_ref[...])), g2_ref[...])

    kkraw = k * fv[10:11]
    ss = _headsum(kkraw * kkraw, hd_ref[...], hdt_ref[...])
    kk = kkraw / jnp.maximum(jnp.sqrt(ss), 1e-12)
    k_a = fv[11:12]
    decay_scale = float(np.exp(-0.5))
    for d, (lw_ref, kd_ref, b_ref) in enumerate(((lw0_ref, kd0_ref, b0_ref), (lw1_ref, kd1_ref, b1_ref))):
        zw = fv[6 + d:7 + d] + _dotb(tw, w2_ref[d])
        lw_ref[0] = -decay_scale * _sigmoid(zw)
        icl = _sigmoid(fv[8 + d:9 + d] + _dotb(la, a2_ref[d]))
        kd_ref[0] = k * (1.0 + (icl - 1.0) * k_a)
        b_ref[0] = kk * icl
    v_ref[0] = v
    kk_ref[0] = kk
    r_ref[0] = r
    g_ref[0] = g


def _head_indicators(d):
    hd = np.zeros((d, HEAD_PAD), np.float32)
    hd[np.arange(d), np.arange(d) // RWKV_HEAD] = 1.0
    return jnp.asarray(hd), jnp.asarray(hd.T.copy())


def _features(h_all, fv, wrkv, w1c, w2p, a1c, a2p, g1, g2, hd, hdt, n_ctx_tiles):
    nb, s, d = h_all.shape
    t = TOK_TILE
    nt = s // t
    r8 = t // 8
    kern = functools.partial(_feat_kernel, n_ctx_tiles=n_ctx_tiles, n_tiles=nt)
    full = lambda a: pl.BlockSpec(a.shape, lambda b, i: (0,) * a.ndim)
    tok = pl.BlockSpec((1, t, d), lambda b, i: (b, i, 0))
    return pl.pallas_call(
        kern,
        grid=(nb, nt),
        in_specs=[tok,
                  pl.BlockSpec((1, 8, d), lambda b, i: (b, jnp.maximum(i * r8 - 1, 0), 0)),
                  pl.BlockSpec((1, 8, d), lambda b, i: (b, jnp.minimum((i + 1) * r8, s // 8 - 1), 0)),
                  full(fv), full(wrkv), full(w1c), full(w2p), full(a1c), full(a2p), full(g1), full(g2),
                  full(hd), full(hdt)],
        out_specs=[tok] * 10,
        out_shape=[jax.ShapeDtypeStruct((nb, s, d), F32)] * 10,
        compiler_params=_params("parallel", "parallel"),
        name="feat",
    )(h_all, h_all, h_all, fv, wrkv, w1c, w2p, a1c, a2p, g1, g2, hd, hdt)


def _scan_masks():
    n = GROUP_HEADS * CHUNK
    rr = np.arange(n)[:, None]
    cc = np.arange(n)[None, :]
    bd = (rr // CHUNK) == (cc // CHUNK)
    r_, c_ = rr % CHUNK, cc % CHUNK
    out = []
    for rev in (False, True):
        strict = bd & ((c_ > r_) if rev else (c_ < r_))
        incl = bd & ((c_ >= r_) if rev else (c_ <= r_))
        d16 = strict & ((r_ // 16) == (c_ // 16))
        l1 = strict & ((r_ // 32) == (c_ // 32)) & ((r_ // 16) != (c_ // 16))
        l2 = strict & ((r_ // 32) != (c_ // 32))
        out.append([strict, incl, d16, l1, l2])
    masks = [out[0][m] for m in range(5)] + [out[1][m] for m in range(5)] + [bd, rr == cc]
    hm = np.zeros((8, GROUP), np.float32)
    for j in range(GROUP_HEADS):
        hm[j, j * RWKV_HEAD:(j + 1) * RWKV_HEAD] = 1.0
    return jnp.asarray(np.stack(masks).astype(np.float32)), jnp.asarray(hm)


def _split3(a):
    a1 = a.astype(BF16)
    r1 = a - a1.astype(F32)
    a2 = r1.astype(BF16)
    return a1, a2, (r1 - a2.astype(F32)).astype(BF16)


def _dot3(a, b, dims=NN):
    a1, a2, _ = _split3(a)
    b1, b2, _ = _split3(b)
    mm = lambda p, q: lax.dot_general(p, q, dims, preferred_element_type=F32)
    return mm(a1, b1) + (mm(a1, b2) + mm(a2, b1))


def _dot01(a, ind):
    ib = ind.astype(BF16)
    a1, a2, a3 = _split3(a)
    mm = lambda p: jnp.dot(p, ib, preferred_element_type=F32)
    return mm(a1) + (mm(a2) + mm(a3))


def _headsum(z, hd, hdt):
    return _dot01(_dot01(z, hd), hdt)


def _chunk_terms(blocks, msk_ref, hm):
    c_len = blocks[0][0].shape[0]
    n4 = GROUP_HEADS * c_len
    revs = [blk[6] for blk in blocks]
    mask = lambda k: [msk_ref[(5 if rev else 0) + k] for rev in revs]
    bd, eye = msk_ref[10], msk_ref[11]
    each = lambda f, *cols: [f(*args) for args in zip(*cols)]
    stack4 = lambda z: jnp.concatenate([z * hm[j:j + 1] for j in range(GROUP_HEADS)], axis=0)
    tile4 = lambda z: jnp.concatenate([z] * GROUP_HEADS, axis=0)
    fold4 = lambda z: sum(z[j * c_len:(j + 1) * c_len] for j in range(GROUP_HEADS))
    lw, kd, b, v, kk, r = [[blk[k] for blk in blocks] for k in range(6)]

    tri = [m_[0:c_len, 0:c_len].astype(BF16) for m_ in mask(1)]
    parts = each(_split3, lw)
    cum = lambda t, p: jnp.dot(t, p, preferred_element_type=F32)
    c = each(lambda t, p: cum(t, p[0]) + (cum(t, p[1]) + cum(t, p[2])), tri, parts)
    ctot = each(lambda c_, rev: c_[0:1] if rev else c_[c_len - 1:c_len], c, revs)
    en = each(lambda c_: jnp.exp(-c_), c)
    khat = each(lambda kk_, c_, lw_: kk_ * jnp.exp(c_ - lw_), kk, c, lw)
    rhat = each(lambda r_, c_: r_ * jnp.exp(c_), r, c)
    bch = each(jnp.multiply, b, en)
    kch = each(jnp.multiply, kd, en)
    et = each(jnp.exp, ctot)
    btil = each(jnp.multiply, bch, et)
    ktil = each(jnp.multiply, kch, et)
    kh4, rh4, v4 = each(stack4, khat), each(stack4, rhat), each(stack4, v)
    sc = each(lambda k_, r_, b_, c_: _dotb(jnp.concatenate([k_, r_], axis=0),
                                           jnp.concatenate([tile4(b_), tile4(c_)], axis=0), NT),
              kh4, rh4, bch, kch)
    a_bk = each(lambda s, m_: s[0:n4, 0:n4] * m_, sc, mask(0))
    a_kk = each(lambda s, m_: s[0:n4, n4:2 * n4] * m_, sc, mask(0))
    b_br = each(lambda s, m_: s[n4:2 * n4, 0:n4] * m_, sc, mask(1))
    b_kr = each(lambda s, m_: s[n4:2 * n4, n4:2 * n4] * m_, sc, mask(1))
    av4 = each(_dotb, a_kk, v4)
    p2 = each(_dotb, b_kr, v4)
    ad = each(jnp.multiply, a_bk, mask(2))
    a2 = each(_dotb, ad, ad)
    x = each(lambda ad_, a2_: _dotb(eye - ad_, eye + a2_), ad, a2)
    a4 = each(_dotb, a2, a2)
    x = each(lambda x_, a4_: _dotb(x_, eye + a4_), x, a4)
    a8 = each(_dotb, a4, a4)
    dinv = each(lambda x_, a8_: _dotb(x_, eye + a8_), x, a8)
    t1 = each(lambda d_, a_, m_: _dotb(d_, a_ * m_), dinv, a_bk, mask(3))
    b1 = each(lambda d_, t_: d_ - _dotb(t_, d_), dinv, t1)
    t2 = each(lambda b_, a_, m_: _dotb(b_, a_ * m_), b1, a_bk, mask(4))
    tinv = each(lambda b_, t_: b_ - _dotb(t_, b_), b1, t2)
    wu = each(lambda t_, k_, a_: _dotb(t_, jnp.concatenate([k_, a_], axis=1)), tinv, kh4, av4)
    p1 = each(_dotb, b_br, wu)
    q = GROUP
    wy = each(lambda r_, p_: fold4(r_ - p_[:, 0:q]), rh4, p1)
    uy = each(lambda p2_, p_: fold4(p2_ - p_[:, q:2 * q]), p2, p1)
    w1 = each(lambda w_: fold4(w_[:, 0:q]), wu)
    u = each(lambda w_: fold4(w_[:, q:2 * q]), wu)
    m = each(lambda e_, b_, w_: eye * e_ - bd * _dotb(b_, w_, TN), et, btil, w1)
    o = each(lambda k_, v_, b_, u_: bd * (_dotb(k_, v_, TN) - _dotb(b_, u_, TN)), ktil, v, btil, u)
    return list(zip(wy, uy, m, o))


def _scan_kernel(lwf, kdf, bf, vf, kkf, rf, lwb, kdb, bb, vb, kkb, rb, msk_ref, hm_ref,
                 yf_ref, yb_ref, st_ref):
    @pl.when(pl.program_id(2) == 0)
    def _():
        st_ref[...] = jnp.zeros_like(st_ref)

    hm = hm_ref[...]
    n_chunks = lwf.shape[1] // CHUNK
    dirs = ((False, (lwf, kdf, bf, vf, kkf, rf), yf_ref), (True, (lwb, kdb, bb, vb, kkb, rb), yb_ref))
    slices = [[slice(ci * CHUNK, (ci + 1) * CHUNK) for ci in (reversed(range(n_chunks)) if rev else range(n_chunks))]
              for rev, _, _ in dirs]
    blocks = [tuple(ref[0, slices[d][k], :] for ref in refs) + (rev,)
              for k in range(n_chunks) for d, (rev, refs, _) in enumerate(dirs)]
    terms = _chunk_terms(blocks, msk_ref, hm)
    st = [st_ref[0], st_ref[1]]
    for k in range(n_chunks):
        for d, (_, _, y_ref) in enumerate(dirs):
            wy, uy, m, o = terms[k * len(dirs) + d]
            y_ref[0, slices[d][k], :] = _dot3(wy, st[d]) + uy
            st[d] = _dot3(m, st[d]) + o
    st_ref[0] = st[0]
    st_ref[1] = st[1]


def _scan(feats, n_ctx_tiles):
    lw0, lw1, kd0, kd1, b0, b1, v, kk, r = feats
    nb, s, d = v.shape
    t = TOK_TILE
    nt = s // t
    msk, hm = _scan_masks()
    fwd = pl.BlockSpec((1, t, GROUP), lambda b, q, i: (b, i, q))
    bwd_idx = lambda i: jnp.where(i < n_ctx_tiles, n_ctx_tiles - 1 - i, nt - 1 - (i - n_ctx_tiles))
    bwd = pl.BlockSpec((1, t, GROUP), lambda b, q, i: (b, bwd_idx(i), q))
    return pl.pallas_call(
        _scan_kernel,
        grid=(nb, d // GROUP, nt),
        in_specs=[fwd] * 6 + [bwd] * 6 + [pl.BlockSpec(msk.shape, lambda b, q, i: (0, 0, 0)),
                                          pl.BlockSpec(hm.shape, lambda b, q, i: (0, 0))],
        out_specs=[fwd, bwd],
        out_shape=[jax.ShapeDtypeStruct((nb, s, d), F32)] * 2,
        scratch_shapes=[pltpu.VMEM((2, GROUP, GROUP), F32)],
        compiler_params=_params("parallel", "parallel", "arbitrary"),
        name="scan",
    )(lw0, kd0, b0, v, kk, r, lw1, kd1, b1, v, kk, r, msk, hm)


def _readout_kernel(yf_ref, yb_ref, r_ref, kd0_ref, kd1_ref, v_ref, g_ref, x_ref, mod_ref, vec_ref,
                    wo_ref, hd_ref, hdt_ref, rwt_ref, rb_ref, u_ref,
                    x3_ref, h2_ref, pos_ref, gd_ref, off_ref, *, tiles_per_moe):
    i = pl.program_id(0)
    vec = vec_ref[...]
    mod = mod_ref[0]
    headsum = lambda z: _headsum(z, hd_ref[...], hdt_ref[...])
    inv_k = 1.0 / RWKV_HEAD
    y = yf_ref[0] + yb_ref[0]
    yc = y - headsum(y) * inv_k
    var = headsum(yc * yc) * inv_k
    yn = yc * lax.rsqrt(var + GN_EPS) * vec[0:1] + vec[1:2]
    bonus = headsum(r_ref[0] * (kd0_ref[0] + kd1_ref[0]) * vec[2:3]) * v_ref[0]
    out = (yn + bonus) * g_ref[0]
    x3 = x_ref[...] + mod[2:3] * _dotb(out, wo_ref[...])
    x3_ref[...] = x3
    _route(x3, mod, vec[3:4], rwt_ref, rb_ref, u_ref, off_ref, h2_ref, pos_ref, gd_ref,
           i % tiles_per_moe == 0)


def _readout(yf, yb, r, kd0, kd1, v, g, x2d, mod, vec, wo, hd, hdt, rwt, rb, n_ctx_tiles, moe_tile,
             tokens_per_batch):
    ntok, d = x2d.shape
    t = TOK_TILE
    n_e = rwt.shape[0]
    tpb = tokens_per_batch // t
    seq = pl.BlockSpec((1, t, d), lambda i: (i // tpb, n_ctx_tiles + i % tpb, 0))
    tokspec = pl.BlockSpec((t, d), lambda i: (i, 0))
    full = lambda a: pl.BlockSpec(a.shape, lambda i: (0,) * a.ndim)
    u = _route_consts(t)
    kern = functools.partial(_readout_kernel, tiles_per_moe=moe_tile // t)
    return pl.pallas_call(
        kern,
        grid=(ntok // t,),
        in_specs=[seq] * 7 + [tokspec, pl.BlockSpec((1, 8, d), lambda i: (i // tpb, 0, 0)),
                              full(vec), full(wo), full(hd), full(hdt), full(rwt), full(rb), full(u)],
        out_specs=[tokspec, tokspec,
                   pl.BlockSpec((n_e, t), lambda i: (0, i)),
                   pl.BlockSpec((n_e, t), lambda i: (0, i))],
        out_shape=[jax.ShapeDtypeStruct((ntok, d), F32),
                   jax.ShapeDtypeStruct((ntok, d), BF16),
                   jax.ShapeDtypeStruct((n_e, ntok), jnp.int32),
                   jax.ShapeDtypeStruct((n_e, ntok), F32)],
        scratch_shapes=[pltpu.VMEM((n_e, 128), F32)],
        compiler_params=_params("arbitrary"),
        name="readout",
    )(yf, yb, r, kd0, kd1, v, g, x2d, mod, vec, wo, hd, hdt, rwt, rb, u)


def _final_kernel(x_ref, ft_ref, mod_ref, g_ref, o_ref):
    x = x_ref[...] + mod_ref[0][5:6] * jnp.transpose(ft_ref[...])
    o_ref[...] = _rms(x) * g_ref[0:1]


def _final(x2d, ft, mod, gvec, tokens_per_batch):
    ntok, d = x2d.shape
    t = TOK_TILE
    tpb = tokens_per_batch // t
    return pl.pallas_call(
        _final_kernel,
        grid=(ntok // t,),
        in_specs=[pl.BlockSpec((t, d), lambda i: (i, 0)),
                  pl.BlockSpec((d, t), lambda i: (0, i)),
                  pl.BlockSpec((1, 8, d), lambda i: (i // tpb, 0, 0)),
                  pl.BlockSpec((8, d), lambda i: (0, 0))],
        out_specs=pl.BlockSpec((t, d), lambda i: (i, 0)),
        out_shape=jax.ShapeDtypeStruct((ntok, d), F32),
        compiler_params=_params("parallel"),
        name="final",
    )(x2d, ft, mod, gvec)


def _rows8(rows, d):
    n = -(-len(rows) // 8) * 8
    out = jnp.zeros((n, d), F32)
    return out.at[:len(rows)].set(jnp.stack([jnp.asarray(r, F32) for r in rows]))


def _pad_lora(w, total):
    two, r, d = w.shape
    out = jnp.zeros((two, total, d), w.dtype)
    for dd in range(two):
        out = out.at[dd, dd * r:(dd + 1) * r].set(w[dd])
    return out


def kernel(x, c, ctx, c_ctx, norm_g, ada_w, ada_b, pool_w, pool_ls, rwkv_mu, rwkv_w_rkv, rwkv_w0, rwkv_w1, rwkv_w2, rwkv_a0, rwkv_a1, rwkv_a2, rwkv_g1, rwkv_g2, rwkv_k_k, rwkv_k_a, rwkv_r_k, rwkv_ln_g, rwkv_ln_b, rwkv_w_o, moe_router_w, moe_router_b, moe_w_gu, moe_b_gu, moe_w_down, moe_b_down, final_g):
    nb, seq, d = x.shape
    n_ctx = ctx.shape[1]
    depth = norm_g.shape[0]
    n_e = moe_router_w.shape[2]
    t = TOK_TILE
    assert depth == 2 and nb <= 7 and seq % t == 0 and n_ctx % t == 0 and t % GRID_W == 0 and d % GROUP == 0
    n_lat, n_ctxtok = nb * seq, nb * n_ctx
    moe_tile_lat = min(MOE_TILE, n_lat)
    moe_tile_ctx = min(MOE_TILE, n_ctxtok)
    assert n_lat % moe_tile_lat == 0 and n_ctxtok % moe_tile_ctx == 0
    n_ctx_tiles = n_ctx // t
    seq_total = n_ctx + seq

    cond8 = jnp.zeros((8, d), F32).at[:nb].set(c).at[nb].set(c_ctx)
    ada = _ada(cond8, ada_w, ada_b).reshape(depth, 8, N_MOD, d)
    pad8 = lambda m: jnp.pad(m, ((0, 0), (0, 8 - N_MOD), (0, 0)))
    mod_lat = [pad8(ada[i, :nb]) for i in range(depth)]
    mod_ctx = [pad8(jnp.broadcast_to(ada[i, nb][None], (nb, N_MOD, d))) for i in range(depth)]

    def expert_params(i):
        return (moe_w_gu[i].astype(BF16), moe_b_gu[i].reshape(n_e, 1, -1),
                jnp.swapaxes(moe_w_down[i], 1, 2).astype(BF16), moe_b_down[i].T)

    def router_params(i):
        return moe_router_w[i].T, moe_router_b[i].reshape(n_e, 1)

    x2d = x.reshape(n_lat, d)
    ctx2d = ctx.reshape(n_ctxtok, d)

    vec0 = _rows8([norm_g[0, 0], norm_g[0, 1], pool_ls[0]], d)
    pw = pool_w[0].astype(BF16)
    rwt0, rb0 = router_params(0)
    ex0 = expert_params(0)
    x1, h2, pos, gd = _pool_layer(x2d, mod_lat[0], vec0, pw, rwt0, rb0, GRID_W, moe_tile_lat, seq)
    ft_lat = _moe(h2, pos, gd, *ex0, moe_tile_lat)
    c1, h2c, posc, gdc = _pool_layer(ctx2d, mod_ctx[0], vec0, pw, rwt0, rb0, n_ctx, moe_tile_ctx, n_ctx)
    ft_ctx = _moe(h2c, posc, gdc, *ex0, moe_tile_ctx)

    vec1 = _rows8([norm_g[1, 0]], d)
    x2, h_all = _respre(x1, ft_lat, mod_lat[0], mod_lat[1], vec1, None, seq, n_ctx_tiles, seq_total)
    _, h_all = _respre(c1, ft_ctx, mod_ctx[0], mod_ctx[1], vec1, h_all, n_ctx, 0, seq_total)

    fv = _rows8([rwkv_mu[0, m] for m in range(6)]
                + [rwkv_w0[0, 0], rwkv_w0[0, 1], rwkv_a0[0, 0], rwkv_a0[0, 1], rwkv_k_k[0], rwkv_k_a[0]], d)
    lora = rwkv_w1.shape[3]
    cat2 = lambda w: jnp.concatenate([w[0, 0], w[0, 1]], axis=1).astype(BF16)
    hd, hdt = _head_indicators(d)
    feats = _features(h_all, fv, rwkv_w_rkv[0].astype(BF16),
                      cat2(rwkv_w1), _pad_lora(rwkv_w2[0], 2 * lora).astype(BF16),
                      cat2(rwkv_a1), _pad_lora(rwkv_a2[0], 2 * rwkv_a1.shape[3]).astype(BF16),
                      rwkv_g1[0].astype(BF16), rwkv_g2[0].astype(BF16), hd, hdt, n_ctx_tiles)
    lw0, lw1, kd0, kd1, b0, b1, v, kk, r, g = feats
    yf, yb = _scan((lw0, lw1, kd0, kd1, b0, b1, v, kk, r), n_ctx_tiles)

    vec_ro = _rows8([rwkv_ln_g[0], rwkv_ln_b[0], rwkv_r_k[0].reshape(-1), norm_g[1, 1]], d)
    rwt1, rb1 = router_params(1)
    x3, h2, pos, gd = _readout(yf, yb, r, kd0, kd1, v, g, x2, mod_lat[1], vec_ro,
                               rwkv_w_o[0].astype(BF16), hd, hdt, rwt1, rb1, n_ctx_tiles, moe_tile_lat, seq)
    ft = _moe(h2, pos, gd, *expert_params(1), moe_tile_lat)
    out = _final(x3, ft, mod_lat[1], _rows8([final_g], d), seq)
    return out.reshape(nb, seq, d)
```

```python
import functools

import numpy as np
import jax
import jax.numpy as jnp
from jax import lax
from jax.experimental import pallas as pl
from jax.experimental.pallas import tpu as pltpu

F32 = jnp.float32
BF16 = jnp.bfloat16
NN = (((1,), (0,)), ((), ()))
NT = (((1,), (1,)), ((), ()))
TN = (((0,), (0,)), ((), ()))

N_MOD = 6
NORM_EPS = 1e-6
GRID_W = 64
POOL_WINDOWS = (2, 4, 8, 16)
RWKV_HEAD = 64
GN_EPS = 64e-5
TOP_K = 4
SWIGLU_ALPHA = 1.702
SWIGLU_LIMIT = 7.0

TOK_TILE = 256
MOE_TILE = 1792
MOE_ROWS = 256
MOE_CHUNK_BLOCKS = 4
CHUNK = 64
GROUP_HEADS = 2
GROUP = GROUP_HEADS * RWKV_HEAD
HEAD_PAD = 128
VMEM_LIMIT_BYTES = 56 * 1024 * 1024


def _dotx(a, b, dims=NN):
    return lax.dot_general(a, b, dims, precision=lax.Precision.HIGHEST, preferred_element_type=F32)


def _dotb(a, b, dims=NN):
    return lax.dot_general(a.astype(BF16), b.astype(BF16), dims, preferred_element_type=F32)


def _params(*sem):
    return pltpu.CompilerParams(dimension_semantics=sem, vmem_limit_bytes=VMEM_LIMIT_BYTES)


def _padded_tokens(ntok, tile):
    return -(-ntok // tile) * tile


def _rms(x):
    return x * lax.rsqrt(jnp.mean(x * x, axis=-1, keepdims=True) + NORM_EPS)


def _sigmoid(x):
    return 1.0 / (1.0 + jnp.exp(-x))


def _ada_kernel(c_ref, w_ref, b_ref, o_ref):
    c = c_ref[...]
    o_ref[0] = _dotx(c * _sigmoid(c), w_ref[0]) + b_ref[0]


def _ada(cond8, ada_w, ada_b):
    depth, d, nd = ada_w.shape
    return pl.pallas_call(
        _ada_kernel,
        grid=(depth, nd // d),
        in_specs=[pl.BlockSpec((8, d), lambda i, n: (0, 0)),
                  pl.BlockSpec((1, d, d), lambda i, n: (i, 0, n)),
                  pl.BlockSpec((1, 1, d), lambda i, n: (i, 0, n))],
        out_specs=pl.BlockSpec((1, 8, d), lambda i, n: (i, 0, n)),
        out_shape=jax.ShapeDtypeStruct((depth, 8, nd), F32),
        compiler_params=_params("parallel", "parallel"),
        name="ada",
    )(cond8, ada_w, ada_b.reshape(depth, 1, nd))


def _route(x_new, mod, g2, rwt_ref, rb_ref, u_ref, off_ref, h2_ref, pos_ref, gd_ref, reset, valid):
    h2 = _rms(x_new) * g2 * (1.0 + mod[4:5]) + mod[3:4]
    h2_ref[...] = h2.astype(BF16)
    logits = _dotx(rwt_ref[...], h2, NT) + rb_ref[...]
    n_e = logits.shape[0]
    eio = lax.broadcasted_iota(jnp.int32, logits.shape, 0).astype(F32)
    live = logits
    sels, vals = [], []
    for _ in range(TOP_K):
        m = jnp.max(live, axis=0, keepdims=True)
        idx = jnp.min(jnp.where(live == m, eio, float(n_e)), axis=0, keepdims=True)
        sel = eio == idx
        sels.append(sel)
        vals.append(m)
        live = jnp.where(sel, -jnp.inf, live)
    exps = [jnp.exp(v - vals[0]) for v in vals]
    inv = 1.0 / (exps[0] + exps[1] + exps[2] + exps[3])
    gd = jnp.zeros_like(logits)
    maskf = jnp.zeros_like(logits)
    for k in range(TOP_K):
        gd = jnp.where(sels[k], exps[k] * inv, gd)
        maskf = jnp.where(sels[k], 1.0, maskf)
    gd = jnp.where(valid, gd, 0.0)
    maskf = jnp.where(valid, maskf, 0.0)

    @pl.when(reset)
    def _():
        off_ref[...] = jnp.zeros_like(off_ref)

    incl = _dotb(maskf, u_ref[...])
    off = off_ref[:, 0:1]
    pos_ref[...] = jnp.where(maskf > 0.5, off + incl - 1.0, -1.0).astype(jnp.int32)
    gd_ref[...] = gd
    off_ref[...] = off_ref[...] + jnp.sum(maskf, axis=1, keepdims=True)


def _route_consts(tile):
    s = np.arange(tile)
    return jnp.asarray((s[:, None] <= s[None, :]).astype(np.float32), BF16)


def _pool_kernel(x_ref, mod_ref, vec_ref, pm_ref, pw_ref, rwt_ref, rb_ref, u_ref,
                 x1_ref, h2_ref, pos_ref, gd_ref, off_ref, *, tiles_per_moe, n_real_tiles):
    i = pl.program_id(0)
    x = x_ref[...]
    mod = mod_ref[0]
    vec = vec_ref[...]
    h = _rms(x) * vec[0:1] * (1.0 + mod[1:2]) + mod[0:1]
    gw = pw_ref.shape[1]
    ys = []
    for g in range(len(POOL_WINDOWS)):
        d = _dot3(pm_ref[g], h[:, g * gw:(g + 1) * gw])
        ys.append(_dotb(d, pw_ref[g]))
    y = jnp.concatenate(ys, axis=1) * vec[2:3]
    x1 = x + mod[2:3] * y
    x1_ref[...] = x1
    _route(x1, mod, vec[1:2], rwt_ref, rb_ref, u_ref, off_ref, h2_ref, pos_ref, gd_ref,
           i % tiles_per_moe == 0, i < n_real_tiles)


def _pool_matrices(tile, row_len):
    p = np.arange(tile)
    pp = p % row_len
    out = []
    for win in POOL_WINDOWS:
        lo = np.clip(pp - win // 2, 0, row_len - 1)
        hi = np.clip(pp + win // 2 - 1, 0, row_len - 1)
        cnt = (hi - lo + 1).astype(np.float64)
        same = (p[:, None] // row_len) == (p[None, :] // row_len)
        inwin = same & (pp[None, :] >= lo[:, None]) & (pp[None, :] <= hi[:, None])
        out.append(inwin / cnt[:, None] - np.eye(tile))
    return jnp.asarray(np.stack(out).astype(np.float32))


def _pool_layer(x2d, mod, vec, pool_w, rwt, rb, row_len, moe_tile, tokens_per_batch):
    ntok, d = x2d.shape
    t = TOK_TILE
    n_e = rwt.shape[0]
    tiles_per_batch = tokens_per_batch // t
    n_real = ntok // t
    n_pad = _padded_tokens(ntok, moe_tile)
    pm = _pool_matrices(t, row_len)
    kern = functools.partial(_pool_kernel, tiles_per_moe=moe_tile // t, n_real_tiles=n_real)
    const2 = lambda i: (0, 0)
    const3 = lambda i: (0, 0, 0)
    real = lambda i: jnp.minimum(i, n_real - 1)
    return pl.pallas_call(
        kern,
        grid=(n_pad // t,),
        in_specs=[pl.BlockSpec((t, d), lambda i: (real(i), 0)),
                  pl.BlockSpec((1, 8, d), lambda i: (real(i) // tiles_per_batch, 0, 0)),
                  pl.BlockSpec((8, d), const2),
                  pl.BlockSpec(pm.shape, const3),
                  pl.BlockSpec(pool_w.shape, const3),
                  pl.BlockSpec(rwt.shape, const2),
                  pl.BlockSpec(rb.shape, const2),
                  pl.BlockSpec((t, t), const2)],
        out_specs=[pl.BlockSpec((t, d), lambda i: (real(i), 0)),
                   pl.BlockSpec((t, d), lambda i: (i, 0)),
                   pl.BlockSpec((n_e, t), lambda i: (0, i)),
                   pl.BlockSpec((n_e, t), lambda i: (0, i))],
        out_shape=[jax.ShapeDtypeStruct((ntok, d), F32),
                   jax.ShapeDtypeStruct((n_pad, d), BF16),
                   jax.ShapeDtypeStruct((n_e, n_pad), jnp.int32),
                   jax.ShapeDtypeStruct((n_e, n_pad), F32)],
        scratch_shapes=[pltpu.VMEM((n_e, 128), F32)],
        compiler_params=_params("arbitrary"),
        name="pool",
    )(x2d, mod, vec, pm, pool_w, rwt, rb, _route_consts(t))


def _one_hot_rows(pos_ref, e, r0, n_rows):
    prow = pos_ref[pl.ds(e, 1), :]
    rows = lax.broadcasted_iota(jnp.int32, (n_rows, prow.shape[1]), 0) + r0
    return jnp.where(prow == rows, 1.0, 0.0).astype(BF16)


def _expert_kernel(tile_ref, e_ref, r0_ref, flag_ref, oidx_ref,
                   pos_ref, gd_ref, x_ref, wgu_ref, bgu_ref, wdt_ref, o_ref):
    b = pl.program_id(0)
    flag = flag_ref[b]
    f = wdt_ref.shape[2]
    n_rows = o_ref.shape[1]

    @pl.when(flag == 2)
    def _():
        o_ref[...] = jnp.zeros_like(o_ref)

    @pl.when(flag == 1)
    def _():
        e = e_ref[b]
        oh = _one_hot_rows(pos_ref, e, r0_ref[b], n_rows)
        grow = gd_ref[pl.ds(e, 1), :]
        g_hi = grow.astype(BF16)
        g_mid = (grow - g_hi.astype(F32)).astype(BF16)
        g_lo = (grow - g_hi.astype(F32) - g_mid.astype(F32)).astype(BF16)
        g3 = jnp.concatenate([g_hi, g_mid, g_lo, jnp.zeros((5, grow.shape[1]), BF16)], axis=0)
        xg = jnp.dot(oh, x_ref[...], preferred_element_type=F32).astype(BF16)
        gu = jnp.dot(xg, wgu_ref[0], preferred_element_type=F32) + bgu_ref[0]
        gate = jnp.minimum(gu[:, :f], SWIGLU_LIMIT)
        up = jnp.clip(gu[:, f:], -SWIGLU_LIMIT, SWIGLU_LIMIT)
        act = (up + 1.0) * (gate * _sigmoid(SWIGLU_ALPHA * gate))
        yt = lax.dot_general(wdt_ref[0], act.astype(BF16), NT, preferred_element_type=F32)
        g8 = lax.dot_general(g3, oh, NT, preferred_element_type=F32)
        o_ref[...] = (yt * (g8[0:1] + g8[1:2] + g8[2:3])).astype(BF16)


def _combine_kernel(e_ref, r0_ref, cs_ref, nch_ref, pos_ref, gd_ref, yt_ref, bdt_ref, o_ref):
    j = pl.program_id(0)
    c = pl.program_id(1)
    n_rows = yt_ref.shape[1] // MOE_CHUNK_BLOCKS

    @pl.when(c == 0)
    def _():
        o_ref[...] = _dotx(bdt_ref[...], gd_ref[...])

    @pl.when(c < nch_ref[j])
    def _():
        b0 = (cs_ref[j] + c) * MOE_CHUNK_BLOCKS
        oh = jnp.concatenate([_one_hot_rows(pos_ref, e_ref[b0 + q], r0_ref[b0 + q], n_rows)
                              for q in range(MOE_CHUNK_BLOCKS)], axis=0)
        o_ref[...] += jnp.dot(yt_ref[...], oh, preferred_element_type=F32)


def _moe_tables(pos, ts):
    n_e, ntok = pos.shape
    nt = ntok // ts
    rb, cb = MOE_ROWS, MOE_CHUNK_BLOCKS
    cnt = jnp.sum((pos.reshape(n_e, nt, ts) >= 0).astype(jnp.int32), axis=-1).T
    nb_je = (cnt + rb - 1) // rb
    ce = jnp.cumsum(nb_je, axis=1)
    nb_j = ce[:, -1]
    nbp_j = (nb_j + cb - 1) // cb * cb
    ends = jnp.cumsum(nbp_j)
    bstart = ends - nbp_j
    total = ends[-1]
    cmax = (ts * TOP_K // rb + n_e + cb - 1) // cb
    nbmax = nt * cb * cmax
    b = jnp.arange(nbmax, dtype=jnp.int32)
    bc = jnp.minimum(b, total - 1)
    tile = jnp.sum((bc[:, None] >= ends[None, :]).astype(jnp.int32), axis=1)
    local = bc - bstart[tile]
    ce_t = ce[tile]
    e = jnp.minimum(jnp.sum((local[:, None] >= ce_t).astype(jnp.int32), axis=1), n_e - 1)
    real = local < nb_j[tile]
    first = jnp.take_along_axis(ce_t - nb_je[tile], e[:, None], axis=1)[:, 0]
    r0 = jnp.where(real, (local - first) * rb, 1 << 20)
    flag = jnp.where(b >= total, 0, jnp.where(real, 1, 2))
    i32 = lambda a: a.astype(jnp.int32)
    return (i32(tile), i32(e), i32(r0), i32(flag), i32(bc), i32(bstart // cb), i32(nbp_j // cb)), cmax, nbmax


def _moe(h2, pos, gd, wgu, bgu, wdt, bdt, ts):
    ntok, d = h2.shape
    n_e = pos.shape[0]
    f2 = wgu.shape[2]
    nt = ntok // ts
    rb, cb = MOE_ROWS, MOE_CHUNK_BLOCKS
    (tile, e, r0, flag, oidx, cs, nch), cmax, nbmax = _moe_tables(pos, ts)
    yt = pl.pallas_call(
        _expert_kernel,
        grid_spec=pltpu.PrefetchScalarGridSpec(
            num_scalar_prefetch=5,
            grid=(nbmax,),
            in_specs=[pl.BlockSpec((n_e, ts), lambda b, tl, ee, rr, fl, oi: (0, tl[b])),
                      pl.BlockSpec((n_e, ts), lambda b, tl, ee, rr, fl, oi: (0, tl[b])),
                      pl.BlockSpec((ts, d), lambda b, tl, ee, rr, fl, oi: (tl[b], 0)),
                      pl.BlockSpec((1, d, f2), lambda b, tl, ee, rr, fl, oi: (ee[b], 0, 0)),
                      pl.BlockSpec((1, 1, f2), lambda b, tl, ee, rr, fl, oi: (ee[b], 0, 0)),
                      pl.BlockSpec((1, d, f2 // 2), lambda b, tl, ee, rr, fl, oi: (ee[b], 0, 0))],
            out_specs=pl.BlockSpec((d, rb), lambda b, tl, ee, rr, fl, oi: (0, oi[b])),
        ),
        out_shape=jax.ShapeDtypeStruct((d, nbmax * rb), BF16),
        compiler_params=_params("arbitrary"),
        name="expert",
    )(tile, e, r0, flag, oidx, pos, gd, h2, wgu, bgu, wdt)
    chunk_idx = lambda j, c, ee, rr, cs_, nch_: (0, cs_[j] + jnp.minimum(c, nch_[j] - 1))
    return pl.pallas_call(
        _combine_kernel,
        grid_spec=pltpu.PrefetchScalarGridSpec(
            num_scalar_prefetch=4,
            grid=(nt, cmax),
            in_specs=[pl.BlockSpec((n_e, ts), lambda j, c, ee, rr, cs_, nch_: (0, j)),
                      pl.BlockSpec((n_e, ts), lambda j, c, ee, rr, cs_, nch_: (0, j)),
                      pl.BlockSpec((d, cb * rb), chunk_idx),
                      pl.BlockSpec((d, n_e), lambda j, c, ee, rr, cs_, nch_: (0, 0))],
            out_specs=pl.BlockSpec((d, ts), lambda j, c, ee, rr, cs_, nch_: (0, j),
                                   pipeline_mode=pl.Buffered(1)),
        ),
        out_shape=jax.ShapeDtypeStruct((d, ntok), F32),
        compiler_params=_params("arbitrary", "arbitrary"),
        name="combine",
    )(e, r0, cs, nch, pos, gd, yt, bdt)


def _respre_kernel(x_ref, ft_ref, modp_ref, modc_ref, vec_ref, *rest):
    x2_ref, h_ref = rest[-2], rest[-1]
    x2 = x_ref[...] + modp_ref[0][5:6] * jnp.transpose(ft_ref[...])
    x2_ref[...] = x2
    modc = modc_ref[0]
    h_ref[0] = _rms(x2) * vec_ref[0:1] * (1.0 + modc[1:2]) + modc[0:1]


def _respre(x2d, ft, modp, modc, vec, h_all, tokens_per_batch, tile_offset, seq_total):
    ntok, d = x2d.shape
    t = TOK_TILE
    nb = modp.shape[0]
    tpb = tokens_per_batch // t
    in_specs = [pl.BlockSpec((t, d), lambda i: (i, 0)),
                pl.BlockSpec((d, t), lambda i: (0, i)),
                pl.BlockSpec((1, 8, d), lambda i: (i // tpb, 0, 0)),
                pl.BlockSpec((1, 8, d), lambda i: (i // tpb, 0, 0)),
                pl.BlockSpec((8, d), lambda i: (0, 0))]
    args = [x2d, ft, modp, modc, vec]
    aliases = {}
    if h_all is not None:
        in_specs.append(pl.BlockSpec(memory_space=pl.ANY))
        args.append(h_all)
        aliases = {5: 1}
    return pl.pallas_call(
        _respre_kernel,
        grid=(ntok // t,),
        in_specs=in_specs,
        out_specs=[pl.BlockSpec((t, d), lambda i: (i, 0)),
                   pl.BlockSpec((1, t, d), lambda i: (i // tpb, tile_offset + i % tpb, 0))],
        out_shape=[jax.ShapeDtypeStruct((ntok, d), F32),
                   jax.ShapeDtypeStruct((nb, seq_total, d), F32)],
        input_output_aliases=aliases,
        compiler_params=_params("parallel"),
        name="respre",
    )(*args)


def _feat_kernel(h_ref, hp_ref, hn_ref, fv_ref, wrkv_ref, w1_ref, w2_ref, a1_ref, a2_ref, g1_ref, g2_ref,
                 hd_ref, hdt_ref,
                 lw0_ref, lw1_ref, kd0_ref, kd1_ref, b0_ref, b1_ref, v_ref, kk_ref, r_ref, g_ref,
                 *, n_ctx_tiles, n_tiles):
    i = pl.program_id(1)
    h = h_ref[0]
    t = h.shape[0]
    fv = fv_ref[...]
    first = jnp.logical_or(i == 0, i == n_ctx_tiles)
    last = jnp.logical_or(i == n_ctx_tiles - 1, i == n_tiles - 1)
    prow = jnp.where(first, 0.0, hp_ref[0][7:8])
    nrow = jnp.where(last, 0.0, hn_ref[0][0:1])
    rio = lax.broadcasted_iota(jnp.int32, h.shape, 0)
    hdn = jnp.where(rio == 0, prow, pltpu.roll(h, 1, 0))
    hup = jnp.where(rio == t - 1, nrow, pltpu.roll(h, t - 1, 0))
    xx = 0.5 * (hdn + hup) - h

    mix = lambda m: h + xx * fv[m:m + 1]
    r = _dotb(mix(0), wrkv_ref[0])
    k = _dotb(mix(2), wrkv_ref[1])
    v = _dotb(mix(3), wrkv_ref[2])
    tw = jnp.tanh(_dotb(mix(1), w1_ref[...]))
    la = _dotb(mix(4), a1_ref[...])
    g = _dotb(_sigmoid(_dotb(mix(5), g1_ref[...])), g2_ref[...])

    kkraw = k * fv[10:11]
    ss = _headsum(kkraw * kkraw, hd_ref[...], hdt_ref[...])
    kk = kkraw / jnp.maximum(jnp.sqrt(ss), 1e-12)
    k_a = fv[11:12]
    decay_scale = float(np.exp(-0.5))
    for d, (lw_ref, kd_ref, b_ref) in enumerate(((lw0_ref, kd0_ref, b0_ref), (lw1_ref, kd1_ref, b1_ref))):
        zw = fv[6 + d:7 + d] + _dotb(tw, w2_ref[d])
        lw_ref[0] = -decay_scale * _sigmoid(zw)
        icl = _sigmoid(fv[8 + d:9 + d] + _dotb(la, a2_ref[d]))
        kd_ref[0] = k * (1.0 + (icl - 1.0) * k_a)
        b_ref[0] = kk * icl
    v_ref[0] = v
    kk_ref[0] = kk
    r_ref[0] = r
    g_ref[0] = g


def _head_indicators(d):
    hd = np.zeros((d, HEAD_PAD), np.float32)
    hd[np.arange(d), np.arange(d) // RWKV_HEAD] = 1.0
    return jnp.asarray(hd), jnp.asarray(hd.T.copy())


def _features(h_all, fv, wrkv, w1c, w2p, a1c, a2p, g1, g2, hd, hdt, n_ctx_tiles):
    nb, s, d = h_all.shape
    t = TOK_TILE
    nt = s // t
    r8 = t // 8
    kern = functools.partial(_feat_kernel, n_ctx_tiles=n_ctx_tiles, n_tiles=nt)
    full = lambda a: pl.BlockSpec(a.shape, lambda b, i: (0,) * a.ndim)
    tok = pl.BlockSpec((1, t, d), lambda b, i: (b, i, 0))
    return pl.pallas_call(
        kern,
        grid=(nb, nt),
        in_specs=[tok,
                  pl.BlockSpec((1, 8, d), lambda b, i: (b, jnp.maximum(i * r8 - 1, 0), 0)),
                  pl.BlockSpec((1, 8, d), lambda b, i: (b, jnp.minimum((i + 1) * r8, s // 8 - 1), 0)),
                  full(fv), full(wrkv), full(w1c), full(w2p), full(a1c), full(a2p), full(g1), full(g2),
                  full(hd), full(hdt)],
        out_specs=[tok] * 10,
        out_shape=[jax.ShapeDtypeStruct((nb, s, d), F32)] * 10,
        compiler_params=_params("parallel", "parallel"),
        name="feat",
    )(h_all, h_all, h_all, fv, wrkv, w1c, w2p, a1c, a2p, g1, g2, hd, hdt)


def _scan_masks():
    n = GROUP_HEADS * CHUNK
    rr = np.arange(n)[:, None]
    cc = np.arange(n)[None, :]
    bd = (rr // CHUNK) == (cc // CHUNK)
    r_, c_ = rr % CHUNK, cc % CHUNK
    out = []
    for rev in (False, True):
        strict = bd & ((c_ > r_) if rev else (c_ < r_))
        incl = bd & ((c_ >= r_) if rev else (c_ <= r_))
        d16 = strict & ((r_ // 16) == (c_ // 16))
        l1 = strict & ((r_ // 32) == (c_ // 32)) & ((r_ // 16) != (c_ // 16))
        l2 = strict & ((r_ // 32) != (c_ // 32))
        out.append([strict, incl, d16, l1, l2])
    masks = [out[0][m] for m in range(5)] + [out[1][m] for m in range(5)] + [bd, rr == cc]
    hm = np.zeros((8, GROUP), np.float32)
    for j in range(GROUP_HEADS):
        hm[j, j * RWKV_HEAD:(j + 1) * RWKV_HEAD] = 1.0
    tri3 = np.stack([np.tile(out[rev][1][0:CHUNK, 0:CHUNK], (1, 3)) for rev in (0, 1)]).astype(np.float32)
    return jnp.asarray(np.stack(masks).astype(np.float32)), jnp.asarray(tri3, BF16), jnp.asarray(hm)


def _split3(a):
    a1 = a.astype(BF16)
    r1 = a - a1.astype(F32)
    a2 = r1.astype(BF16)
    return a1, a2, (r1 - a2.astype(F32)).astype(BF16)


def _dot3(a, b, dims=NN):
    a1, a2, _ = _split3(a)
    b1, b2, _ = _split3(b)
    mm = lambda p, q: lax.dot_general(p, q, dims, preferred_element_type=F32)
    return mm(a1, b1) + (mm(a1, b2) + mm(a2, b1))


def _dot01(a, ind):
    ib = ind.astype(BF16)
    a1, a2, _ = _split3(a)
    mm = lambda p: jnp.dot(p, ib, preferred_element_type=F32)
    return mm(a1) + mm(a2)


def _headsum(z, hd, hdt):
    return _dot01(_dot01(z, hd), hdt)


def _chunk_terms(blocks, msk_ref, tri_ref, hm):
    c_len = blocks[0][0].shape[0]
    n4 = GROUP_HEADS * c_len
    revs = [blk[6] for blk in blocks]
    mask = lambda k: [msk_ref[(5 if rev else 0) + k] for rev in revs]
    bd, eye = msk_ref[10], msk_ref[11]
    each = lambda f, *cols: [f(*args) for args in zip(*cols)]
    stack4 = lambda z: jnp.concatenate([z * hm[j:j + 1] for j in range(GROUP_HEADS)], axis=0)
    tile4 = lambda z: jnp.concatenate([z] * GROUP_HEADS, axis=0)
    fold4 = lambda z: sum(z[j * c_len:(j + 1) * c_len] for j in range(GROUP_HEADS))
    lw, kd, b, v, kk, r = [[blk[k] for blk in blocks] for k in range(6)]

    tri3 = [tri_ref[1 if rev else 0] for rev in revs]
    parts = each(lambda z: jnp.concatenate(_split3(z), axis=0), lw)
    c = each(lambda t, p: jnp.dot(t, p, preferred_element_type=F32), tri3, parts)
    ctot = each(lambda c_, rev: c_[0:1] if rev else c_[c_len - 1:c_len], c, revs)
    en = each(lambda c_: jnp.exp(-c_), c)
    khat = each(lambda kk_, c_, lw_: kk_ * jnp.exp(c_ - lw_), kk, c, lw)
    rhat = each(lambda r_, c_: r_ * jnp.exp(c_), r, c)
    bch = each(jnp.multiply, b, en)
    kch = each(jnp.multiply, kd, en)
    et = each(jnp.exp, ctot)
    btil = each(jnp.multiply, bch, et)
    ktil = each(jnp.multiply, kch, et)
    kh4, rh4, v4 = each(stack4, khat), each(stack4, rhat), each(stack4, v)
    sc = each(lambda k_, r_, b_, c_: _dotb(jnp.concatenate([k_, r_], axis=0),
                                           jnp.concatenate([tile4(b_), tile4(c_)], axis=0), NT),
              kh4, rh4, bch, kch)
    a_bk = each(lambda s, m_: s[0:n4, 0:n4] * m_, sc, mask(0))
    a_kk = each(lambda s, m_: s[0:n4, n4:2 * n4] * m_, sc, mask(0))
    b_br = each(lambda s, m_: s[n4:2 * n4, 0:n4] * m_, sc, mask(1))
    b_kr = each(lambda s, m_: s[n4:2 * n4, n4:2 * n4] * m_, sc, mask(1))
    avp = each(lambda a_, b_, v_: _dotb(jnp.concatenate([a_, b_], axis=0), v_), a_kk, b_kr, v4)
    av4 = [z[0:n4] for z in avp]
    p2 = [z[n4:2 * n4] for z in avp]
    ad = each(jnp.multiply, a_bk, mask(2))
    a2 = each(_dotb, ad, ad)
    times_and_square = lambda x_, p_: _dotb(jnp.concatenate([x_, p_], axis=0), p_)
    x = [eye - ad_ for ad_ in ad]
    both = each(times_and_square, x, a2)
    x = each(lambda x_, z: x_ + z[0:n4], x, both)
    a4 = [z[n4:2 * n4] for z in both]
    both = each(times_and_square, x, a4)
    x = each(lambda x_, z: x_ + z[0:n4], x, both)
    a8 = [z[n4:2 * n4] for z in both]
    dinv = each(lambda x_, a8_: x_ + _dotb(x_, a8_), x, a8)
    t1 = each(lambda d_, a_, m_: _dotb(d_, a_ * m_), dinv, a_bk, mask(3))
    b1 = each(lambda d_, t_: d_ - _dotb(t_, d_), dinv, t1)
    t2 = each(lambda b_, a_, m_: _dotb(b_, a_ * m_), b1, a_bk, mask(4))
    tinv = each(lambda b_, t_: b_ - _dotb(t_, b_), b1, t2)
    wu = each(lambda t_, k_, a_: _dotb(t_, jnp.concatenate([k_, a_], axis=1)), tinv, kh4, av4)
    p1 = each(_dotb, b_br, wu)
    q = GROUP
    wy = each(lambda r_, p_: fold4(r_ - p_[:, 0:q]), rh4, p1)
    uy = each(lambda p2_, p_: fold4(p2_ - p_[:, q:2 * q]), p2, p1)
    w1 = each(lambda w_: fold4(w_[:, 0:q]), wu)
    u = each(lambda w_: fold4(w_[:, q:2 * q]), wu)
    om = each(lambda k_, b_, v_, u_, w_: _dotb(
        jnp.concatenate([k_, b_], axis=0),
        jnp.concatenate([jnp.concatenate([v_, jnp.zeros_like(v_)], axis=1),
                         jnp.concatenate([-u_, -w_], axis=1)], axis=0), TN), ktil, btil, v, u, w1)
    o = [bd * z[:, 0:q] for z in om]
    m = each(lambda e_, z: eye * e_ + bd * z[:, q:2 * q], et, om)
    return list(zip(wy, uy, m, o))


def _scan_kernel(lwf, kdf, bf, vf, kkf, rf, lwb, kdb, bb, vb, kkb, rb, msk_ref, tri_ref, hm_ref,
                 yf_ref, yb_ref, st_ref):
    @pl.when(pl.program_id(1) == 0)
    def _():
        st_ref[...] = jnp.zeros_like(st_ref)

    hm = hm_ref[...]
    n_batch = lwf.shape[0]
    n_chunks = lwf.shape[1] // CHUNK
    dirs = ((False, (lwf, kdf, bf, vf, kkf, rf), yf_ref), (True, (lwb, kdb, bb, vb, kkb, rb), yb_ref))
    seqs = [(rev, refs, y_ref, bi) for rev, refs, y_ref in dirs for bi in range(n_batch)]
    order = lambda rev: list(reversed(range(n_chunks))) if rev else list(range(n_chunks))
    sl = lambda rev, k: slice(order(rev)[k] * CHUNK, (order(rev)[k] + 1) * CHUNK)
    blocks = [tuple(ref[bi, sl(rev, k), :] for ref in refs) + (rev,)
              for k in range(n_chunks) for rev, refs, _, bi in seqs]
    terms = _chunk_terms(blocks, msk_ref, tri_ref, hm)
    st = [st_ref[si] for si in range(len(seqs))]
    c_len = CHUNK
    for k in range(n_chunks):
        lhs = [jnp.concatenate(terms[k * len(seqs) + si][0::2], axis=0) for si in range(len(seqs))]
        lp = [_split3(z) for z in lhs]
        sp = [_split3(z) for z in st]
        res = [jnp.dot(jnp.concatenate([l[0], l[1], l[0]], axis=1), jnp.concatenate([s[0], s[0], s[1]], axis=0),
                       preferred_element_type=F32) for l, s in zip(lp, sp)]
        n_l = c_len + GROUP
        for si, (rev, _, y_ref, bi) in enumerate(seqs):
            _, uy, _, o = terms[k * len(seqs) + si]
            y_ref[bi, sl(rev, k), :] = res[si][0:c_len] + uy
            st[si] = res[si][c_len:n_l] + o
    for si in range(len(seqs)):
        st_ref[si] = st[si]


def _scan(feats, n_ctx_tiles):
    lw0, lw1, kd0, kd1, b0, b1, v, kk, r = feats
    nb, s, d = v.shape
    t = TOK_TILE
    nt = s // t
    msk, tri3, hm = _scan_masks()
    fwd = pl.BlockSpec((nb, t, GROUP), lambda q, i: (0, i, q))
    bwd_idx = lambda i: jnp.where(i < n_ctx_tiles, n_ctx_tiles - 1 - i, nt - 1 - (i - n_ctx_tiles))
    bwd = pl.BlockSpec((nb, t, GROUP), lambda q, i: (0, bwd_idx(i), q))
    return pl.pallas_call(
        _scan_kernel,
        grid=(d // GROUP, nt),
        in_specs=[fwd] * 6 + [bwd] * 6 + [pl.BlockSpec(msk.shape, lambda q, i: (0, 0, 0)),
                                          pl.BlockSpec(tri3.shape, lambda q, i: (0, 0, 0)),
                                          pl.BlockSpec(hm.shape, lambda q, i: (0, 0))],
        out_specs=[fwd, bwd],
        out_shape=[jax.ShapeDtypeStruct((nb, s, d), F32)] * 2,
        scratch_shapes=[pltpu.VMEM((2 * nb, GROUP, GROUP), F32)],
        compiler_params=_params("parallel", "arbitrary"),
        name="scan",
    )(lw0, kd0, b0, v, kk, r, lw1, kd1, b1, v, kk, r, msk, tri3, hm)


def _readout_kernel(yf_ref, yb_ref, r_ref, kd0_ref, kd1_ref, v_ref, g_ref, x_ref, mod_ref, vec_ref,
                    wo_ref, hd_ref, hdt_ref, rwt_ref, rb_ref, u_ref,
                    x3_ref, h2_ref, pos_ref, gd_ref, off_ref, *, tiles_per_moe, n_real_tiles):
    i = pl.program_id(0)
    vec = vec_ref[...]
    mod = mod_ref[0]
    headsum = lambda z: _headsum(z, hd_ref[...], hdt_ref[...])
    inv_k = 1.0 / RWKV_HEAD
    y = yf_ref[0] + yb_ref[0]
    yc = y - headsum(y) * inv_k
    var = headsum(yc * yc) * inv_k
    yn = yc * lax.rsqrt(var + GN_EPS) * vec[0:1] + vec[1:2]
    bonus = headsum(r_ref[0] * (kd0_ref[0] + kd1_ref[0]) * vec[2:3]) * v_ref[0]
    out = (yn + bonus) * g_ref[0]
    x3 = x_ref[...] + mod[2:3] * _dotb(out, wo_ref[...])
    x3_ref[...] = x3
    _route(x3, mod, vec[3:4], rwt_ref, rb_ref, u_ref, off_ref, h2_ref, pos_ref, gd_ref,
           i % tiles_per_moe == 0, i < n_real_tiles)


def _readout(yf, yb, r, kd0, kd1, v, g, x2d, mod, vec, wo, hd, hdt, rwt, rb, n_ctx_tiles, moe_tile,
             tokens_per_batch):
    ntok, d = x2d.shape
    t = TOK_TILE
    n_e = rwt.shape[0]
    tpb = tokens_per_batch // t
    n_real = ntok // t
    n_pad = _padded_tokens(ntok, moe_tile)
    real = lambda i: jnp.minimum(i, n_real - 1)
    seq = pl.BlockSpec((1, t, d), lambda i: (real(i) // tpb, n_ctx_tiles + real(i) % tpb, 0))
    tokspec = pl.BlockSpec((t, d), lambda i: (real(i), 0))
    full = lambda a: pl.BlockSpec(a.shape, lambda i: (0,) * a.ndim)
    u = _route_consts(t)
    kern = functools.partial(_readout_kernel, tiles_per_moe=moe_tile // t, n_real_tiles=n_real)
    return pl.pallas_call(
        kern,
        grid=(n_pad // t,),
        in_specs=[seq] * 7 + [tokspec, pl.BlockSpec((1, 8, d), lambda i: (real(i) // tpb, 0, 0)),
                              full(vec), full(wo), full(hd), full(hdt), full(rwt), full(rb), full(u)],
        out_specs=[tokspec, pl.BlockSpec((t, d), lambda i: (i, 0)),
                   pl.BlockSpec((n_e, t), lambda i: (0, i)),
                   pl.BlockSpec((n_e, t), lambda i: (0, i))],
        out_shape=[jax.ShapeDtypeStruct((ntok, d), F32),
                   jax.ShapeDtypeStruct((n_pad, d), BF16),
                   jax.ShapeDtypeStruct((n_e, n_pad), jnp.int32),
                   jax.ShapeDtypeStruct((n_e, n_pad), F32)],
        scratch_shapes=[pltpu.VMEM((n_e, 128), F32)],
        compiler_params=_params("arbitrary"),
        name="readout",
    )(yf, yb, r, kd0, kd1, v, g, x2d, mod, vec, wo, hd, hdt, rwt, rb, u)


def _final_kernel(x_ref, ft_ref, mod_ref, g_ref, o_ref):
    x = x_ref[...] + mod_ref[0][5:6] * jnp.transpose(ft_ref[...])
    o_ref[...] = _rms(x) * g_ref[0:1]


def _final(x2d, ft, mod, gvec, tokens_per_batch):
    ntok, d = x2d.shape
    t = TOK_TILE
    tpb = tokens_per_batch // t
    return pl.pallas_call(
        _final_kernel,
        grid=(ntok // t,),
        in_specs=[pl.BlockSpec((t, d), lambda i: (i, 0)),
                  pl.BlockSpec((d, t), lambda i: (0, i)),
                  pl.BlockSpec((1, 8, d), lambda i: (i // tpb, 0, 0)),
                  pl.BlockSpec((8, d), lambda i: (0, 0))],
        out_specs=pl.BlockSpec((t, d), lambda i: (i, 0)),
        out_shape=jax.ShapeDtypeStruct((ntok, d), F32),
        compiler_params=_params("parallel"),
        name="final",
    )(x2d, ft, mod, gvec)


def _rows8(rows, d):
    n = -(-len(rows) // 8) * 8
    out = jnp.zeros((n, d), F32)
    return out.at[:len(rows)].set(jnp.stack([jnp.asarray(r, F32) for r in rows]))


def _pad_lora(w, total):
    two, r, d = w.shape
    out = jnp.zeros((two, total, d), w.dtype)
    for dd in range(two):
        out = out.at[dd, dd * r:(dd + 1) * r].set(w[dd])
    return out


def kernel(x, c, ctx, c_ctx, norm_g, ada_w, ada_b, pool_w, pool_ls, rwkv_mu, rwkv_w_rkv, rwkv_w0, rwkv_w1, rwkv_w2, rwkv_a0, rwkv_a1, rwkv_a2, rwkv_g1, rwkv_g2, rwkv_k_k, rwkv_k_a, rwkv_r_k, rwkv_ln_g, rwkv_ln_b, rwkv_w_o, moe_router_w, moe_router_b, moe_w_gu, moe_b_gu, moe_w_down, moe_b_down, final_g):
    nb, seq, d = x.shape
    n_ctx = ctx.shape[1]
    depth = norm_g.shape[0]
    n_e = moe_router_w.shape[2]
    t = TOK_TILE
    assert depth == 2 and nb <= 7 and seq % t == 0 and n_ctx % t == 0 and t % GRID_W == 0 and d % GROUP == 0
    n_lat, n_ctxtok = nb * seq, nb * n_ctx
    moe_tile_lat = min(MOE_TILE, n_lat)
    moe_tile_ctx = min(MOE_TILE, n_ctxtok)
    n_ctx_tiles = n_ctx // t
    seq_total = n_ctx + seq

    cond8 = jnp.zeros((8, d), F32).at[:nb].set(c).at[nb].set(c_ctx)
    ada = _ada(cond8, ada_w, ada_b).reshape(depth, 8, N_MOD, d)
    pad8 = lambda m: jnp.pad(m, ((0, 0), (0, 8 - N_MOD), (0, 0)))
    mod_lat = [pad8(ada[i, :nb]) for i in range(depth)]
    mod_ctx = [pad8(jnp.broadcast_to(ada[i, nb][None], (nb, N_MOD, d))) for i in range(depth)]

    def expert_params(i):
        return (moe_w_gu[i].astype(BF16), moe_b_gu[i].reshape(n_e, 1, -1),
                jnp.swapaxes(moe_w_down[i], 1, 2).astype(BF16), moe_b_down[i].T)

    def router_params(i):
        return moe_router_w[i].T, moe_router_b[i].reshape(n_e, 1)

    x2d = x.reshape(n_lat, d)
    ctx2d = ctx.reshape(n_ctxtok, d)

    vec0 = _rows8([norm_g[0, 0], norm_g[0, 1], pool_ls[0]], d)
    pw = pool_w[0].astype(BF16)
    rwt0, rb0 = router_params(0)
    ex0 = expert_params(0)
    x1, h2, pos, gd = _pool_layer(x2d, mod_lat[0], vec0, pw, rwt0, rb0, GRID_W, moe_tile_lat, seq)
    ft_lat = _moe(h2, pos, gd, *ex0, moe_tile_lat)
    c1, h2c, posc, gdc = _pool_layer(ctx2d, mod_ctx[0], vec0, pw, rwt0, rb0, n_ctx, moe_tile_ctx, n_ctx)
    ft_ctx = _moe(h2c, posc, gdc, *ex0, moe_tile_ctx)

    vec1 = _rows8([norm_g[1, 0]], d)
    x2, h_all = _respre(x1, ft_lat, mod_lat[0], mod_lat[1], vec1, None, seq, n_ctx_tiles, seq_total)
    _, h_all = _respre(c1, ft_ctx, mod_ctx[0], mod_ctx[1], vec1, h_all, n_ctx, 0, seq_total)

    fv = _rows8([rwkv_mu[0, m] for m in range(6)]
                + [rwkv_w0[0, 0], rwkv_w0[0, 1], rwkv_a0[0, 0], rwkv_a0[0, 1], rwkv_k_k[0], rwkv_k_a[0]], d)
    lora = rwkv_w1.shape[3]
    cat2 = lambda w: jnp.concatenate([w[0, 0], w[0, 1]], axis=1).astype(BF16)
    hd, hdt = _head_indicators(d)
    feats = _features(h_all, fv, rwkv_w_rkv[0].astype(BF16),
                      cat2(rwkv_w1), _pad_lora(rwkv_w2[0], 2 * lora).astype(BF16),
                      cat2(rwkv_a1), _pad_lora(rwkv_a2[0], 2 * rwkv_a1.shape[3]).astype(BF16),
                      rwkv_g1[0].astype(BF16), rwkv_g2[0].astype(BF16), hd, hdt, n_ctx_tiles)
    lw0, lw1, kd0, kd1, b0, b1, v, kk, r, g = feats
    yf, yb = _scan((lw0, lw1, kd0, kd1, b0, b1, v, kk, r), n_ctx_tiles)

    vec_ro = _rows8([rwkv_ln_g[0], rwkv_ln_b[0], rwkv_r_k[0].reshape(-1), norm_g[1, 1]], d)
    rwt1, rb1 = router_params(1)
    x3, h2, pos, gd = _readout(yf, yb, r, kd0, kd1, v, g, x2, mod_lat[1], vec_ro,
                               rwkv_w_o[0].astype(BF16), hd, hdt, rwt1, rb1, n_ctx_tiles, moe_tile_lat, seq)
    ft = _moe(h2, pos, gd, *expert_params(1), moe_tile_lat)
    out = _final(x3, ft, mod_lat[1], _rows8([final_g], d), seq)
    return out.reshape(nb, seq, d)
```

```python
import functools

import numpy as np
import jax
import jax.numpy as jnp
from jax import lax
from jax.experimental import pallas as pl
from jax.experimental.pallas import tpu as pltpu

F32 = jnp.float32
BF16 = jnp.bfloat16
NN = (((1,), (0,)), ((), ()))
NT = (((1,), (1,)), ((), ()))
TN = (((0,), (0,)), ((), ()))

N_MOD = 6
NORM_EPS = 1e-6
GRID_W = 64
POOL_WINDOWS = (2, 4, 8, 16)
RWKV_HEAD = 64
GN_EPS = 64e-5
TOP_K = 4
SWIGLU_ALPHA = 1.702
SWIGLU_LIMIT = 7.0

TOK_TILE = 256
MOE_TILE = 1792
MOE_GROUP = 4
MOE_ROWS = 256
MOE_CHUNK_BLOCKS = 4
CHUNK = 64
GROUP_HEADS = 2
GROUP = GROUP_HEADS * RWKV_HEAD
HEAD_PAD = 128
VMEM_LIMIT_BYTES = 56 * 1024 * 1024


def _dotx(a, b, dims=NN):
    return lax.dot_general(a, b, dims, precision=lax.Precision.HIGHEST, preferred_element_type=F32)


def _dotb(a, b, dims=NN):
    return lax.dot_general(a.astype(BF16), b.astype(BF16), dims, preferred_element_type=F32)


def _params(*sem):
    return pltpu.CompilerParams(dimension_semantics=sem, vmem_limit_bytes=VMEM_LIMIT_BYTES)


def _padded_tokens(ntok, tile):
    return -(-ntok // tile) * tile


def _rms(x):
    return x * lax.rsqrt(jnp.mean(x * x, axis=-1, keepdims=True) + NORM_EPS)


def _sigmoid(x):
    return 1.0 / (1.0 + jnp.exp(-x))


def _ada_kernel(c_ref, w_ref, b_ref, o_ref):
    c = c_ref[...]
    o_ref[0] = _dotx(c * _sigmoid(c), w_ref[0]) + b_ref[0]


def _ada(cond8, ada_w, ada_b):
    depth, d, nd = ada_w.shape
    return pl.pallas_call(
        _ada_kernel,
        grid=(depth, nd // d),
        in_specs=[pl.BlockSpec((8, d), lambda i, n: (0, 0)),
                  pl.BlockSpec((1, d, d), lambda i, n: (i, 0, n)),
                  pl.BlockSpec((1, 1, d), lambda i, n: (i, 0, n))],
        out_specs=pl.BlockSpec((1, 8, d), lambda i, n: (i, 0, n)),
        out_shape=jax.ShapeDtypeStruct((depth, 8, nd), F32),
        compiler_params=_params("parallel", "parallel"),
        name="ada",
    )(cond8, ada_w, ada_b.reshape(depth, 1, nd))


def _route_padding(h2_ref, pos_ref, gd_ref):
    h2_ref[...] = jnp.zeros_like(h2_ref)
    pos_ref[...] = jnp.full(pos_ref.shape, -1, jnp.int32)
    gd_ref[...] = jnp.zeros_like(gd_ref)


def _route(x_new, mod, g2, rwt_ref, rb_ref, u_ref, off_ref, h2_ref, pos_ref, gd_ref, reset):
    h2 = _rms(x_new) * g2 * (1.0 + mod[4:5]) + mod[3:4]
    h2_ref[...] = h2.astype(BF16)
    logits = _dotx(rwt_ref[...], h2, NT) + rb_ref[...]
    n_e = logits.shape[0]
    eio = lax.broadcasted_iota(jnp.int32, logits.shape, 0).astype(F32)
    live = logits
    sels, vals = [], []
    for _ in range(TOP_K):
        m = jnp.max(live, axis=0, keepdims=True)
        idx = jnp.min(jnp.where(live == m, eio, float(n_e)), axis=0, keepdims=True)
        sel = eio == idx
        sels.append(sel)
        vals.append(m)
        live = jnp.where(sel, -jnp.inf, live)
    exps = [jnp.exp(v - vals[0]) for v in vals]
    inv = 1.0 / (exps[0] + exps[1] + exps[2] + exps[3])
    gd = jnp.zeros_like(logits)
    maskf = jnp.zeros_like(logits)
    for k in range(TOP_K):
        gd = jnp.where(sels[k], exps[k] * inv, gd)
        maskf = jnp.where(sels[k], 1.0, maskf)

    @pl.when(reset)
    def _():
        off_ref[...] = jnp.zeros_like(off_ref)

    incl = _dotb(maskf, u_ref[...])
    off = off_ref[:, 0:1]
    pos_ref[0] = jnp.where(maskf > 0.5, off + incl - 1.0, -1.0).astype(jnp.int32)
    gd_ref[0] = gd
    off_ref[...] = off_ref[...] + jnp.sum(maskf, axis=1, keepdims=True)


def _route_outputs(n_pad, d, n_e, moe_tile):
    t = TOK_TILE
    tpm = moe_tile // t
    specs = [pl.BlockSpec((t, d), lambda i: (i, 0)),
             pl.BlockSpec((1, n_e, t), lambda i: (i // tpm, 0, i % tpm)),
             pl.BlockSpec((1, n_e, t), lambda i: (i // tpm, 0, i % tpm))]
    shapes = [jax.ShapeDtypeStruct((n_pad, d), BF16),
              jax.ShapeDtypeStruct((n_pad // moe_tile, n_e, moe_tile), jnp.int32),
              jax.ShapeDtypeStruct((n_pad // moe_tile, n_e, moe_tile), F32)]
    return specs, shapes


def _route_consts(tile):
    s = np.arange(tile)
    return jnp.asarray((s[:, None] <= s[None, :]).astype(np.float32), BF16)


def _pool_kernel(x_ref, mod_ref, vec_ref, pm_ref, pw_ref, rwt_ref, rb_ref, u_ref,
                 x1_ref, h2_ref, pos_ref, gd_ref, off_ref, *, tiles_per_moe, n_real_tiles):
    i = pl.program_id(0)

    @pl.when(i >= n_real_tiles)
    def _():
        _route_padding(h2_ref, pos_ref, gd_ref)

    @pl.when(i < n_real_tiles)
    def _():
        x = x_ref[...]
        mod = mod_ref[0]
        vec = vec_ref[...]
        h = _rms(x) * vec[0:1] * (1.0 + mod[1:2]) + mod[0:1]
        gw = pw_ref.shape[1]
        ys = []
        for g in range(len(POOL_WINDOWS)):
            d = _dot3(pm_ref[g], h[:, g * gw:(g + 1) * gw])
            ys.append(_dotb(d, pw_ref[g]))
        y = jnp.concatenate(ys, axis=1) * vec[2:3]
        x1 = x + mod[2:3] * y
        x1_ref[...] = x1
        _route(x1, mod, vec[1:2], rwt_ref, rb_ref, u_ref, off_ref, h2_ref, pos_ref, gd_ref,
               i % tiles_per_moe == 0)


def _pool_matrices(tile, row_len):
    p = np.arange(tile)
    pp = p % row_len
    out = []
    for win in POOL_WINDOWS:
        lo = np.clip(pp - win // 2, 0, row_len - 1)
        hi = np.clip(pp + win // 2 - 1, 0, row_len - 1)
        cnt = (hi - lo + 1).astype(np.float64)
        same = (p[:, None] // row_len) == (p[None, :] // row_len)
        inwin = same & (pp[None, :] >= lo[:, None]) & (pp[None, :] <= hi[:, None])
        out.append(inwin / cnt[:, None] - np.eye(tile))
    return jnp.asarray(np.stack(out).astype(np.float32))


def _pool_layer(x2d, mod, vec, pool_w, rwt, rb, row_len, moe_tile, moe_span, tokens_per_batch):
    ntok, d = x2d.shape
    t = TOK_TILE
    n_e = rwt.shape[0]
    tiles_per_batch = tokens_per_batch // t
    n_real = ntok // t
    n_pad = _padded_tokens(ntok, moe_span)
    route_specs, route_shapes = _route_outputs(n_pad, d, n_e, moe_tile)
    pm = _pool_matrices(t, row_len)
    kern = functools.partial(_pool_kernel, tiles_per_moe=moe_tile // t, n_real_tiles=n_real)
    const2 = lambda i: (0, 0)
    const3 = lambda i: (0, 0, 0)
    real = lambda i: jnp.minimum(i, n_real - 1)
    return pl.pallas_call(
        kern,
        grid=(n_pad // t,),
        in_specs=[pl.BlockSpec((t, d), lambda i: (real(i), 0)),
                  pl.BlockSpec((1, 8, d), lambda i: (real(i) // tiles_per_batch, 0, 0)),
                  pl.BlockSpec((8, d), const2),
                  pl.BlockSpec(pm.shape, const3),
                  pl.BlockSpec(pool_w.shape, const3),
                  pl.BlockSpec(rwt.shape, const2),
                  pl.BlockSpec(rb.shape, const2),
                  pl.BlockSpec((t, t), const2)],
        out_specs=[pl.BlockSpec((t, d), lambda i: (real(i), 0))] + route_specs,
        out_shape=[jax.ShapeDtypeStruct((ntok, d), F32)] + route_shapes,
        scratch_shapes=[pltpu.VMEM((n_e, 128), F32)],
        compiler_params=_params("arbitrary"),
        name="pool",
    )(x2d, mod, vec, pm, pool_w, rwt, rb, _route_consts(t))


def _one_hot_rows(prow, r0, n_rows):
    rows = lax.broadcasted_iota(jnp.int32, (n_rows, prow.shape[1]), 0) + r0
    return jnp.where(prow == rows, 1.0, 0.0).astype(BF16)


def _expert_kernel(grp_ref, sub_ref, e_ref, r0_ref,
                   pos_ref, gd_ref, x_ref, wgu_ref, bgu_ref, wdt_ref, o_ref):
    b = pl.program_id(0)
    f = wdt_ref.shape[2]
    n_rows = o_ref.shape[1]
    ts = pos_ref.shape[2]
    e = e_ref[b]
    sub = sub_ref[b]
    oh = _one_hot_rows(pos_ref[sub, pl.ds(e, 1), :], r0_ref[b], n_rows)
    grow = gd_ref[sub, pl.ds(e, 1), :]
    g_hi = grow.astype(BF16)
    g_mid = (grow - g_hi.astype(F32)).astype(BF16)
    g_lo = (grow - g_hi.astype(F32) - g_mid.astype(F32)).astype(BF16)
    g3 = jnp.concatenate([g_hi, g_mid, g_lo, jnp.zeros((5, ts), BF16)], axis=0)
    x = x_ref[pl.ds(pl.multiple_of(sub * ts, ts), ts), :]
    xg = jnp.dot(oh, x, preferred_element_type=F32).astype(BF16)
    gu = jnp.dot(xg, wgu_ref[0], preferred_element_type=F32) + bgu_ref[0]
    gate = jnp.minimum(gu[:, :f], SWIGLU_LIMIT)
    up = jnp.clip(gu[:, f:], -SWIGLU_LIMIT, SWIGLU_LIMIT)
    act = (up + 1.0) * (gate * _sigmoid(SWIGLU_ALPHA * gate))
    yt = lax.dot_general(wdt_ref[0], act.astype(BF16), NT, preferred_element_type=F32)
    g8 = lax.dot_general(g3, oh, NT, preferred_element_type=F32)
    o_ref[...] = (yt * (g8[0:1] + g8[1:2] + g8[2:3])).astype(BF16)


def _combine_kernel(tile_ref, first_ref, blk_ref, e_ref, r0_ref, pos_ref, gd_ref, *rest):
    yt_refs, bdt_ref, o_ref = rest[:MOE_CHUNK_BLOCKS], rest[-2], rest[-1]
    c = pl.program_id(0)
    n_rows = yt_refs[0].shape[1]

    @pl.when(first_ref[c] == 1)
    def _():
        o_ref[...] = _dotx(bdt_ref[...], gd_ref[0])

    slot = lambda q: c * MOE_CHUNK_BLOCKS + q
    oh = jnp.concatenate([_one_hot_rows(pos_ref[0, pl.ds(e_ref[slot(q)], 1), :], r0_ref[slot(q)], n_rows)
                          for q in range(MOE_CHUNK_BLOCKS)], axis=0)
    yt = jnp.concatenate([ref[...] for ref in yt_refs], axis=1)
    o_ref[...] += jnp.dot(yt, oh, preferred_element_type=F32)


def _moe_tables(pos, group):
    nt, n_e, ts = pos.shape
    ng = nt // group
    rb, cb = MOE_ROWS, MOE_CHUNK_BLOCKS
    i32 = lambda a: a.astype(jnp.int32)
    nb = (jnp.sum(i32(pos >= 0), axis=-1) + rb - 1) // rb
    max_tile_blocks = ts * TOP_K // rb + n_e
    flat = nb.reshape(ng, group, n_e).transpose(0, 2, 1).reshape(-1)
    cum = jnp.cumsum(flat)
    start = cum - flat
    n_blocks = cum[-1]
    b = jnp.minimum(jnp.arange(nt * max_tile_blocks, dtype=jnp.int32), n_blocks - 1)
    idx = jnp.minimum(jnp.searchsorted(cum, b, side='right'), flat.shape[0] - 1)
    expert_tables = (i32(idx // (n_e * group)), i32(idx % group), i32((idx // group) % n_e),
                     i32((b - start[idx]) * rb))
    ce = jnp.cumsum(nb, axis=1)
    nch = (ce[:, -1] + cb - 1) // cb
    cch = jnp.cumsum(nch)
    n_chunks = cch[-1]
    max_chunks = nt * (-(-max_tile_blocks // cb))
    c = jnp.minimum(jnp.arange(max_chunks, dtype=jnp.int32), n_chunks - 1)
    tile = jnp.minimum(jnp.searchsorted(cch, c, side='right'), nt - 1)
    lc = c - (cch - nch)[tile]
    slot = (lc[:, None] * cb + jnp.arange(cb, dtype=jnp.int32)[None, :]).reshape(-1)
    tile_s = jnp.repeat(tile, cb)
    used = slot < ce[tile_s, -1]
    slot = jnp.where(used, slot, 0)
    ce_s = ce[tile_s]
    e_s = jnp.minimum(jnp.sum(i32(slot[:, None] >= ce_s), axis=1), n_e - 1)
    local = slot - jnp.take_along_axis(ce_s - nb[tile_s], e_s[:, None], axis=1)[:, 0]
    blk = start.reshape(ng, n_e, group)[tile_s // group, e_s, tile_s % group] + local
    combine_tables = (i32(tile), i32(lc == 0), i32(blk), i32(e_s), i32(jnp.where(used, local * rb, 1 << 20)))
    return expert_tables, i32(n_blocks), combine_tables, i32(n_chunks)


def _moe(h2, pos, gd, wgu, bgu, wdt, bdt, group):
    ntok, d = h2.shape
    nt, n_e, ts = pos.shape
    f2 = wgu.shape[2]
    rb, cb = MOE_ROWS, MOE_CHUNK_BLOCKS
    expert_tables, n_blocks, combine_tables, n_chunks = _moe_tables(pos, group)
    max_blocks = expert_tables[0].shape[0]
    yt = pl.pallas_call(
        _expert_kernel,
        grid_spec=pltpu.PrefetchScalarGridSpec(
            num_scalar_prefetch=4,
            grid=(n_blocks,),
            in_specs=[pl.BlockSpec((group, n_e, ts), lambda b, gp, sb, ee, rr: (gp[b], 0, 0)),
                      pl.BlockSpec((group, n_e, ts), lambda b, gp, sb, ee, rr: (gp[b], 0, 0)),
                      pl.BlockSpec((group * ts, d), lambda b, gp, sb, ee, rr: (gp[b], 0),
                                   pipeline_mode=pl.Buffered(1)),
                      pl.BlockSpec((1, d, f2), lambda b, gp, sb, ee, rr: (ee[b], 0, 0)),
                      pl.BlockSpec((1, 1, f2), lambda b, gp, sb, ee, rr: (ee[b], 0, 0)),
                      pl.BlockSpec((1, d, f2 // 2), lambda b, gp, sb, ee, rr: (ee[b], 0, 0))],
            out_specs=pl.BlockSpec((d, rb), lambda b, gp, sb, ee, rr: (0, b)),
        ),
        out_shape=jax.ShapeDtypeStruct((d, max_blocks * rb), BF16),
        compiler_params=_params("arbitrary"),
        name="expert",
    )(*expert_tables, pos, gd, h2, wgu, bgu, wdt)
    yt_spec = lambda q: pl.BlockSpec((d, rb), lambda c, tl, fs, bk, ee, rr: (0, bk[c * cb + q]))
    return pl.pallas_call(
        _combine_kernel,
        grid_spec=pltpu.PrefetchScalarGridSpec(
            num_scalar_prefetch=5,
            grid=(n_chunks,),
            in_specs=[pl.BlockSpec((1, n_e, ts), lambda c, tl, fs, bk, ee, rr: (tl[c], 0, 0)),
                      pl.BlockSpec((1, n_e, ts), lambda c, tl, fs, bk, ee, rr: (tl[c], 0, 0))]
                     + [yt_spec(q) for q in range(cb)]
                     + [pl.BlockSpec((d, n_e), lambda c, tl, fs, bk, ee, rr: (0, 0))],
            out_specs=pl.BlockSpec((d, ts), lambda c, tl, fs, bk, ee, rr: (0, tl[c]),
                                   pipeline_mode=pl.Buffered(1)),
        ),
        out_shape=jax.ShapeDtypeStruct((d, ntok), F32),
        compiler_params=_params("arbitrary"),
        name="combine",
    )(*combine_tables, pos, gd, *([yt] * cb), bdt)


def _respre_kernel(x_ref, ft_ref, modp_ref, modc_ref, vec_ref, *rest):
    x2_ref, h_ref = rest[-2], rest[-1]
    x2 = x_ref[...] + modp_ref[0][5:6] * jnp.transpose(ft_ref[...])
    x2_ref[...] = x2
    modc = modc_ref[0]
    h_ref[0] = _rms(x2) * vec_ref[0:1] * (1.0 + modc[1:2]) + modc[0:1]


def _respre(x2d, ft, modp, modc, vec, h_all, tokens_per_batch, tile_offset, seq_total):
    ntok, d = x2d.shape
    t = TOK_TILE
    nb = modp.shape[0]
    tpb = tokens_per_batch // t
    in_specs = [pl.BlockSpec((t, d), lambda i: (i, 0)),
                pl.BlockSpec((d, t), lambda i: (0, i)),
                pl.BlockSpec((1, 8, d), lambda i: (i // tpb, 0, 0)),
                pl.BlockSpec((1, 8, d), lambda i: (i // tpb, 0, 0)),
                pl.BlockSpec((8, d), lambda i: (0, 0))]
    args = [x2d, ft, modp, modc, vec]
    aliases = {}
    if h_all is not None:
        in_specs.append(pl.BlockSpec(memory_space=pl.ANY))
        args.append(h_all)
        aliases = {5: 1}
    return pl.pallas_call(
        _respre_kernel,
        grid=(ntok // t,),
        in_specs=in_specs,
        out_specs=[pl.BlockSpec((t, d), lambda i: (i, 0)),
                   pl.BlockSpec((1, t, d), lambda i: (i // tpb, tile_offset + i % tpb, 0))],
        out_shape=[jax.ShapeDtypeStruct((ntok, d), F32),
                   jax.ShapeDtypeStruct((nb, seq_total, d), F32)],
        input_output_aliases=aliases,
        compiler_params=_params("parallel"),
        name="respre",
    )(*args)


def _feat_kernel(h_ref, hp_ref, hn_ref, fv_ref, wrkv_ref, w1_ref, w2_ref, a1_ref, a2_ref, g1_ref, g2_ref,
                 hd_ref, hdt_ref,
                 lw0_ref, lw1_ref, kd0_ref, kd1_ref, b0_ref, b1_ref, v_ref, kk_ref, r_ref, g_ref,
                 *, n_ctx_tiles, n_tiles):
    i = pl.program_id(1)
    h = h_ref[0]
    t = h.shape[0]
    fv = fv_ref[...]
    first = jnp.logical_or(i == 0, i == n_ctx_tiles)
    last = jnp.logical_or(i == n_ctx_tiles - 1, i == n_tiles - 1)
    prow = jnp.where(first, 0.0, hp_ref[0][7:8])
    nrow = jnp.where(last, 0.0, hn_ref[0][0:1])
    rio = lax.broadcasted_iota(jnp.int32, h.shape, 0)
    hdn = jnp.where(rio == 0, prow, pltpu.roll(h, 1, 0))
    hup = jnp.where(rio == t - 1, nrow, pltpu.roll(h, t - 1, 0))
    xx = 0.5 * (hdn + hup) - h

    mix = lambda m: h + xx * fv[m:m + 1]
    r = _dotb(mix(0), wrkv_ref[0])
    k = _dotb(mix(2), wrkv_ref[1])
    v = _dotb(mix(3), wrkv_ref[2])
    tw = jnp.tanh(_dotb(mix(1), w1_ref[...]))
    la = _dotb(mix(4), a1_ref[...])
    g = _dotb(_sigmoid(_dotb(mix(5), g1_ref[...])), g2_ref[...])

    kkraw = k * fv[10:11]
    ss = _headsum(kkraw * kkraw, hd_ref[...], hdt_ref[...])
    kk = kkraw / jnp.maximum(jnp.sqrt(ss), 1e-12)
    k_a = fv[11:12]
    decay_scale = float(np.exp(-0.5))
    for d, (lw_ref, kd_ref, b_ref) in enumerate(((lw0_ref, kd0_ref, b0_ref), (lw1_ref, kd1_ref, b1_ref))):
        zw = fv[6 + d:7 + d] + _dotb(tw, w2_ref[d])
        lw_ref[0] = -decay_scale * _sigmoid(zw)
        icl = _sigmoid(fv[8 + d:9 + d] + _dotb(la, a2_ref[d]))
        kd_ref[0] = k * (1.0 + (icl - 1.0) * k_a)
        b_ref[0] = kk * icl
    v_ref[0] = v
    kk_ref[0] = kk
    r_ref[0] = r
    g_ref[0] = g


def _head_indicators(d):
    hd = np.zeros((d, HEAD_PAD), np.float32)
    hd[np.arange(d), np.arange(d) // RWKV_HEAD] = 1.0
    return jnp.asarray(hd), jnp.asarray(hd.T.copy())


def _features(h_all, fv, wrkv, w1c, w2p, a1c, a2p, g1, g2, hd, hdt, n_ctx_tiles):
    nb, s, d = h_all.shape
    t = TOK_TILE
    nt = s // t
    r8 = t // 8
    kern = functools.partial(_feat_kernel, n_ctx_tiles=n_ctx_tiles, n_tiles=nt)
    full = lambda a: pl.BlockSpec(a.shape, lambda b, i: (0,) * a.ndim)
    tok = pl.BlockSpec((1, t, d), lambda b, i: (b, i, 0))
    return pl.pallas_call(
        kern,
        grid=(nb, nt),
        in_specs=[tok,
                  pl.BlockSpec((1, 8, d), lambda b, i: (b, jnp.maximum(i * r8 - 1, 0), 0)),
                  pl.BlockSpec((1, 8, d), lambda b, i: (b, jnp.minimum((i + 1) * r8, s // 8 - 1), 0)),
                  full(fv), full(wrkv), full(w1c), full(w2p), full(a1c), full(a2p), full(g1), full(g2),
                  full(hd), full(hdt)],
        out_specs=[tok] * 10,
        out_shape=[jax.ShapeDtypeStruct((nb, s, d), F32)] * 10,
        compiler_params=_params("parallel", "parallel"),
        name="feat",
    )(h_all, h_all, h_all, fv, wrkv, w1c, w2p, a1c, a2p, g1, g2, hd, hdt)


def _scan_masks():
    n = GROUP_HEADS * CHUNK
    rr = np.arange(n)[:, None]
    cc = np.arange(n)[None, :]
    bd = (rr // CHUNK) == (cc // CHUNK)
    r_, c_ = rr % CHUNK, cc % CHUNK
    out = []
    for rev in (False, True):
        strict = bd & ((c_ > r_) if rev else (c_ < r_))
        incl = bd & ((c_ >= r_) if rev else (c_ <= r_))
        d16 = strict & ((r_ // 16) == (c_ // 16))
        l1 = strict & ((r_ // 32) == (c_ // 32)) & ((r_ // 16) != (c_ // 16))
        l2 = strict & ((r_ // 32) != (c_ // 32))
        out.append([strict, incl, d16, l1, l2])
    masks = [out[0][m] for m in range(5)] + [out[1][m] for m in range(5)] + [bd, rr == cc]
    hm = np.zeros((8, GROUP), np.float32)
    for j in range(GROUP_HEADS):
        hm[j, j * RWKV_HEAD:(j + 1) * RWKV_HEAD] = 1.0
    tri3 = np.stack([np.tile(out[rev][1][0:CHUNK, 0:CHUNK], (1, 3)) for rev in (0, 1)]).astype(np.float32)
    return jnp.asarray(np.stack(masks).astype(np.float32)), jnp.asarray(tri3, BF16), jnp.asarray(hm)


def _split3(a):
    a1 = a.astype(BF16)
    r1 = a - a1.astype(F32)
    a2 = r1.astype(BF16)
    return a1, a2, (r1 - a2.astype(F32)).astype(BF16)


def _dot3(a, b, dims=NN):
    a1, a2, _ = _split3(a)
    b1, b2, _ = _split3(b)
    mm = lambda p, q: lax.dot_general(p, q, dims, preferred_element_type=F32)
    return mm(a1, b1) + (mm(a1, b2) + mm(a2, b1))


def _dot01(a, ind):
    ib = ind.astype(BF16)
    a1, a2, _ = _split3(a)
    mm = lambda p: jnp.dot(p, ib, preferred_element_type=F32)
    return mm(a1) + mm(a2)


def _headsum(z, hd, hdt):
    return _dot01(_dot01(z, hd), hdt)


def _chunk_terms(blocks, msk_ref, tri_ref, hm):
    c_len = blocks[0][0].shape[0]
    n4 = GROUP_HEADS * c_len
    revs = [blk[6] for blk in blocks]
    mask = lambda k: [msk_ref[(5 if rev else 0) + k] for rev in revs]
    bd, eye = msk_ref[10], msk_ref[11]
    each = lambda f, *cols: [f(*args) for args in zip(*cols)]
    stack4 = lambda z: jnp.concatenate([z * hm[j:j + 1] for j in range(GROUP_HEADS)], axis=0)
    tile4 = lambda z: jnp.concatenate([z] * GROUP_HEADS, axis=0)
    fold4 = lambda z: sum(z[j * c_len:(j + 1) * c_len] for j in range(GROUP_HEADS))
    lw, kd, b, v, kk, r = [[blk[k] for blk in blocks] for k in range(6)]

    tri3 = [tri_ref[1 if rev else 0] for rev in revs]
    parts = each(lambda z: jnp.concatenate(_split3(z), axis=0), lw)
    c = each(lambda t, p: jnp.dot(t, p, preferred_element_type=F32), tri3, parts)
    ctot = each(lambda c_, rev: c_[0:1] if rev else c_[c_len - 1:c_len], c, revs)
    en = each(lambda c_: jnp.exp(-c_), c)
    khat = each(lambda kk_, c_, lw_: kk_ * jnp.exp(c_ - lw_), kk, c, lw)
    rhat = each(lambda r_, c_: r_ * jnp.exp(c_), r, c)
    bch = each(jnp.multiply, b, en)
    kch = each(jnp.multiply, kd, en)
    et = each(jnp.exp, ctot)
    btil = each(jnp.multiply, bch, et)
    ktil = each(jnp.multiply, kch, et)
    kh4, rh4, v4 = each(stack4, khat), each(stack4, rhat), each(stack4, v)
    sc = each(lambda k_, r_, b_, c_: _dotb(jnp.concatenate([k_, r_], axis=0),
                                           jnp.concatenate([tile4(b_), tile4(c_)], axis=0), NT),
              kh4, rh4, bch, kch)
    a_bk = each(lambda s, m_: s[0:n4, 0:n4] * m_, sc, mask(0))
    a_kk = each(lambda s, m_: s[0:n4, n4:2 * n4] * m_, sc, mask(0))
    b_br = each(lambda s, m_: s[n4:2 * n4, 0:n4] * m_, sc, mask(1))
    b_kr = each(lambda s, m_: s[n4:2 * n4, n4:2 * n4] * m_, sc, mask(1))
    avp = each(lambda a_, b_, v_: _dotb(jnp.concatenate([a_, b_], axis=0), v_), a_kk, b_kr, v4)
    av4 = [z[0:n4] for z in avp]
    p2 = [z[n4:2 * n4] for z in avp]
    ad = each(jnp.multiply, a_bk, mask(2))
    a2 = each(_dotb, ad, ad)
    times_and_square = lambda x_, p_: _dotb(jnp.concatenate([x_, p_], axis=0), p_)
    x = [eye - ad_ for ad_ in ad]
    both = each(times_and_square, x, a2)
    x = each(lambda x_, z: x_ + z[0:n4], x, both)
    a4 = [z[n4:2 * n4] for z in both]
    both = each(times_and_square, x, a4)
    x = each(lambda x_, z: x_ + z[0:n4], x, both)
    a8 = [z[n4:2 * n4] for z in both]
    dinv = each(lambda x_, a8_: x_ + _dotb(x_, a8_), x, a8)
    t1 = each(lambda d_, a_, m_: _dotb(d_, a_ * m_), dinv, a_bk, mask(3))
    b1 = each(lambda d_, t_: d_ - _dotb(t_, d_), dinv, t1)
    t2 = each(lambda b_, a_, m_: _dotb(b_, a_ * m_), b1, a_bk, mask(4))
    tinv = each(lambda b_, t_: b_ - _dotb(t_, b_), b1, t2)
    wu = each(lambda t_, k_, a_: _dotb(t_, jnp.concatenate([k_, a_], axis=1)), tinv, kh4, av4)
    p1 = each(_dotb, b_br, wu)
    q = GROUP
    wy = each(lambda r_, p_: fold4(r_ - p_[:, 0:q]), rh4, p1)
    uy = each(lambda p2_, p_: fold4(p2_ - p_[:, q:2 * q]), p2, p1)
    w1 = each(lambda w_: fold4(w_[:, 0:q]), wu)
    u = each(lambda w_: fold4(w_[:, q:2 * q]), wu)
    om = each(lambda k_, b_, v_, u_, w_: _dotb(
        jnp.concatenate([k_, b_], axis=0),
        jnp.concatenate([jnp.concatenate([v_, jnp.zeros_like(v_)], axis=1),
                         jnp.concatenate([-u_, -w_], axis=1)], axis=0), TN), ktil, btil, v, u, w1)
    o = [bd * z[:, 0:q] for z in om]
    m = each(lambda e_, z: eye * e_ + bd * z[:, q:2 * q], et, om)
    return list(zip(wy, uy, m, o))


def _scan_kernel(lwf, kdf, bf, vf, kkf, rf, lwb, kdb, bb, vb, kkb, rb, msk_ref, tri_ref, hm_ref,
                 yf_ref, yb_ref, st_ref):
    @pl.when(pl.program_id(1) == 0)
    def _():
        st_ref[...] = jnp.zeros_like(st_ref)

    hm = hm_ref[...]
    n_batch = lwf.shape[0]
    n_chunks = lwf.shape[1] // CHUNK
    dirs = ((False, (lwf, kdf, bf, vf, kkf, rf), yf_ref), (True, (lwb, kdb, bb, vb, kkb, rb), yb_ref))
    seqs = [(rev, refs, y_ref, bi) for rev, refs, y_ref in dirs for bi in range(n_batch)]
    order = lambda rev: list(reversed(range(n_chunks))) if rev else list(range(n_chunks))
    sl = lambda rev, k: slice(order(rev)[k] * CHUNK, (order(rev)[k] + 1) * CHUNK)
    blocks = [tuple(ref[bi, sl(rev, k), :] for ref in refs) + (rev,)
              for k in range(n_chunks) for rev, refs, _, bi in seqs]
    terms = _chunk_terms(blocks, msk_ref, tri_ref, hm)
    st = [st_ref[si] for si in range(len(seqs))]
    c_len = CHUNK
    for k in range(n_chunks):
        lhs = [jnp.concatenate(terms[k * len(seqs) + si][0::2], axis=0) for si in range(len(seqs))]
        lp = [_split3(z) for z in lhs]
        sp = [_split3(z) for z in st]
        res = [jnp.dot(jnp.concatenate([l[0], l[1], l[0]], axis=1), jnp.concatenate([s[0], s[0], s[1]], axis=0),
                       preferred_element_type=F32) for l, s in zip(lp, sp)]
        n_l = c_len + GROUP
        for si, (rev, _, y_ref, bi) in enumerate(seqs):
            _, uy, _, o = terms[k * len(seqs) + si]
            y_ref[bi, sl(rev, k), :] = res[si][0:c_len] + uy
            st[si] = res[si][c_len:n_l] + o
    for si in range(len(seqs)):
        st_ref[si] = st[si]


def _scan(feats, n_ctx_tiles):
    lw0, lw1, kd0, kd1, b0, b1, v, kk, r = feats
    nb, s, d = v.shape
    t = TOK_TILE
    nt = s // t
    msk, tri3, hm = _scan_masks()
    fwd = pl.BlockSpec((nb, t, GROUP), lambda q, i: (0, i, q))
    bwd_idx = lambda i: jnp.where(i < n_ctx_tiles, n_ctx_tiles - 1 - i, nt - 1 - (i - n_ctx_tiles))
    bwd = pl.BlockSpec((nb, t, GROUP), lambda q, i: (0, bwd_idx(i), q))
    return pl.pallas_call(
        _scan_kernel,
        grid=(d // GROUP, nt),
        in_specs=[fwd] * 6 + [bwd] * 6 + [pl.BlockSpec(msk.shape, lambda q, i: (0, 0, 0)),
                                          pl.BlockSpec(tri3.shape, lambda q, i: (0, 0, 0)),
                                          pl.BlockSpec(hm.shape, lambda q, i: (0, 0))],
        out_specs=[fwd, bwd],
        out_shape=[jax.ShapeDtypeStruct((nb, s, d), F32)] * 2,
        scratch_shapes=[pltpu.VMEM((2 * nb, GROUP, GROUP), F32)],
        compiler_params=_params("parallel", "arbitrary"),
        name="scan",
    )(lw0, kd0, b0, v, kk, r, lw1, kd1, b1, v, kk, r, msk, tri3, hm)


def _readout_kernel(yf_ref, yb_ref, r_ref, kd0_ref, kd1_ref, v_ref, g_ref, x_ref, mod_ref, vec_ref,
                    wo_ref, hd_ref, hdt_ref, rwt_ref, rb_ref, u_ref,
                    x3_ref, h2_ref, pos_ref, gd_ref, off_ref, *, tiles_per_moe, n_real_tiles):
    i = pl.program_id(0)

    @pl.when(i >= n_real_tiles)
    def _():
        _route_padding(h2_ref, pos_ref, gd_ref)

    @pl.when(i < n_real_tiles)
    def _():
        vec = vec_ref[...]
        mod = mod_ref[0]
        headsum = lambda z: _headsum(z, hd_ref[...], hdt_ref[...])
        inv_k = 1.0 / RWKV_HEAD
        y = yf_ref[0] + yb_ref[0]
        yc = y - headsum(y) * inv_k
        var = headsum(yc * yc) * inv_k
        yn = yc * lax.rsqrt(var + GN_EPS) * vec[0:1] + vec[1:2]
        bonus = headsum(r_ref[0] * (kd0_ref[0] + kd1_ref[0]) * vec[2:3]) * v_ref[0]
        out = (yn + bonus) * g_ref[0]
        x3 = x_ref[...] + mod[2:3] * _dotb(out, wo_ref[...])
        x3_ref[...] = x3
        _route(x3, mod, vec[3:4], rwt_ref, rb_ref, u_ref, off_ref, h2_ref, pos_ref, gd_ref,
               i % tiles_per_moe == 0)


def _readout(yf, yb, r, kd0, kd1, v, g, x2d, mod, vec, wo, hd, hdt, rwt, rb, n_ctx_tiles, moe_tile,
             moe_span, tokens_per_batch):
    ntok, d = x2d.shape
    t = TOK_TILE
    n_e = rwt.shape[0]
    tpb = tokens_per_batch // t
    n_real = ntok // t
    n_pad = _padded_tokens(ntok, moe_span)
    route_specs, route_shapes = _route_outputs(n_pad, d, n_e, moe_tile)
    real = lambda i: jnp.minimum(i, n_real - 1)
    seq = pl.BlockSpec((1, t, d), lambda i: (real(i) // tpb, n_ctx_tiles + real(i) % tpb, 0))
    tokspec = pl.BlockSpec((t, d), lambda i: (real(i), 0))
    full = lambda a: pl.BlockSpec(a.shape, lambda i: (0,) * a.ndim)
    u = _route_consts(t)
    kern = functools.partial(_readout_kernel, tiles_per_moe=moe_tile // t, n_real_tiles=n_real)
    return pl.pallas_call(
        kern,
        grid=(n_pad // t,),
        in_specs=[seq] * 7 + [tokspec, pl.BlockSpec((1, 8, d), lambda i: (real(i) // tpb, 0, 0)),
                              full(vec), full(wo), full(hd), full(hdt), full(rwt), full(rb), full(u)],
        out_specs=[tokspec] + route_specs,
        out_shape=[jax.ShapeDtypeStruct((ntok, d), F32)] + route_shapes,
        scratch_shapes=[pltpu.VMEM((n_e, 128), F32)],
        compiler_params=_params("arbitrary"),
        name="readout",
    )(yf, yb, r, kd0, kd1, v, g, x2d, mod, vec, wo, hd, hdt, rwt, rb, u)


def _final_kernel(x_ref, ft_ref, mod_ref, g_ref, o_ref):
    x = x_ref[...] + mod_ref[0][5:6] * jnp.transpose(ft_ref[...])
    o_ref[...] = _rms(x) * g_ref[0:1]


def _final(x2d, ft, mod, gvec, tokens_per_batch):
    ntok, d = x2d.shape
    t = TOK_TILE
    tpb = tokens_per_batch // t
    return pl.pallas_call(
        _final_kernel,
        grid=(ntok // t,),
        in_specs=[pl.BlockSpec((t, d), lambda i: (i, 0)),
                  pl.BlockSpec((d, t), lambda i: (0, i)),
                  pl.BlockSpec((1, 8, d), lambda i: (i // tpb, 0, 0)),
                  pl.BlockSpec((8, d), lambda i: (0, 0))],
        out_specs=pl.BlockSpec((t, d), lambda i: (i, 0)),
        out_shape=jax.ShapeDtypeStruct((ntok, d), F32),
        compiler_params=_params("parallel"),
        name="final",
    )(x2d, ft, mod, gvec)


def _rows8(rows, d):
    n = -(-len(rows) // 8) * 8
    out = jnp.zeros((n, d), F32)
    return out.at[:len(rows)].set(jnp.stack([jnp.asarray(r, F32) for r in rows]))


def _pad_lora(w, total):
    two, r, d = w.shape
    out = jnp.zeros((two, total, d), w.dtype)
    for dd in range(two):
        out = out.at[dd, dd * r:(dd + 1) * r].set(w[dd])
    return out


def kernel(x, c, ctx, c_ctx, norm_g, ada_w, ada_b, pool_w, pool_ls, rwkv_mu, rwkv_w_rkv, rwkv_w0, rwkv_w1, rwkv_w2, rwkv_a0, rwkv_a1, rwkv_a2, rwkv_g1, rwkv_g2, rwkv_k_k, rwkv_k_a, rwkv_r_k, rwkv_ln_g, rwkv_ln_b, rwkv_w_o, moe_router_w, moe_router_b, moe_w_gu, moe_b_gu, moe_w_down, moe_b_down, final_g):
    nb, seq, d = x.shape
    n_ctx = ctx.shape[1]
    depth = norm_g.shape[0]
    n_e = moe_router_w.shape[2]
    t = TOK_TILE
    assert depth == 2 and nb <= 7 and seq % t == 0 and n_ctx % t == 0 and t % GRID_W == 0 and d % GROUP == 0
    n_lat, n_ctxtok = nb * seq, nb * n_ctx
    moe_tile_lat = min(MOE_TILE, n_lat)
    moe_tile_ctx = min(MOE_TILE, n_ctxtok)
    group_lat = min(MOE_GROUP, -(-n_lat // moe_tile_lat))
    group_ctx = min(MOE_GROUP, -(-n_ctxtok // moe_tile_ctx))
    n_ctx_tiles = n_ctx // t
    seq_total = n_ctx + seq

    cond8 = jnp.zeros((8, d), F32).at[:nb].set(c).at[nb].set(c_ctx)
    ada = _ada(cond8, ada_w, ada_b).reshape(depth, 8, N_MOD, d)
    pad8 = lambda m: jnp.pad(m, ((0, 0), (0, 8 - N_MOD), (0, 0)))
    mod_lat = [pad8(ada[i, :nb]) for i in range(depth)]
    mod_ctx = [pad8(jnp.broadcast_to(ada[i, nb][None], (nb, N_MOD, d))) for i in range(depth)]

    def expert_params(i):
        return (moe_w_gu[i].astype(BF16), moe_b_gu[i].reshape(n_e, 1, -1),
                jnp.swapaxes(moe_w_down[i], 1, 2).astype(BF16), moe_b_down[i].T)

    def router_params(i):
        return moe_router_w[i].T, moe_router_b[i].reshape(n_e, 1)

    x2d = x.reshape(n_lat, d)
    ctx2d = ctx.reshape(n_ctxtok, d)

    vec0 = _rows8([norm_g[0, 0], norm_g[0, 1], pool_ls[0]], d)
    pw = pool_w[0].astype(BF16)
    rwt0, rb0 = router_params(0)
    ex0 = expert_params(0)
    x1, h2, pos, gd = _pool_layer(x2d, mod_lat[0], vec0, pw, rwt0, rb0, GRID_W, moe_tile_lat,
                                  group_lat * moe_tile_lat, seq)
    ft_lat = _moe(h2, pos, gd, *ex0, group_lat)
    c1, h2c, posc, gdc = _pool_layer(ctx2d, mod_ctx[0], vec0, pw, rwt0, rb0, n_ctx, moe_tile_ctx,
                                     group_ctx * moe_tile_ctx, n_ctx)
    ft_ctx = _moe(h2c, posc, gdc, *ex0, group_ctx)

    vec1 = _rows8([norm_g[1, 0]], d)
    x2, h_all = _respre(x1, ft_lat, mod_lat[0], mod_lat[1], vec1, None, seq, n_ctx_tiles, seq_total)
    _, h_all = _respre(c1, ft_ctx, mod_ctx[0], mod_ctx[1], vec1, h_all, n_ctx, 0, seq_total)

    fv = _rows8([rwkv_mu[0, m] for m in range(6)]
                + [rwkv_w0[0, 0], rwkv_w0[0, 1], rwkv_a0[0, 0], rwkv_a0[0, 1], rwkv_k_k[0], rwkv_k_a[0]], d)
    lora = rwkv_w1.shape[3]
    cat2 = lambda w: jnp.concatenate([w[0, 0], w[0, 1]], axis=1).astype(BF16)
    hd, hdt = _head_indicators(d)
    feats = _features(h_all, fv, rwkv_w_rkv[0].astype(BF16),
                      cat2(rwkv_w1), _pad_lora(rwkv_w2[0], 2 * lora).astype(BF16),
                      cat2(rwkv_a1), _pad_lora(rwkv_a2[0], 2 * rwkv_a1.shape[3]).astype(BF16),
                      rwkv_g1[0].astype(BF16), rwkv_g2[0].astype(BF16), hd, hdt, n_ctx_tiles)
    lw0, lw1, kd0, kd1, b0, b1, v, kk, r, g = feats
    yf, yb = _scan((lw0, lw1, kd0, kd1, b0, b1, v, kk, r), n_ctx_tiles)

    vec_ro = _rows8([rwkv_ln_g[0], rwkv_ln_b[0], rwkv_r_k[0].reshape(-1), norm_g[1, 1]], d)
    rwt1, rb1 = router_params(1)
    x3, h2, pos, gd = _readout(yf, yb, r, kd0, kd1, v, g, x2, mod_lat[1], vec_ro,
                               rwkv_w_o[0].astype(BF16), hd, hdt, rwt1, rb1, n_ctx_tiles, moe_tile_lat,
                               group_lat * moe_tile_lat, seq)
    ft = _moe(h2, pos, gd, *expert_params(1), group_lat)
    out = _final(x3, ft, mod_lat[1], _rows8([final_g], d), seq)
    return out.reshape(nb, seq, d)
```

```python
import functools

import numpy as np
import jax
import jax.numpy as jnp
from jax import lax
from jax.experimental import pallas as pl
from jax.experimental.pallas import tpu as pltpu

F32 = jnp.float32
BF16 = jnp.bfloat16
NN = (((1,), (0,)), ((), ()))
NT = (((1,), (1,)), ((), ()))
TN = (((0,), (0,)), ((), ()))

N_MOD = 6
NORM_EPS = 1e-6
GRID_W = 64
POOL_WINDOWS = (2, 4, 8, 16)
RWKV_HEAD = 64
GN_EPS = 64e-5
TOP_K = 4
SWIGLU_ALPHA = 1.702
SWIGLU_LIMIT = 7.0

TOK_TILE = 256
MOE_TILE = 1792
MOE_GROUP = 4
MOE_ROWS = 256
MOE_CHUNK_BLOCKS = 4
CHUNK = 64
GROUP_HEADS = 2
GROUP = GROUP_HEADS * RWKV_HEAD
HEAD_PAD = 128
VMEM_LIMIT_BYTES = 56 * 1024 * 1024


def _dotx(a, b, dims=NN):
    return lax.dot_general(a, b, dims, precision=lax.Precision.HIGHEST, preferred_element_type=F32)


def _dotb(a, b, dims=NN):
    return lax.dot_general(a.astype(BF16), b.astype(BF16), dims, preferred_element_type=F32)


def _params(*sem):
    return pltpu.CompilerParams(dimension_semantics=sem, vmem_limit_bytes=VMEM_LIMIT_BYTES)


def _padded_tokens(ntok, tile):
    return -(-ntok // tile) * tile


def _rms(x):
    return x * lax.rsqrt(jnp.mean(x * x, axis=-1, keepdims=True) + NORM_EPS)


def _sigmoid(x):
    return 1.0 / (1.0 + jnp.exp(-x))


def _ada_kernel(c_ref, w_ref, b_ref, o_ref):
    c = c_ref[...]
    o_ref[0] = _dotx(c * _sigmoid(c), w_ref[0]) + b_ref[0]


def _ada(cond8, ada_w, ada_b):
    depth, d, nd = ada_w.shape
    return pl.pallas_call(
        _ada_kernel,
        grid=(depth, nd // d),
        in_specs=[pl.BlockSpec((8, d), lambda i, n: (0, 0)),
                  pl.BlockSpec((1, d, d), lambda i, n: (i, 0, n)),
                  pl.BlockSpec((1, 1, d), lambda i, n: (i, 0, n))],
        out_specs=pl.BlockSpec((1, 8, d), lambda i, n: (i, 0, n)),
        out_shape=jax.ShapeDtypeStruct((depth, 8, nd), F32),
        compiler_params=_params("parallel", "parallel"),
        name="ada",
    )(cond8, ada_w, ada_b.reshape(depth, 1, nd))


def _route_padding(h2_ref, pos_ref, gd_ref):
    h2_ref[...] = jnp.zeros_like(h2_ref)
    pos_ref[...] = jnp.full(pos_ref.shape, -1, jnp.int32)
    gd_ref[...] = jnp.zeros_like(gd_ref)


def _route(x_new, mod, g2, rwt_ref, rb_ref, u_ref, off_ref, h2_ref, pos_ref, gd_ref, reset):
    h2 = _rms(x_new) * g2 * (1.0 + mod[4:5]) + mod[3:4]
    h2_ref[...] = h2.astype(BF16)
    logits = _dotx(rwt_ref[...], h2, NT) + rb_ref[...]
    n_e = logits.shape[0]
    eio = lax.broadcasted_iota(jnp.int32, logits.shape, 0).astype(F32)
    live = logits
    sels, vals = [], []
    for _ in range(TOP_K):
        m = jnp.max(live, axis=0, keepdims=True)
        idx = jnp.min(jnp.where(live == m, eio, float(n_e)), axis=0, keepdims=True)
        sel = eio == idx
        sels.append(sel)
        vals.append(m)
        live = jnp.where(sel, -jnp.inf, live)
    exps = [jnp.exp(v - vals[0]) for v in vals]
    inv = 1.0 / (exps[0] + exps[1] + exps[2] + exps[3])
    gd = jnp.zeros_like(logits)
    maskf = jnp.zeros_like(logits)
    for k in range(TOP_K):
        gd = jnp.where(sels[k], exps[k] * inv, gd)
        maskf = jnp.where(sels[k], 1.0, maskf)

    @pl.when(reset)
    def _():
        off_ref[...] = jnp.zeros_like(off_ref)

    incl = _dotb(maskf, u_ref[...])
    off = off_ref[:, 0:1]
    pos_ref[0] = jnp.where(maskf > 0.5, off + incl - 1.0, -1.0).astype(jnp.int32)
    gd_ref[0] = gd
    off_ref[...] = off_ref[...] + jnp.sum(maskf, axis=1, keepdims=True)


def _route_outputs(n_pad, d, n_e, moe_tile):
    t = TOK_TILE
    tpm = moe_tile // t
    specs = [pl.BlockSpec((t, d), lambda i: (i, 0)),
             pl.BlockSpec((1, n_e, t), lambda i: (i // tpm, 0, i % tpm)),
             pl.BlockSpec((1, n_e, t), lambda i: (i // tpm, 0, i % tpm))]
    shapes = [jax.ShapeDtypeStruct((n_pad, d), BF16),
              jax.ShapeDtypeStruct((n_pad // moe_tile, n_e, moe_tile), jnp.int32),
              jax.ShapeDtypeStruct((n_pad // moe_tile, n_e, moe_tile), F32)]
    return specs, shapes


def _route_consts(tile):
    s = np.arange(tile)
    return jnp.asarray((s[:, None] <= s[None, :]).astype(np.float32), BF16)


def _pool_kernel(x_ref, mod_ref, vec_ref, pm_ref, pw_ref, rwt_ref, rb_ref, u_ref,
                 x1_ref, h2_ref, pos_ref, gd_ref, off_ref, *, tiles_per_moe, n_real_tiles):
    i = pl.program_id(0)

    @pl.when(i >= n_real_tiles)
    def _():
        _route_padding(h2_ref, pos_ref, gd_ref)

    @pl.when(i < n_real_tiles)
    def _():
        x = x_ref[...]
        mod = mod_ref[0]
        vec = vec_ref[...]
        h = _rms(x) * vec[0:1] * (1.0 + mod[1:2]) + mod[0:1]
        gw = pw_ref.shape[1]
        ys = []
        for g in range(len(POOL_WINDOWS)):
            d = _dot3(pm_ref[g], h[:, g * gw:(g + 1) * gw])
            ys.append(_dotb(d, pw_ref[g]))
        y = jnp.concatenate(ys, axis=1) * vec[2:3]
        x1 = x + mod[2:3] * y
        x1_ref[...] = x1
        _route(x1, mod, vec[1:2], rwt_ref, rb_ref, u_ref, off_ref, h2_ref, pos_ref, gd_ref,
               i % tiles_per_moe == 0)


def _pool_matrices(tile, row_len):
    p = np.arange(tile)
    pp = p % row_len
    out = []
    for win in POOL_WINDOWS:
        lo = np.clip(pp - win // 2, 0, row_len - 1)
        hi = np.clip(pp + win // 2 - 1, 0, row_len - 1)
        cnt = (hi - lo + 1).astype(np.float64)
        same = (p[:, None] // row_len) == (p[None, :] // row_len)
        inwin = same & (pp[None, :] >= lo[:, None]) & (pp[None, :] <= hi[:, None])
        out.append(inwin / cnt[:, None] - np.eye(tile))
    return jnp.asarray(np.stack(out).astype(np.float32))


def _pool_layer(x2d, mod, vec, pool_w, rwt, rb, row_len, moe_tile, moe_span, tokens_per_batch):
    ntok, d = x2d.shape
    t = TOK_TILE
    n_e = rwt.shape[0]
    tiles_per_batch = tokens_per_batch // t
    n_real = ntok // t
    n_pad = _padded_tokens(ntok, moe_span)
    route_specs, route_shapes = _route_outputs(n_pad, d, n_e, moe_tile)
    pm = _pool_matrices(t, row_len)
    kern = functools.partial(_pool_kernel, tiles_per_moe=moe_tile // t, n_real_tiles=n_real)
    const2 = lambda i: (0, 0)
    const3 = lambda i: (0, 0, 0)
    real = lambda i: jnp.minimum(i, n_real - 1)
    return pl.pallas_call(
        kern,
        grid=(n_pad // t,),
        in_specs=[pl.BlockSpec((t, d), lambda i: (real(i), 0)),
                  pl.BlockSpec((1, 8, d), lambda i: (real(i) // tiles_per_batch, 0, 0)),
                  pl.BlockSpec((8, d), const2),
                  pl.BlockSpec(pm.shape, const3),
                  pl.BlockSpec(pool_w.shape, const3),
                  pl.BlockSpec(rwt.shape, const2),
                  pl.BlockSpec(rb.shape, const2),
                  pl.BlockSpec((t, t), const2)],
        out_specs=[pl.BlockSpec((t, d), lambda i: (real(i), 0))] + route_specs,
        out_shape=[jax.ShapeDtypeStruct((ntok, d), F32)] + route_shapes,
        scratch_shapes=[pltpu.VMEM((n_e, 128), F32)],
        compiler_params=_params("arbitrary"),
        name="pool",
    )(x2d, mod, vec, pm, pool_w, rwt, rb, _route_consts(t))


def _one_hot_rows(prow, r0, n_rows):
    rows = lax.broadcasted_iota(jnp.int32, (n_rows, prow.shape[1]), 0) + r0
    return jnp.where(prow == rows, 1.0, 0.0).astype(BF16)


def _expert_kernel(grp_ref, sub_ref, e_ref, r0_ref,
                   pos_ref, gd_ref, x_ref, wgu_ref, bgu_ref, wdt_ref, o_ref):
    b = pl.program_id(0)
    f = wdt_ref.shape[2]
    n_rows = o_ref.shape[2]
    ts = pos_ref.shape[2]
    e = e_ref[b]
    sub = sub_ref[b]
    oh = _one_hot_rows(pos_ref[sub, pl.ds(e, 1), :], r0_ref[b], n_rows)
    grow = gd_ref[sub, pl.ds(e, 1), :]
    g_hi = grow.astype(BF16)
    g_mid = (grow - g_hi.astype(F32)).astype(BF16)
    g_lo = (grow - g_hi.astype(F32) - g_mid.astype(F32)).astype(BF16)
    g3 = jnp.concatenate([g_hi, g_mid, g_lo, jnp.zeros((5, ts), BF16)], axis=0)
    x = x_ref[pl.ds(pl.multiple_of(sub * ts, ts), ts), :]
    xg = jnp.dot(oh, x, preferred_element_type=F32).astype(BF16)
    gu = jnp.dot(xg, wgu_ref[0], preferred_element_type=F32) + bgu_ref[0]
    gate = jnp.minimum(gu[:, :f], SWIGLU_LIMIT)
    up = jnp.clip(gu[:, f:], -SWIGLU_LIMIT, SWIGLU_LIMIT)
    act = (up + 1.0) * (gate * _sigmoid(SWIGLU_ALPHA * gate))
    yt = lax.dot_general(wdt_ref[0], act.astype(BF16), NT, preferred_element_type=F32)
    g8 = lax.dot_general(g3, oh, NT, preferred_element_type=F32)
    o_ref[0] = (yt * (g8[0:1] + g8[1:2] + g8[2:3])).astype(BF16)


def _combine_kernel(tile_ref, first_ref, blk_ref, e_ref, r0_ref, pos_ref, gd_ref, *rest):
    yt_refs, bdt_ref, o_ref = rest[:MOE_CHUNK_BLOCKS], rest[-2], rest[-1]
    c = pl.program_id(0)
    n_rows = yt_refs[0].shape[2]

    @pl.when(first_ref[c] == 1)
    def _():
        o_ref[...] = _dot3(bdt_ref[...], gd_ref[0])

    slot = lambda q: c * MOE_CHUNK_BLOCKS + q
    oh = jnp.concatenate([_one_hot_rows(pos_ref[0, pl.ds(e_ref[slot(q)], 1), :], r0_ref[slot(q)], n_rows)
                          for q in range(MOE_CHUNK_BLOCKS)], axis=0)
    yt = jnp.concatenate([ref[0] for ref in yt_refs], axis=1)
    o_ref[...] += jnp.dot(yt, oh, preferred_element_type=F32)


def _moe_tables(pos, group):
    nt, n_e, ts = pos.shape
    ng = nt // group
    rb, cb = MOE_ROWS, MOE_CHUNK_BLOCKS
    i32 = lambda a: a.astype(jnp.int32)
    nb = (jnp.sum(i32(pos >= 0), axis=-1) + rb - 1) // rb
    max_tile_blocks = ts * TOP_K // rb + n_e
    flat = nb.reshape(ng, group, n_e).transpose(0, 2, 1).reshape(-1)
    cum = jnp.cumsum(flat)
    start = cum - flat
    n_blocks = cum[-1]
    b = jnp.minimum(jnp.arange(nt * max_tile_blocks, dtype=jnp.int32), n_blocks - 1)
    count_le = lambda sorted_, q: jnp.sum(i32(sorted_[None, :] <= q[:, None]), axis=1)
    idx = jnp.minimum(count_le(cum, b), flat.shape[0] - 1)
    expert_tables = (i32(idx // (n_e * group)), i32(idx % group), i32((idx // group) % n_e),
                     i32((b - start[idx]) * rb))
    ce = jnp.cumsum(nb, axis=1)
    nch = (ce[:, -1] + cb - 1) // cb
    cch = jnp.cumsum(nch)
    n_chunks = cch[-1]
    max_chunks = nt * (-(-max_tile_blocks // cb))
    c = jnp.minimum(jnp.arange(max_chunks, dtype=jnp.int32), n_chunks - 1)
    tile = jnp.minimum(count_le(cch, c), nt - 1)
    lc = c - (cch - nch)[tile]
    slot = (lc[:, None] * cb + jnp.arange(cb, dtype=jnp.int32)[None, :]).reshape(-1)
    tile_s = jnp.repeat(tile, cb)
    used = slot < ce[tile_s, -1]
    slot = jnp.where(used, slot, 0)
    ce_s = ce[tile_s]
    e_s = jnp.minimum(jnp.sum(i32(slot[:, None] >= ce_s), axis=1), n_e - 1)
    local = slot - jnp.take_along_axis(ce_s - nb[tile_s], e_s[:, None], axis=1)[:, 0]
    blk = start.reshape(ng, n_e, group)[tile_s // group, e_s, tile_s % group] + local
    combine_tables = (i32(tile), i32(lc == 0), i32(blk), i32(e_s), i32(jnp.where(used, local * rb, 1 << 20)))
    return expert_tables, i32(n_blocks), combine_tables, i32(n_chunks)


def _moe(h2, pos, gd, wgu, bgu, wdt, bdt, group):
    ntok, d = h2.shape
    nt, n_e, ts = pos.shape
    f2 = wgu.shape[2]
    rb, cb = MOE_ROWS, MOE_CHUNK_BLOCKS
    expert_tables, n_blocks, combine_tables, n_chunks = _moe_tables(pos, group)
    max_blocks = expert_tables[0].shape[0]
    yt = pl.pallas_call(
        _expert_kernel,
        grid_spec=pltpu.PrefetchScalarGridSpec(
            num_scalar_prefetch=4,
            grid=(n_blocks,),
            in_specs=[pl.BlockSpec((group, n_e, ts), lambda b, gp, sb, ee, rr: (gp[b], 0, 0)),
                      pl.BlockSpec((group, n_e, ts), lambda b, gp, sb, ee, rr: (gp[b], 0, 0)),
                      pl.BlockSpec((group * ts, d), lambda b, gp, sb, ee, rr: (gp[b], 0),
                                   pipeline_mode=pl.Buffered(1)),
                      pl.BlockSpec((1, d, f2), lambda b, gp, sb, ee, rr: (ee[b], 0, 0)),
                      pl.BlockSpec((1, 1, f2), lambda b, gp, sb, ee, rr: (ee[b], 0, 0)),
                      pl.BlockSpec((1, d, f2 // 2), lambda b, gp, sb, ee, rr: (ee[b], 0, 0))],
            out_specs=pl.BlockSpec((1, d, rb), lambda b, gp, sb, ee, rr: (b, 0, 0)),
        ),
        out_shape=jax.ShapeDtypeStruct((max_blocks, d, rb), BF16),
        compiler_params=_params("arbitrary"),
        name="expert",
    )(*expert_tables, pos, gd, h2, wgu, bgu, wdt)
    yt_spec = lambda q: pl.BlockSpec((1, d, rb), lambda c, tl, fs, bk, ee, rr: (bk[c * cb + q], 0, 0))
    return pl.pallas_call(
        _combine_kernel,
        grid_spec=pltpu.PrefetchScalarGridSpec(
            num_scalar_prefetch=5,
            grid=(n_chunks,),
            in_specs=[pl.BlockSpec((1, n_e, ts), lambda c, tl, fs, bk, ee, rr: (tl[c], 0, 0)),
                      pl.BlockSpec((1, n_e, ts), lambda c, tl, fs, bk, ee, rr: (tl[c], 0, 0))]
                     + [yt_spec(q) for q in range(cb)]
                     + [pl.BlockSpec((d, n_e), lambda c, tl, fs, bk, ee, rr: (0, 0))],
            out_specs=pl.BlockSpec((d, ts), lambda c, tl, fs, bk, ee, rr: (0, tl[c]),
                                   pipeline_mode=pl.Buffered(1)),
        ),
        out_shape=jax.ShapeDtypeStruct((d, ntok), F32),
        compiler_params=_params("arbitrary"),
        name="combine",
    )(*combine_tables, pos, gd, *([yt] * cb), bdt)


def _respre_kernel(x_ref, ft_ref, modp_ref, modc_ref, vec_ref, *rest):
    x2_ref, h_ref = rest[-2], rest[-1]
    x2 = x_ref[...] + modp_ref[0][5:6] * jnp.transpose(ft_ref[...])
    x2_ref[...] = x2
    modc = modc_ref[0]
    h_ref[0] = _rms(x2) * vec_ref[0:1] * (1.0 + modc[1:2]) + modc[0:1]


def _respre(x2d, ft, modp, modc, vec, h_all, tokens_per_batch, tile_offset, seq_total):
    ntok, d = x2d.shape
    t = TOK_TILE
    nb = modp.shape[0]
    tpb = tokens_per_batch // t
    in_specs = [pl.BlockSpec((t, d), lambda i: (i, 0)),
                pl.BlockSpec((d, t), lambda i: (0, i)),
                pl.BlockSpec((1, 8, d), lambda i: (i // tpb, 0, 0)),
                pl.BlockSpec((1, 8, d), lambda i: (i // tpb, 0, 0)),
                pl.BlockSpec((8, d), lambda i: (0, 0))]
    args = [x2d, ft, modp, modc, vec]
    aliases = {}
    if h_all is not None:
        in_specs.append(pl.BlockSpec(memory_space=pl.ANY))
        args.append(h_all)
        aliases = {5: 1}
    return pl.pallas_call(
        _respre_kernel,
        grid=(ntok // t,),
        in_specs=in_specs,
        out_specs=[pl.BlockSpec((t, d), lambda i: (i, 0)),
                   pl.BlockSpec((1, t, d), lambda i: (i // tpb, tile_offset + i % tpb, 0))],
        out_shape=[jax.ShapeDtypeStruct((ntok, d), F32),
                   jax.ShapeDtypeStruct((nb, seq_total, d), F32)],
        input_output_aliases=aliases,
        compiler_params=_params("parallel"),
        name="respre",
    )(*args)


def _feat_kernel(h_ref, hp_ref, hn_ref, fv_ref, wrkv_ref, w1_ref, w2_ref, a1_ref, a2_ref, g1_ref, g2_ref,
                 hd_ref, hdt_ref,
                 lw0_ref, lw1_ref, kd0_ref, kd1_ref, b0_ref, b1_ref, v_ref, kk_ref, r_ref, g_ref,
                 *, n_ctx_tiles, n_tiles):
    i = pl.program_id(1)
    h = h_ref[0]
    t = h.shape[0]
    fv = fv_ref[...]
    first = jnp.logical_or(i == 0, i == n_ctx_tiles)
    last = jnp.logical_or(i == n_ctx_tiles - 1, i == n_tiles - 1)
    prow = jnp.where(first, 0.0, hp_ref[0][7:8])
    nrow = jnp.where(last, 0.0, hn_ref[0][0:1])
    rio = lax.broadcasted_iota(jnp.int32, h.shape, 0)
    hdn = jnp.where(rio == 0, prow, pltpu.roll(h, 1, 0))
    hup = jnp.where(rio == t - 1, nrow, pltpu.roll(h, t - 1, 0))
    xx = 0.5 * (hdn + hup) - h

    mix = lambda m: h + xx * fv[m:m + 1]
    r = _dotb(mix(0), wrkv_ref[0])
    k = _dotb(mix(2), wrkv_ref[1])
    v = _dotb(mix(3), wrkv_ref[2])
    tw = jnp.tanh(_dotb(mix(1), w1_ref[...]))
    la = _dotb(mix(4), a1_ref[...])
    g = _dotb(_sigmoid(_dotb(mix(5), g1_ref[...])), g2_ref[...])

    kkraw = k * fv[10:11]
    ss = _headsum(kkraw * kkraw, hd_ref[...], hdt_ref[...])
    kk = kkraw / jnp.maximum(jnp.sqrt(ss), 1e-12)
    k_a = fv[11:12]
    decay_scale = float(np.exp(-0.5))
    for d, (lw_ref, kd_ref, b_ref) in enumerate(((lw0_ref, kd0_ref, b0_ref), (lw1_ref, kd1_ref, b1_ref))):
        zw = fv[6 + d:7 + d] + _dotb(tw, w2_ref[d])
        lw_ref[0] = -decay_scale * _sigmoid(zw)
        icl = _sigmoid(fv[8 + d:9 + d] + _dotb(la, a2_ref[d]))
        kd_ref[0] = k * (1.0 + (icl - 1.0) * k_a)
        b_ref[0] = kk * icl
    v_ref[0] = v
    kk_ref[0] = kk
    r_ref[0] = r
    g_ref[0] = g


def _head_indicators(d):
    hd = np.zeros((d, HEAD_PAD), np.float32)
    hd[np.arange(d), np.arange(d) // RWKV_HEAD] = 1.0
    return jnp.asarray(hd), jnp.asarray(hd.T.copy())


def _features(h_all, fv, wrkv, w1c, w2p, a1c, a2p, g1, g2, hd, hdt, n_ctx_tiles):
    nb, s, d = h_all.shape
    t = TOK_TILE
    nt = s // t
    r8 = t // 8
    kern = functools.partial(_feat_kernel, n_ctx_tiles=n_ctx_tiles, n_tiles=nt)
    full = lambda a: pl.BlockSpec(a.shape, lambda b, i: (0,) * a.ndim)
    tok = pl.BlockSpec((1, t, d), lambda b, i: (b, i, 0))
    return pl.pallas_call(
        kern,
        grid=(nb, nt),
        in_specs=[tok,
                  pl.BlockSpec((1, 8, d), lambda b, i: (b, jnp.maximum(i * r8 - 1, 0), 0)),
                  pl.BlockSpec((1, 8, d), lambda b, i: (b, jnp.minimum((i + 1) * r8, s // 8 - 1), 0)),
                  full(fv), full(wrkv), full(w1c), full(w2p), full(a1c), full(a2p), full(g1), full(g2),
                  full(hd), full(hdt)],
        out_specs=[tok] * 10,
        out_shape=[jax.ShapeDtypeStruct((nb, s, d), F32)] * 10,
        compiler_params=_params("parallel", "parallel"),
        name="feat",
    )(h_all, h_all, h_all, fv, wrkv, w1c, w2p, a1c, a2p, g1, g2, hd, hdt)


def _scan_masks():
    n = GROUP_HEADS * CHUNK
    rr = np.arange(n)[:, None]
    cc = np.arange(n)[None, :]
    bd = (rr // CHUNK) == (cc // CHUNK)
    r_, c_ = rr % CHUNK, cc % CHUNK
    out = []
    for rev in (False, True):
        strict = bd & ((c_ > r_) if rev else (c_ < r_))
        incl = bd & ((c_ >= r_) if rev else (c_ <= r_))
        d16 = strict & ((r_ // 16) == (c_ // 16))
        l1 = strict & ((r_ // 32) == (c_ // 32)) & ((r_ // 16) != (c_ // 16))
        l2 = strict & ((r_ // 32) != (c_ // 32))
        out.append([strict, incl, d16, l1, l2])
    masks = [out[0][m] for m in range(5)] + [out[1][m] for m in range(5)] + [bd, rr == cc]
    hm = np.zeros((8, GROUP), np.float32)
    for j in range(GROUP_HEADS):
        hm[j, j * RWKV_HEAD:(j + 1) * RWKV_HEAD] = 1.0
    tri3 = np.stack([np.tile(out[rev][1][0:CHUNK, 0:CHUNK], (1, 3)) for rev in (0, 1)]).astype(np.float32)
    return jnp.asarray(np.stack(masks).astype(np.float32)), jnp.asarray(tri3, BF16), jnp.asarray(hm)


def _split3(a):
    a1 = a.astype(BF16)
    r1 = a - a1.astype(F32)
    a2 = r1.astype(BF16)
    return a1, a2, (r1 - a2.astype(F32)).astype(BF16)


def _dot3(a, b, dims=NN):
    a1, a2, _ = _split3(a)
    b1, b2, _ = _split3(b)
    mm = lambda p, q: lax.dot_general(p, q, dims, preferred_element_type=F32)
    return mm(a1, b1) + (mm(a1, b2) + mm(a2, b1))


def _dot01(a, ind):
    ib = ind.astype(BF16)
    a1, a2, _ = _split3(a)
    mm = lambda p: jnp.dot(p, ib, preferred_element_type=F32)
    return mm(a1) + mm(a2)


def _headsum(z, hd, hdt):
    return _dot01(_dot01(z, hd), hdt)


def _chunk_terms(blocks, msk_ref, tri_ref, hm):
    c_len = blocks[0][0].shape[0]
    n4 = GROUP_HEADS * c_len
    revs = [blk[6] for blk in blocks]
    mask = lambda k: [msk_ref[(5 if rev else 0) + k] for rev in revs]
    bd, eye = msk_ref[10], msk_ref[11]
    each = lambda f, *cols: [f(*args) for args in zip(*cols)]
    stack4 = lambda z: jnp.concatenate([z * hm[j:j + 1] for j in range(GROUP_HEADS)], axis=0)
    tile4 = lambda z: jnp.concatenate([z] * GROUP_HEADS, axis=0)
    fold4 = lambda z: sum(z[j * c_len:(j + 1) * c_len] for j in range(GROUP_HEADS))
    lw, kd, b, v, kk, r = [[blk[k] for blk in blocks] for k in range(6)]

    tri3 = [tri_ref[1 if rev else 0] for rev in revs]
    parts = each(lambda z: jnp.concatenate(_split3(z), axis=0), lw)
    c = each(lambda t, p: jnp.dot(t, p, preferred_element_type=F32), tri3, parts)
    ctot = each(lambda c_, rev: c_[0:1] if rev else c_[c_len - 1:c_len], c, revs)
    en = each(lambda c_: jnp.exp(-c_), c)
    khat = each(lambda kk_, c_, lw_: kk_ * jnp.exp(c_ - lw_), kk, c, lw)
    rhat = each(lambda r_, c_: r_ * jnp.exp(c_), r, c)
    bch = each(jnp.multiply, b, en)
    kch = each(jnp.multiply, kd, en)
    et = each(jnp.exp, ctot)
    btil = each(jnp.multiply, bch, et)
    ktil = each(jnp.multiply, kch, et)
    kh4, rh4, v4 = each(stack4, khat), each(stack4, rhat), each(stack4, v)
    sc = each(lambda k_, r_, b_, c_: _dotb(jnp.concatenate([k_, r_], axis=0),
                                           jnp.concatenate([tile4(b_), tile4(c_)], axis=0), NT),
              kh4, rh4, bch, kch)
    a_bk = each(lambda s, m_: s[0:n4, 0:n4] * m_, sc, mask(0))
    a_kk = each(lambda s, m_: s[0:n4, n4:2 * n4] * m_, sc, mask(0))
    b_br = each(lambda s, m_: s[n4:2 * n4, 0:n4] * m_, sc, mask(1))
    b_kr = each(lambda s, m_: s[n4:2 * n4, n4:2 * n4] * m_, sc, mask(1))
    avp = each(lambda a_, b_, v_: _dotb(jnp.concatenate([a_, b_], axis=0), v_), a_kk, b_kr, v4)
    av4 = [z[0:n4] for z in avp]
    p2 = [z[n4:2 * n4] for z in avp]
    ad = each(jnp.multiply, a_bk, mask(2))
    a2 = each(_dotb, ad, ad)
    times_and_square = lambda x_, p_: _dotb(jnp.concatenate([x_, p_], axis=0), p_)
    x = [eye - ad_ for ad_ in ad]
    both = each(times_and_square, x, a2)
    x = each(lambda x_, z: x_ + z[0:n4], x, both)
    a4 = [z[n4:2 * n4] for z in both]
    both = each(times_and_square, x, a4)
    x = each(lambda x_, z: x_ + z[0:n4], x, both)
    a8 = [z[n4:2 * n4] for z in both]
    dinv = each(lambda x_, a8_: x_ + _dotb(x_, a8_), x, a8)
    t1 = each(lambda d_, a_, m_: _dotb(d_, a_ * m_), dinv, a_bk, mask(3))
    b1 = each(lambda d_, t_: d_ - _dotb(t_, d_), dinv, t1)
    t2 = each(lambda b_, a_, m_: _dotb(b_, a_ * m_), b1, a_bk, mask(4))
    tinv = each(lambda b_, t_: b_ - _dotb(t_, b_), b1, t2)
    wu = each(lambda t_, k_, a_: _dotb(t_, jnp.concatenate([k_, a_], axis=1)), tinv, kh4, av4)
    p1 = each(_dotb, b_br, wu)
    q = GROUP
    wy = each(lambda r_, p_: fold4(r_ - p_[:, 0:q]), rh4, p1)
    uy = each(lambda p2_, p_: fold4(p2_ - p_[:, q:2 * q]), p2, p1)
    w1 = each(lambda w_: fold4(w_[:, 0:q]), wu)
    u = each(lambda w_: fold4(w_[:, q:2 * q]), wu)
    om = each(lambda k_, b_, v_, u_, w_: _dotb(
        jnp.concatenate([k_, b_], axis=0),
        jnp.concatenate([jnp.concatenate([v_, jnp.zeros_like(v_)], axis=1),
                         jnp.concatenate([-u_, -w_], axis=1)], axis=0), TN), ktil, btil, v, u, w1)
    o = [bd * z[:, 0:q] for z in om]
    m = each(lambda e_, z: eye * e_ + bd * z[:, q:2 * q], et, om)
    return list(zip(wy, uy, m, o))


def _scan_kernel(lwf, kdf, bf, vf, kkf, rf, lwb, kdb, bb, vb, kkb, rb, msk_ref, tri_ref, hm_ref,
                 yf_ref, yb_ref, st_ref):
    @pl.when(pl.program_id(1) == 0)
    def _():
        st_ref[...] = jnp.zeros_like(st_ref)

    hm = hm_ref[...]
    n_batch = lwf.shape[0]
    n_chunks = lwf.shape[1] // CHUNK
    dirs = ((False, (lwf, kdf, bf, vf, kkf, rf), yf_ref), (True, (lwb, kdb, bb, vb, kkb, rb), yb_ref))
    seqs = [(rev, refs, y_ref, bi) for rev, refs, y_ref in dirs for bi in range(n_batch)]
    order = lambda rev: list(reversed(range(n_chunks))) if rev else list(range(n_chunks))
    sl = lambda rev, k: slice(order(rev)[k] * CHUNK, (order(rev)[k] + 1) * CHUNK)
    blocks = [tuple(ref[bi, sl(rev, k), :] for ref in refs) + (rev,)
              for k in range(n_chunks) for rev, refs, _, bi in seqs]
    terms = _chunk_terms(blocks, msk_ref, tri_ref, hm)
    st = [st_ref[si] for si in range(len(seqs))]
    c_len = CHUNK
    for k in range(n_chunks):
        lhs = [jnp.concatenate(terms[k * len(seqs) + si][0::2], axis=0) for si in range(len(seqs))]
        lp = [_split3(z) for z in lhs]
        sp = [_split3(z) for z in st]
        res = [jnp.dot(jnp.concatenate([l[0], l[1], l[0]], axis=1), jnp.concatenate([s[0], s[0], s[1]], axis=0),
                       preferred_element_type=F32) for l, s in zip(lp, sp)]
        n_l = c_len + GROUP
        for si, (rev, _, y_ref, bi) in enumerate(seqs):
            _, uy, _, o = terms[k * len(seqs) + si]
            y_ref[bi, sl(rev, k), :] = res[si][0:c_len] + uy
            st[si] = res[si][c_len:n_l] + o
    for si in range(len(seqs)):
        st_ref[si] = st[si]


def _scan(feats, n_ctx_tiles):
    lw0, lw1, kd0, kd1, b0, b1, v, kk, r = feats
    nb, s, d = v.shape
    t = TOK_TILE
    nt = s // t
    msk, tri3, hm = _scan_masks()
    fwd = pl.BlockSpec((nb, t, GROUP), lambda q, i: (0, i, q))
    bwd_idx = lambda i: jnp.where(i < n_ctx_tiles, n_ctx_tiles - 1 - i, nt - 1 - (i - n_ctx_tiles))
    bwd = pl.BlockSpec((nb, t, GROUP), lambda q, i: (0, bwd_idx(i), q))
    return pl.pallas_call(
        _scan_kernel,
        grid=(d // GROUP, nt),
        in_specs=[fwd] * 6 + [bwd] * 6 + [pl.BlockSpec(msk.shape, lambda q, i: (0, 0, 0)),
                                          pl.BlockSpec(tri3.shape, lambda q, i: (0, 0, 0)),
                                          pl.BlockSpec(hm.shape, lambda q, i: (0, 0))],
        out_specs=[fwd, bwd],
        out_shape=[jax.ShapeDtypeStruct((nb, s, d), F32)] * 2,
        scratch_shapes=[pltpu.VMEM((2 * nb, GROUP, GROUP), F32)],
        compiler_params=_params("parallel", "arbitrary"),
        name="scan",
    )(lw0, kd0, b0, v, kk, r, lw1, kd1, b1, v, kk, r, msk, tri3, hm)


def _readout_kernel(yf_ref, yb_ref, r_ref, kd0_ref, kd1_ref, v_ref, g_ref, x_ref, mod_ref, vec_ref,
                    wo_ref, hd_ref, hdt_ref, rwt_ref, rb_ref, u_ref,
                    x3_ref, h2_ref, pos_ref, gd_ref, off_ref, *, tiles_per_moe, n_real_tiles):
    i = pl.program_id(0)

    @pl.when(i >= n_real_tiles)
    def _():
        _route_padding(h2_ref, pos_ref, gd_ref)

    @pl.when(i < n_real_tiles)
    def _():
        vec = vec_ref[...]
        mod = mod_ref[0]
        headsum = lambda z: _headsum(z, hd_ref[...], hdt_ref[...])
        inv_k = 1.0 / RWKV_HEAD
        y = yf_ref[0] + yb_ref[0]
        yc = y - headsum(y) * inv_k
        var = headsum(yc * yc) * inv_k
        yn = yc * lax.rsqrt(var + GN_EPS) * vec[0:1] + vec[1:2]
        bonus = headsum(r_ref[0] * (kd0_ref[0] + kd1_ref[0]) * vec[2:3]) * v_ref[0]
        out = (yn + bonus) * g_ref[0]
        x3 = x_ref[...] + mod[2:3] * _dotb(out, wo_ref[...])
        x3_ref[...] = x3
        _route(x3, mod, vec[3:4], rwt_ref, rb_ref, u_ref, off_ref, h2_ref, pos_ref, gd_ref,
               i % tiles_per_moe == 0)


def _readout(yf, yb, r, kd0, kd1, v, g, x2d, mod, vec, wo, hd, hdt, rwt, rb, n_ctx_tiles, moe_tile,
             moe_span, tokens_per_batch):
    ntok, d = x2d.shape
    t = TOK_TILE
    n_e = rwt.shape[0]
    tpb = tokens_per_batch // t
    n_real = ntok // t
    n_pad = _padded_tokens(ntok, moe_span)
    route_specs, route_shapes = _route_outputs(n_pad, d, n_e, moe_tile)
    real = lambda i: jnp.minimum(i, n_real - 1)
    seq = pl.BlockSpec((1, t, d), lambda i: (real(i) // tpb, n_ctx_tiles + real(i) % tpb, 0))
    tokspec = pl.BlockSpec((t, d), lambda i: (real(i), 0))
    full = lambda a: pl.BlockSpec(a.shape, lambda i: (0,) * a.ndim)
    u = _route_consts(t)
    kern = functools.partial(_readout_kernel, tiles_per_moe=moe_tile // t, n_real_tiles=n_real)
    return pl.pallas_call(
        kern,
        grid=(n_pad // t,),
        in_specs=[seq] * 7 + [tokspec, pl.BlockSpec((1, 8, d), lambda i: (real(i) // tpb, 0, 0)),
                              full(vec), full(wo), full(hd), full(hdt), full(rwt), full(rb), full(u)],
        out_specs=[tokspec] + route_specs,
        out_shape=[jax.ShapeDtypeStruct((ntok, d), F32)] + route_shapes,
        scratch_shapes=[pltpu.VMEM((n_e, 128), F32)],
        compiler_params=_params("arbitrary"),
        name="readout",
    )(yf, yb, r, kd0, kd1, v, g, x2d, mod, vec, wo, hd, hdt, rwt, rb, u)


def _final_kernel(x_ref, ft_ref, mod_ref, g_ref, o_ref):
    x = x_ref[...] + mod_ref[0][5:6] * jnp.transpose(ft_ref[...])
    o_ref[...] = _rms(x) * g_ref[0:1]


def _final(x2d, ft, mod, gvec, tokens_per_batch):
    ntok, d = x2d.shape
    t = TOK_TILE
    tpb = tokens_per_batch // t
    return pl.pallas_call(
        _final_kernel,
        grid=(ntok // t,),
        in_specs=[pl.BlockSpec((t, d), lambda i: (i, 0)),
                  pl.BlockSpec((d, t), lambda i: (0, i)),
                  pl.BlockSpec((1, 8, d), lambda i: (i // tpb, 0, 0)),
                  pl.BlockSpec((8, d), lambda i: (0, 0))],
        out_specs=pl.BlockSpec((t, d), lambda i: (i, 0)),
        out_shape=jax.ShapeDtypeStruct((ntok, d), F32),
        compiler_params=_params("parallel"),
        name="final",
    )(x2d, ft, mod, gvec)


def _rows8(rows, d):
    n = -(-len(rows) // 8) * 8
    out = jnp.zeros((n, d), F32)
    return out.at[:len(rows)].set(jnp.stack([jnp.asarray(r, F32) for r in rows]))


def _pad_lora(w, total):
    two, r, d = w.shape
    out = jnp.zeros((two, total, d), w.dtype)
    for dd in range(two):
        out = out.at[dd, dd * r:(dd + 1) * r].set(w[dd])
    return out


def kernel(x, c, ctx, c_ctx, norm_g, ada_w, ada_b, pool_w, pool_ls, rwkv_mu, rwkv_w_rkv, rwkv_w0, rwkv_w1, rwkv_w2, rwkv_a0, rwkv_a1, rwkv_a2, rwkv_g1, rwkv_g2, rwkv_k_k, rwkv_k_a, rwkv_r_k, rwkv_ln_g, rwkv_ln_b, rwkv_w_o, moe_router_w, moe_router_b, moe_w_gu, moe_b_gu, moe_w_down, moe_b_down, final_g):
    nb, seq, d = x.shape
    n_ctx = ctx.shape[1]
    depth = norm_g.shape[0]
    n_e = moe_router_w.shape[2]
    t = TOK_TILE
    assert depth == 2 and nb <= 7 and seq % t == 0 and n_ctx % t == 0 and t % GRID_W == 0 and d % GROUP == 0
    n_lat, n_ctxtok = nb * seq, nb * n_ctx
    moe_tile_lat = min(MOE_TILE, n_lat)
    moe_tile_ctx = min(MOE_TILE, n_ctxtok)
    group_lat = min(MOE_GROUP, -(-n_lat // moe_tile_lat))
    group_ctx = min(MOE_GROUP, -(-n_ctxtok // moe_tile_ctx))
    n_ctx_tiles = n_ctx // t
    seq_total = n_ctx + seq

    cond8 = jnp.zeros((8, d), F32).at[:nb].set(c).at[nb].set(c_ctx)
    ada = _ada(cond8, ada_w, ada_b).reshape(depth, 8, N_MOD, d)
    pad8 = lambda m: jnp.pad(m, ((0, 0), (0, 8 - N_MOD), (0, 0)))
    mod_lat = [pad8(ada[i, :nb]) for i in range(depth)]
    mod_ctx = [pad8(jnp.broadcast_to(ada[i, nb][None], (nb, N_MOD, d))) for i in range(depth)]

    def expert_params(i):
        return (moe_w_gu[i].astype(BF16), moe_b_gu[i].reshape(n_e, 1, -1),
                jnp.swapaxes(moe_w_down[i], 1, 2).astype(BF16), moe_b_down[i].T)

    def router_params(i):
        return moe_router_w[i].T, moe_router_b[i].reshape(n_e, 1)

    x2d = x.reshape(n_lat, d)
    ctx2d = ctx.reshape(n_ctxtok, d)

    vec0 = _rows8([norm_g[0, 0], norm_g[0, 1], pool_ls[0]], d)
    pw = pool_w[0].astype(BF16)
    rwt0, rb0 = router_params(0)
    ex0 = expert_params(0)
    x1, h2, pos, gd = _pool_layer(x2d, mod_lat[0], vec0, pw, rwt0, rb0, GRID_W, moe_tile_lat,
                                  group_lat * moe_tile_lat, seq)
    ft_lat = _moe(h2, pos, gd, *ex0, group_lat)
    c1, h2c, posc, gdc = _pool_layer(ctx2d, mod_ctx[0], vec0, pw, rwt0, rb0, n_ctx, moe_tile_ctx,
                                     group_ctx * moe_tile_ctx, n_ctx)
    ft_ctx = _moe(h2c, posc, gdc, *ex0, group_ctx)

    vec1 = _rows8([norm_g[1, 0]], d)
    x2, h_all = _respre(x1, ft_lat, mod_lat[0], mod_lat[1], vec1, None, seq, n_ctx_tiles, seq_total)
    _, h_all = _respre(c1, ft_ctx, mod_ctx[0], mod_ctx[1], vec1, h_all, n_ctx, 0, seq_total)

    fv = _rows8([rwkv_mu[0, m] for m in range(6)]
                + [rwkv_w0[0, 0], rwkv_w0[0, 1], rwkv_a0[0, 0], rwkv_a0[0, 1], rwkv_k_k[0], rwkv_k_a[0]], d)
    lora = rwkv_w1.shape[3]
    cat2 = lambda w: jnp.concatenate([w[0, 0], w[0, 1]], axis=1).astype(BF16)
    hd, hdt = _head_indicators(d)
    feats = _features(h_all, fv, rwkv_w_rkv[0].astype(BF16),
                      cat2(rwkv_w1), _pad_lora(rwkv_w2[0], 2 * lora).astype(BF16),
                      cat2(rwkv_a1), _pad_lora(rwkv_a2[0], 2 * rwkv_a1.shape[3]).astype(BF16),
                      rwkv_g1[0].astype(BF16), rwkv_g2[0].astype(BF16), hd, hdt, n_ctx_tiles)
    lw0, lw1, kd0, kd1, b0, b1, v, kk, r, g = feats
    yf, yb = _scan((lw0, lw1, kd0, kd1, b0, b1, v, kk, r), n_ctx_tiles)

    vec_ro = _rows8([rwkv_ln_g[0], rwkv_ln_b[0], rwkv_r_k[0].reshape(-1), norm_g[1, 1]], d)
    rwt1, rb1 = router_params(1)
    x3, h2, pos, gd = _readout(yf, yb, r, kd0, kd1, v, g, x2, mod_lat[1], vec_ro,
                               rwkv_w_o[0].astype(BF16), hd, hdt, rwt1, rb1, n_ctx_tiles, moe_tile_lat,
                               group_lat * moe_tile_lat, seq)
    ft = _moe(h2, pos, gd, *expert_params(1), group_lat)
    out = _final(x3, ft, mod_lat[1], _rows8([final_g], d), seq)
    return out.reshape(nb, seq, d)
```

```python
import functools

import numpy as np
import jax
import jax.numpy as jnp
from jax import lax
from jax.experimental import pallas as pl
from jax.experimental.pallas import tpu as pltpu

F32 = jnp.float32
BF16 = jnp.bfloat16
NN = (((1,), (0,)), ((), ()))
NT = (((1,), (1,)), ((), ()))
TN = (((0,), (0,)), ((), ()))

N_MOD = 6
NORM_EPS = 1e-6
GRID_W = 64
POOL_WINDOWS = (2, 4, 8, 16)
RWKV_HEAD = 64
GN_EPS = 64e-5
TOP_K = 4
SWIGLU_ALPHA = 1.702
SWIGLU_LIMIT = 7.0

TOK_TILE = 256
MOE_TILE = 1536
MOE_GROUP = 4
MOE_ROWS = 256
MOE_CHUNK_BLOCKS = 4
CHUNK = 64
GROUP_HEADS = 2
GROUP = GROUP_HEADS * RWKV_HEAD
HEAD_PAD = 128
VMEM_LIMIT_BYTES = 56 * 1024 * 1024


def _dotx(a, b, dims=NN):
    return lax.dot_general(a, b, dims, precision=lax.Precision.HIGHEST, preferred_element_type=F32)


def _dotb(a, b, dims=NN):
    return lax.dot_general(a.astype(BF16), b.astype(BF16), dims, preferred_element_type=F32)


def _params(*sem):
    return pltpu.CompilerParams(dimension_semantics=sem, vmem_limit_bytes=VMEM_LIMIT_BYTES)


def _padded_tokens(ntok, tile):
    return -(-ntok // tile) * tile


def _rms(x):
    return x * lax.rsqrt(jnp.mean(x * x, axis=-1, keepdims=True) + NORM_EPS)


def _sigmoid(x):
    return 1.0 / (1.0 + jnp.exp(-x))


def _ada_kernel(c_ref, w_ref, b_ref, o_ref):
    c = c_ref[...]
    o_ref[0] = _dotx(c * _sigmoid(c), w_ref[0]) + b_ref[0]


def _ada(cond8, ada_w, ada_b):
    depth, d, nd = ada_w.shape
    return pl.pallas_call(
        _ada_kernel,
        grid=(depth, nd // d),
        in_specs=[pl.BlockSpec((8, d), lambda i, n: (0, 0)),
                  pl.BlockSpec((1, d, d), lambda i, n: (i, 0, n)),
                  pl.BlockSpec((1, 1, d), lambda i, n: (i, 0, n))],
        out_specs=pl.BlockSpec((1, 8, d), lambda i, n: (i, 0, n)),
        out_shape=jax.ShapeDtypeStruct((depth, 8, nd), F32),
        compiler_params=_params("parallel", "parallel"),
        name="ada",
    )(cond8, ada_w, ada_b.reshape(depth, 1, nd))


def _route_padding(h2_ref, pos_ref, gd_ref):
    h2_ref[...] = jnp.zeros_like(h2_ref)
    pos_ref[...] = jnp.full(pos_ref.shape, -1, jnp.int32)
    gd_ref[...] = jnp.zeros_like(gd_ref)


def _route(x_new, mod, g2, rwt_ref, rb_ref, u_ref, off_ref, h2_ref, pos_ref, gd_ref, reset):
    h2 = _rms(x_new) * g2 * (1.0 + mod[4:5]) + mod[3:4]
    h2_ref[...] = h2.astype(BF16)
    logits = _dotx(rwt_ref[...], h2, NT) + rb_ref[...]
    n_e = logits.shape[0]
    eio = lax.broadcasted_iota(jnp.int32, logits.shape, 0).astype(F32)
    live = logits
    sels, vals = [], []
    for _ in range(TOP_K):
        m = jnp.max(live, axis=0, keepdims=True)
        idx = jnp.min(jnp.where(live == m, eio, float(n_e)), axis=0, keepdims=True)
        sel = eio == idx
        sels.append(sel)
        vals.append(m)
        live = jnp.where(sel, -jnp.inf, live)
    exps = [jnp.exp(v - vals[0]) for v in vals]
    inv = 1.0 / (exps[0] + exps[1] + exps[2] + exps[3])
    gd = jnp.zeros_like(logits)
    maskf = jnp.zeros_like(logits)
    for k in range(TOP_K):
        gd = jnp.where(sels[k], exps[k] * inv, gd)
        maskf = jnp.where(sels[k], 1.0, maskf)

    @pl.when(reset)
    def _():
        off_ref[...] = jnp.zeros_like(off_ref)

    incl = _dotb(maskf, u_ref[...])
    off = off_ref[:, 0:1]
    pos_ref[0] = jnp.where(maskf > 0.5, off + incl - 1.0, -1.0).astype(jnp.int32)
    gd_ref[0] = gd
    off_ref[...] = off_ref[...] + jnp.sum(maskf, axis=1, keepdims=True)


def _route_outputs(n_pad, d, n_e, moe_tile):
    t = TOK_TILE
    tpm = moe_tile // t
    specs = [pl.BlockSpec((t, d), lambda i: (i, 0)),
             pl.BlockSpec((1, n_e, t), lambda i: (i // tpm, 0, i % tpm)),
             pl.BlockSpec((1, n_e, t), lambda i: (i // tpm, 0, i % tpm))]
    shapes = [jax.ShapeDtypeStruct((n_pad, d), BF16),
              jax.ShapeDtypeStruct((n_pad // moe_tile, n_e, moe_tile), jnp.int32),
              jax.ShapeDtypeStruct((n_pad // moe_tile, n_e, moe_tile), F32)]
    return specs, shapes


def _route_consts(tile):
    s = np.arange(tile)
    return jnp.asarray((s[:, None] <= s[None, :]).astype(np.float32), BF16)


def _pool_kernel(x_ref, mod_ref, vec_ref, pm_ref, pw_ref, rwt_ref, rb_ref, u_ref,
                 x1_ref, h2_ref, pos_ref, gd_ref, off_ref, *, tiles_per_moe, n_real_tiles):
    i = pl.program_id(0)

    @pl.when(i >= n_real_tiles)
    def _():
        _route_padding(h2_ref, pos_ref, gd_ref)

    @pl.when(i < n_real_tiles)
    def _():
        x = x_ref[...]
        mod = mod_ref[0]
        vec = vec_ref[...]
        h = _rms(x) * vec[0:1] * (1.0 + mod[1:2]) + mod[0:1]
        gw = pw_ref.shape[1]
        ys = []
        for g in range(len(POOL_WINDOWS)):
            d = _dot3(pm_ref[g], h[:, g * gw:(g + 1) * gw])
            ys.append(_dotb(d, pw_ref[g]))
        y = jnp.concatenate(ys, axis=1) * vec[2:3]
        x1 = x + mod[2:3] * y
        x1_ref[...] = x1
        _route(x1, mod, vec[1:2], rwt_ref, rb_ref, u_ref, off_ref, h2_ref, pos_ref, gd_ref,
               i % tiles_per_moe == 0)


def _pool_matrices(tile, row_len):
    p = np.arange(tile)
    pp = p % row_len
    out = []
    for win in POOL_WINDOWS:
        lo = np.clip(pp - win // 2, 0, row_len - 1)
        hi = np.clip(pp + win // 2 - 1, 0, row_len - 1)
        cnt = (hi - lo + 1).astype(np.float64)
        same = (p[:, None] // row_len) == (p[None, :] // row_len)
        inwin = same & (pp[None, :] >= lo[:, None]) & (pp[None, :] <= hi[:, None])
        out.append(inwin / cnt[:, None] - np.eye(tile))
    return jnp.asarray(np.stack(out).astype(np.float32))


def _pool_layer(x2d, mod, vec, pool_w, rwt, rb, row_len, moe_tile, moe_span, tokens_per_batch):
    ntok, d = x2d.shape
    t = TOK_TILE
    n_e = rwt.shape[0]
    tiles_per_batch = tokens_per_batch // t
    n_real = ntok // t
    n_pad = _padded_tokens(ntok, moe_span)
    route_specs, route_shapes = _route_outputs(n_pad, d, n_e, moe_tile)
    pm = _pool_matrices(t, row_len)
    kern = functools.partial(_pool_kernel, tiles_per_moe=moe_tile // t, n_real_tiles=n_real)
    const2 = lambda i: (0, 0)
    const3 = lambda i: (0, 0, 0)
    real = lambda i: jnp.minimum(i, n_real - 1)
    return pl.pallas_call(
        kern,
        grid=(n_pad // t,),
        in_specs=[pl.BlockSpec((t, d), lambda i: (real(i), 0)),
                  pl.BlockSpec((1, 8, d), lambda i: (real(i) // tiles_per_batch, 0, 0)),
                  pl.BlockSpec((8, d), const2),
                  pl.BlockSpec(pm.shape, const3),
                  pl.BlockSpec(pool_w.shape, const3),
                  pl.BlockSpec(rwt.shape, const2),
                  pl.BlockSpec(rb.shape, const2),
                  pl.BlockSpec((t, t), const2)],
        out_specs=[pl.BlockSpec((t, d), lambda i: (real(i), 0))] + route_specs,
        out_shape=[jax.ShapeDtypeStruct((ntok, d), F32)] + route_shapes,
        scratch_shapes=[pltpu.VMEM((n_e, 128), F32)],
        compiler_params=_params("arbitrary"),
        name="pool",
    )(x2d, mod, vec, pm, pool_w, rwt, rb, _route_consts(t))


def _one_hot_rows(prow, r0, n_rows):
    rows = lax.broadcasted_iota(jnp.int32, (n_rows, prow.shape[1]), 0) + r0
    return jnp.where(prow == rows, 1.0, 0.0).astype(BF16)


def _expert_kernel(grp_ref, sub_ref, e_ref, r0_ref,
                   pos_ref, gd_ref, x_ref, wgu_ref, bgu_ref, wdt_ref, o_ref):
    b = pl.program_id(0)
    f = wdt_ref.shape[2]
    n_rows = o_ref.shape[2]
    ts = pos_ref.shape[2]
    e = e_ref[b]
    sub = sub_ref[b]
    oh = _one_hot_rows(pos_ref[sub, pl.ds(e, 1), :], r0_ref[b], n_rows)
    grow = gd_ref[sub, pl.ds(e, 1), :]
    g_hi = grow.astype(BF16)
    g_mid = (grow - g_hi.astype(F32)).astype(BF16)
    g_lo = (grow - g_hi.astype(F32) - g_mid.astype(F32)).astype(BF16)
    g3 = jnp.concatenate([g_hi, g_mid, g_lo, jnp.zeros((5, ts), BF16)], axis=0)
    x = x_ref[pl.ds(pl.multiple_of(sub * ts, ts), ts), :]
    xg = jnp.dot(oh, x, preferred_element_type=F32).astype(BF16)
    gu = jnp.dot(xg, wgu_ref[0], preferred_element_type=F32) + bgu_ref[0]
    gate = jnp.minimum(gu[:, :f], SWIGLU_LIMIT)
    up = jnp.clip(gu[:, f:], -SWIGLU_LIMIT, SWIGLU_LIMIT)
    act = (up + 1.0) * (gate * _sigmoid(SWIGLU_ALPHA * gate))
    yt = lax.dot_general(wdt_ref[0], act.astype(BF16), NT, preferred_element_type=F32)
    g8 = lax.dot_general(g3, oh, NT, preferred_element_type=F32)
    o_ref[0] = (yt * (g8[0:1] + g8[1:2] + g8[2:3])).astype(BF16)


def _combine_kernel(tile_ref, first_ref, blk_ref, e_ref, r0_ref, pos_ref, gd_ref, *rest):
    yt_refs, bdt_ref, o_ref = rest[:MOE_CHUNK_BLOCKS], rest[-2], rest[-1]
    c = pl.program_id(0)
    n_rows = yt_refs[0].shape[2]

    @pl.when(first_ref[c] == 1)
    def _():
        o_ref[...] = _dot3(bdt_ref[...], gd_ref[0])

    slot = lambda q: c * MOE_CHUNK_BLOCKS + q
    oh = jnp.concatenate([_one_hot_rows(pos_ref[0, pl.ds(e_ref[slot(q)], 1), :], r0_ref[slot(q)], n_rows)
                          for q in range(MOE_CHUNK_BLOCKS)], axis=0)
    yt = jnp.concatenate([ref[0] for ref in yt_refs], axis=1)
    o_ref[...] += jnp.dot(yt, oh, preferred_element_type=F32)


def _moe_tables(pos, group):
    nt, n_e, ts = pos.shape
    ng = nt // group
    rb, cb = MOE_ROWS, MOE_CHUNK_BLOCKS
    i32 = lambda a: a.astype(jnp.int32)
    nb = (jnp.sum(i32(pos >= 0), axis=-1) + rb - 1) // rb
    max_tile_blocks = ts * TOP_K // rb + n_e
    flat = nb.reshape(ng, group, n_e).transpose(0, 2, 1).reshape(-1)
    cum = jnp.cumsum(flat)
    start = cum - flat
    n_blocks = cum[-1]
    b = jnp.minimum(jnp.arange(nt * max_tile_blocks, dtype=jnp.int32), n_blocks - 1)
    count_le = lambda sorted_, q: jnp.sum(i32(sorted_[None, :] <= q[:, None]), axis=1)
    idx = jnp.minimum(count_le(cum, b), flat.shape[0] - 1)
    expert_tables = (i32(idx // (n_e * group)), i32(idx % group), i32((idx // group) % n_e),
                     i32((b - start[idx]) * rb))
    ce = jnp.cumsum(nb, axis=1)
    nch = (ce[:, -1] + cb - 1) // cb
    cch = jnp.cumsum(nch)
    n_chunks = cch[-1]
    max_chunks = nt * (-(-max_tile_blocks // cb))
    c = jnp.minimum(jnp.arange(max_chunks, dtype=jnp.int32), n_chunks - 1)
    tile = jnp.minimum(count_le(cch, c), nt - 1)
    lc = c - (cch - nch)[tile]
    slot = (lc[:, None] * cb + jnp.arange(cb, dtype=jnp.int32)[None, :]).reshape(-1)
    tile_s = jnp.repeat(tile, cb)
    used = slot < ce[tile_s, -1]
    slot = jnp.where(used, slot, 0)
    ce_s = ce[tile_s]
    e_s = jnp.minimum(jnp.sum(i32(slot[:, None] >= ce_s), axis=1), n_e - 1)
    local = slot - jnp.take_along_axis(ce_s - nb[tile_s], e_s[:, None], axis=1)[:, 0]
    blk = start.reshape(ng, n_e, group)[tile_s // group, e_s, tile_s % group] + local
    combine_tables = (i32(tile), i32(lc == 0), i32(blk), i32(e_s), i32(jnp.where(used, local * rb, 1 << 20)))
    return expert_tables, i32(n_blocks), combine_tables, i32(n_chunks)


def _moe(h2, pos, gd, wgu, bgu, wdt, bdt, group):
    ntok, d = h2.shape
    nt, n_e, ts = pos.shape
    f2 = wgu.shape[2]
    rb, cb = MOE_ROWS, MOE_CHUNK_BLOCKS
    expert_tables, n_blocks, combine_tables, n_chunks = _moe_tables(pos, group)
    max_blocks = expert_tables[0].shape[0]
    yt = pl.pallas_call(
        _expert_kernel,
        grid_spec=pltpu.PrefetchScalarGridSpec(
            num_scalar_prefetch=4,
            grid=(n_blocks,),
            in_specs=[pl.BlockSpec((group, n_e, ts), lambda b, gp, sb, ee, rr: (gp[b], 0, 0)),
                      pl.BlockSpec((group, n_e, ts), lambda b, gp, sb, ee, rr: (gp[b], 0, 0)),
                      pl.BlockSpec((group * ts, d), lambda b, gp, sb, ee, rr: (gp[b], 0),
                                   pipeline_mode=pl.Buffered(1)),
                      pl.BlockSpec((1, d, f2), lambda b, gp, sb, ee, rr: (ee[b], 0, 0)),
                      pl.BlockSpec((1, 1, f2), lambda b, gp, sb, ee, rr: (ee[b], 0, 0)),
                      pl.BlockSpec((1, d, f2 // 2), lambda b, gp, sb, ee, rr: (ee[b], 0, 0))],
            out_specs=pl.BlockSpec((1, d, rb), lambda b, gp, sb, ee, rr: (b, 0, 0)),
        ),
        out_shape=jax.ShapeDtypeStruct((max_blocks, d, rb), BF16),
        compiler_params=_params("arbitrary"),
        name="expert",
    )(*expert_tables, pos, gd, h2, wgu, bgu, wdt)
    yt_spec = lambda q: pl.BlockSpec((1, d, rb), lambda c, tl, fs, bk, ee, rr: (bk[c * cb + q], 0, 0))
    return pl.pallas_call(
        _combine_kernel,
        grid_spec=pltpu.PrefetchScalarGridSpec(
            num_scalar_prefetch=5,
            grid=(n_chunks,),
            in_specs=[pl.BlockSpec((1, n_e, ts), lambda c, tl, fs, bk, ee, rr: (tl[c], 0, 0)),
                      pl.BlockSpec((1, n_e, ts), lambda c, tl, fs, bk, ee, rr: (tl[c], 0, 0))]
                     + [yt_spec(q) for q in range(cb)]
                     + [pl.BlockSpec((d, n_e), lambda c, tl, fs, bk, ee, rr: (0, 0))],
            out_specs=pl.BlockSpec((d, ts), lambda c, tl, fs, bk, ee, rr: (0, tl[c]),
                                   pipeline_mode=pl.Buffered(1)),
        ),
        out_shape=jax.ShapeDtypeStruct((d, ntok), F32),
        compiler_params=_params("arbitrary"),
        name="combine",
    )(*combine_tables, pos, gd, *([yt] * cb), bdt)


def _respre_kernel(x_ref, ft_ref, modp_ref, modc_ref, vec_ref, *rest):
    x2_ref, h_ref = rest[-2], rest[-1]
    x2 = x_ref[...] + modp_ref[0][5:6] * jnp.transpose(ft_ref[...])
    x2_ref[...] = x2
    modc = modc_ref[0]
    h_ref[0] = _rms(x2) * vec_ref[0:1] * (1.0 + modc[1:2]) + modc[0:1]


def _respre(x2d, ft, modp, modc, vec, h_all, tokens_per_batch, tile_offset, seq_total):
    ntok, d = x2d.shape
    t = TOK_TILE
    nb = modp.shape[0]
    tpb = tokens_per_batch // t
    in_specs = [pl.BlockSpec((t, d), lambda i: (i, 0)),
                pl.BlockSpec((d, t), lambda i: (0, i)),
                pl.BlockSpec((1, 8, d), lambda i: (i // tpb, 0, 0)),
                pl.BlockSpec((1, 8, d), lambda i: (i // tpb, 0, 0)),
                pl.BlockSpec((8, d), lambda i: (0, 0))]
    args = [x2d, ft, modp, modc, vec]
    aliases = {}
    if h_all is not None:
        in_specs.append(pl.BlockSpec(memory_space=pl.ANY))
        args.append(h_all)
        aliases = {5: 1}
    return pl.pallas_call(
        _respre_kernel,
        grid=(ntok // t,),
        in_specs=in_specs,
        out_specs=[pl.BlockSpec((t, d), lambda i: (i, 0)),
                   pl.BlockSpec((1, t, d), lambda i: (i // tpb, tile_offset + i % tpb, 0))],
        out_shape=[jax.ShapeDtypeStruct((ntok, d), F32),
                   jax.ShapeDtypeStruct((nb, seq_total, d), F32)],
        input_output_aliases=aliases,
        compiler_params=_params("parallel"),
        name="respre",
    )(*args)


def _feat_kernel(h_ref, hp_ref, hn_ref, fv_ref, wrkv_ref, w1_ref, w2_ref, a1_ref, a2_ref, g1_ref, g2_ref,
                 hd_ref, hdt_ref,
                 lw0_ref, lw1_ref, kd0_ref, kd1_ref, b0_ref, b1_ref, v_ref, kk_ref, r_ref, g_ref,
                 *, n_ctx_tiles, n_tiles):
    i = pl.program_id(1)
    h = h_ref[0]
    t = h.shape[0]
    fv = fv_ref[...]
    first = jnp.logical_or(i == 0, i == n_ctx_tiles)
    last = jnp.logical_or(i == n_ctx_tiles - 1, i == n_tiles - 1)
    prow = jnp.where(first, 0.0, hp_ref[0][7:8])
    nrow = jnp.where(last, 0.0, hn_ref[0][0:1])
    rio = lax.broadcasted_iota(jnp.int32, h.shape, 0)
    hdn = jnp.where(rio == 0, prow, pltpu.roll(h, 1, 0))
    hup = jnp.where(rio == t - 1, nrow, pltpu.roll(h, t - 1, 0))
    xx = 0.5 * (hdn + hup) - h

    mix = lambda m: h + xx * fv[m:m + 1]
    r = _dotb(mix(0), wrkv_ref[0])
    k = _dotb(mix(2), wrkv_ref[1])
    v = _dotb(mix(3), wrkv_ref[2])
    tw = jnp.tanh(_dotb(mix(1), w1_ref[...]))
    la = _dotb(mix(4), a1_ref[...])
    g = _dotb(_sigmoid(_dotb(mix(5), g1_ref[...])), g2_ref[...])

    kkraw = k * fv[10:11]
    ss = _headsum(kkraw * kkraw, hd_ref[...], hdt_ref[...])
    kk = kkraw / jnp.maximum(jnp.sqrt(ss), 1e-12)
    k_a = fv[11:12]
    decay_scale = float(np.exp(-0.5))
    for d, (lw_ref, kd_ref, b_ref) in enumerate(((lw0_ref, kd0_ref, b0_ref), (lw1_ref, kd1_ref, b1_ref))):
        zw = fv[6 + d:7 + d] + _dotb(tw, w2_ref[d])
        lw_ref[0] = -decay_scale * _sigmoid(zw)
        icl = _sigmoid(fv[8 + d:9 + d] + _dotb(la, a2_ref[d]))
        kd_ref[0] = k * (1.0 + (icl - 1.0) * k_a)
        b_ref[0] = kk * icl
    v_ref[0] = v
    kk_ref[0] = kk
    r_ref[0] = r
    g_ref[0] = g


def _head_indicators(d):
    hd = np.zeros((d, HEAD_PAD), np.float32)
    hd[np.arange(d), np.arange(d) // RWKV_HEAD] = 1.0
    return jnp.asarray(hd), jnp.asarray(hd.T.copy())


def _features(h_all, fv, wrkv, w1c, w2p, a1c, a2p, g1, g2, hd, hdt, n_ctx_tiles):
    nb, s, d = h_all.shape
    t = TOK_TILE
    nt = s // t
    r8 = t // 8
    kern = functools.partial(_feat_kernel, n_ctx_tiles=n_ctx_tiles, n_tiles=nt)
    full = lambda a: pl.BlockSpec(a.shape, lambda b, i: (0,) * a.ndim)
    tok = pl.BlockSpec((1, t, d), lambda b, i: (b, i, 0))
    return pl.pallas_call(
        kern,
        grid=(nb, nt),
        in_specs=[tok,
                  pl.BlockSpec((1, 8, d), lambda b, i: (b, jnp.maximum(i * r8 - 1, 0), 0)),
                  pl.BlockSpec((1, 8, d), lambda b, i: (b, jnp.minimum((i + 1) * r8, s // 8 - 1), 0)),
                  full(fv), full(wrkv), full(w1c), full(w2p), full(a1c), full(a2p), full(g1), full(g2),
                  full(hd), full(hdt)],
        out_specs=[tok] * 10,
        out_shape=[jax.ShapeDtypeStruct((nb, s, d), F32)] * 10,
        compiler_params=_params("parallel", "parallel"),
        name="feat",
    )(h_all, h_all, h_all, fv, wrkv, w1c, w2p, a1c, a2p, g1, g2, hd, hdt)


def _scan_masks():
    n = GROUP_HEADS * CHUNK
    rr = np.arange(n)[:, None]
    cc = np.arange(n)[None, :]
    bd = (rr // CHUNK) == (cc // CHUNK)
    r_, c_ = rr % CHUNK, cc % CHUNK
    out = []
    for rev in (False, True):
        strict = bd & ((c_ > r_) if rev else (c_ < r_))
        incl = bd & ((c_ >= r_) if rev else (c_ <= r_))
        d16 = strict & ((r_ // 16) == (c_ // 16))
        l1 = strict & ((r_ // 32) == (c_ // 32)) & ((r_ // 16) != (c_ // 16))
        l2 = strict & ((r_ // 32) != (c_ // 32))
        out.append([strict, incl, d16, l1, l2])
    masks = [out[0][m] for m in range(5)] + [out[1][m] for m in range(5)] + [bd, rr == cc]
    hm = np.zeros((8, GROUP), np.float32)
    for j in range(GROUP_HEADS):
        hm[j, j * RWKV_HEAD:(j + 1) * RWKV_HEAD] = 1.0
    tri3 = np.stack([np.tile(out[rev][1][0:CHUNK, 0:CHUNK], (1, 3)) for rev in (0, 1)]).astype(np.float32)
    return jnp.asarray(np.stack(masks).astype(np.float32)), jnp.asarray(tri3, BF16), jnp.asarray(hm)


def _split3(a):
    a1 = a.astype(BF16)
    r1 = a - a1.astype(F32)
    a2 = r1.astype(BF16)
    return a1, a2, (r1 - a2.astype(F32)).astype(BF16)


def _dot3(a, b, dims=NN):
    a1, a2, _ = _split3(a)
    b1, b2, _ = _split3(b)
    mm = lambda p, q: lax.dot_general(p, q, dims, preferred_element_type=F32)
    return mm(a1, b1) + (mm(a1, b2) + mm(a2, b1))


def _dot01(a, ind):
    ib = ind.astype(BF16)
    a1, a2, _ = _split3(a)
    mm = lambda p: jnp.dot(p, ib, preferred_element_type=F32)
    return mm(a1) + mm(a2)


def _headsum(z, hd, hdt):
    return _dot01(_dot01(z, hd), hdt)


def _chunk_terms(blocks, msk_ref, tri_ref, hm):
    c_len = blocks[0][0].shape[0]
    n4 = GROUP_HEADS * c_len
    revs = [blk[6] for blk in blocks]
    mask = lambda k: [msk_ref[(5 if rev else 0) + k] for rev in revs]
    bd, eye = msk_ref[10], msk_ref[11]
    each = lambda f, *cols: [f(*args) for args in zip(*cols)]
    stack4 = lambda z: jnp.concatenate([z * hm[j:j + 1] for j in range(GROUP_HEADS)], axis=0)
    tile4 = lambda z: jnp.concatenate([z] * GROUP_HEADS, axis=0)
    fold4 = lambda z: sum(z[j * c_len:(j + 1) * c_len] for j in range(GROUP_HEADS))
    lw, kd, b, v, kk, r = [[blk[k] for blk in blocks] for k in range(6)]

    tri3 = [tri_ref[1 if rev else 0] for rev in revs]
    parts = each(lambda z: jnp.concatenate(_split3(z), axis=0), lw)
    c = each(lambda t, p: jnp.dot(t, p, preferred_element_type=F32), tri3, parts)
    ctot = each(lambda c_, rev: c_[0:1] if rev else c_[c_len - 1:c_len], c, revs)
    en = each(lambda c_: jnp.exp(-c_), c)
    khat = each(lambda kk_, c_, lw_: kk_ * jnp.exp(c_ - lw_), kk, c, lw)
    rhat = each(lambda r_, c_: r_ * jnp.exp(c_), r, c)
    bch = each(jnp.multiply, b, en)
    kch = each(jnp.multiply, kd, en)
    et = each(jnp.exp, ctot)
    btil = each(jnp.multiply, bch, et)
    ktil = each(jnp.multiply, kch, et)
    kh4, rh4, v4 = each(stack4, khat), each(stack4, rhat), each(stack4, v)
    sc = each(lambda k_, r_, b_, c_: _dotb(jnp.concatenate([k_, r_], axis=0),
                                           jnp.concatenate([tile4(b_), tile4(c_)], axis=0), NT),
              kh4, rh4, bch, kch)
    a_bk = each(lambda s, m_: s[0:n4, 0:n4] * m_, sc, mask(0))
    a_kk = each(lambda s, m_: s[0:n4, n4:2 * n4] * m_, sc, mask(0))
    b_br = each(lambda s, m_: s[n4:2 * n4, 0:n4] * m_, sc, mask(1))
    b_kr = each(lambda s, m_: s[n4:2 * n4, n4:2 * n4] * m_, sc, mask(1))
    avp = each(lambda a_, b_, v_: _dotb(jnp.concatenate([a_, b_], axis=0), v_), a_kk, b_kr, v4)
    av4 = [z[0:n4] for z in avp]
    p2 = [z[n4:2 * n4] for z in avp]
    ad = each(jnp.multiply, a_bk, mask(2))
    a2 = each(_dotb, ad, ad)
    times_and_square = lambda x_, p_: _dotb(jnp.concatenate([x_, p_], axis=0), p_)
    x = [eye - ad_ for ad_ in ad]
    both = each(times_and_square, x, a2)
    x = each(lambda x_, z: x_ + z[0:n4], x, both)
    a4 = [z[n4:2 * n4] for z in both]
    both = each(times_and_square, x, a4)
    x = each(lambda x_, z: x_ + z[0:n4], x, both)
    a8 = [z[n4:2 * n4] for z in both]
    dinv = each(lambda x_, a8_: x_ + _dotb(x_, a8_), x, a8)
    t1 = each(lambda d_, a_, m_: _dotb(d_, a_ * m_), dinv, a_bk, mask(3))
    b1 = each(lambda d_, t_: d_ - _dotb(t_, d_), dinv, t1)
    t2 = each(lambda b_, a_, m_: _dotb(b_, a_ * m_), b1, a_bk, mask(4))
    tinv = each(lambda b_, t_: b_ - _dotb(t_, b_), b1, t2)
    wu = each(lambda t_, k_, a_: _dotb(t_, jnp.concatenate([k_, a_], axis=1)), tinv, kh4, av4)
    p1 = each(_dotb, b_br, wu)
    q = GROUP
    wy = each(lambda r_, p_: fold4(r_ - p_[:, 0:q]), rh4, p1)
    uy = each(lambda p2_, p_: fold4(p2_ - p_[:, q:2 * q]), p2, p1)
    w1 = each(lambda w_: fold4(w_[:, 0:q]), wu)
    u = each(lambda w_: fold4(w_[:, q:2 * q]), wu)
    om = each(lambda k_, b_, v_, u_, w_: _dotb(
        jnp.concatenate([k_, b_], axis=0),
        jnp.concatenate([jnp.concatenate([v_, jnp.zeros_like(v_)], axis=1),
                         jnp.concatenate([-u_, -w_], axis=1)], axis=0), TN), ktil, btil, v, u, w1)
    o = [bd * z[:, 0:q] for z in om]
    m = each(lambda e_, z: eye * e_ + bd * z[:, q:2 * q], et, om)
    return list(zip(wy, uy, m, o))


def _scan_kernel(lwf, kdf, bf, vf, kkf, rf, lwb, kdb, bb, vb, kkb, rb, msk_ref, tri_ref, hm_ref,
                 yf_ref, yb_ref, st_ref):
    @pl.when(pl.program_id(1) == 0)
    def _():
        st_ref[...] = jnp.zeros_like(st_ref)

    hm = hm_ref[...]
    n_batch = lwf.shape[0]
    n_chunks = lwf.shape[1] // CHUNK
    dirs = ((False, (lwf, kdf, bf, vf, kkf, rf), yf_ref), (True, (lwb, kdb, bb, vb, kkb, rb), yb_ref))
    seqs = [(rev, refs, y_ref, bi) for rev, refs, y_ref in dirs for bi in range(n_batch)]
    order = lambda rev: list(reversed(range(n_chunks))) if rev else list(range(n_chunks))
    sl = lambda rev, k: slice(order(rev)[k] * CHUNK, (order(rev)[k] + 1) * CHUNK)
    blocks = [tuple(ref[bi, sl(rev, k), :] for ref in refs) + (rev,)
              for k in range(n_chunks) for rev, refs, _, bi in seqs]
    terms = _chunk_terms(blocks, msk_ref, tri_ref, hm)
    st = [st_ref[si] for si in range(len(seqs))]
    c_len = CHUNK
    for k in range(n_chunks):
        lhs = [jnp.concatenate(terms[k * len(seqs) + si][0::2], axis=0) for si in range(len(seqs))]
        lp = [_split3(z) for z in lhs]
        sp = [_split3(z) for z in st]
        res = [jnp.dot(jnp.concatenate([l[0], l[1], l[0]], axis=1), jnp.concatenate([s[0], s[0], s[1]], axis=0),
                       preferred_element_type=F32) for l, s in zip(lp, sp)]
        n_l = c_len + GROUP
        for si, (rev, _, y_ref, bi) in enumerate(seqs):
            _, uy, _, o = terms[k * len(seqs) + si]
            y_ref[bi, sl(rev, k), :] = res[si][0:c_len] + uy
            st[si] = res[si][c_len:n_l] + o
    for si in range(len(seqs)):
        st_ref[si] = st[si]


def _scan(feats, n_ctx_tiles):
    lw0, lw1, kd0, kd1, b0, b1, v, kk, r = feats
    nb, s, d = v.shape
    t = TOK_TILE
    nt = s // t
    msk, tri3, hm = _scan_masks()
    fwd = pl.BlockSpec((nb, t, GROUP), lambda q, i: (0, i, q))
    bwd_idx = lambda i: jnp.where(i < n_ctx_tiles, n_ctx_tiles - 1 - i, nt - 1 - (i - n_ctx_tiles))
    bwd = pl.BlockSpec((nb, t, GROUP), lambda q, i: (0, bwd_idx(i), q))
    return pl.pallas_call(
        _scan_kernel,
        grid=(d // GROUP, nt),
        in_specs=[fwd] * 6 + [bwd] * 6 + [pl.BlockSpec(msk.shape, lambda q, i: (0, 0, 0)),
                                          pl.BlockSpec(tri3.shape, lambda q, i: (0, 0, 0)),
                                          pl.BlockSpec(hm.shape, lambda q, i: (0, 0))],
        out_specs=[fwd, bwd],
        out_shape=[jax.ShapeDtypeStruct((nb, s, d), F32)] * 2,
        scratch_shapes=[pltpu.VMEM((2 * nb, GROUP, GROUP), F32)],
        compiler_params=_params("parallel", "arbitrary"),
        name="scan",
    )(lw0, kd0, b0, v, kk, r, lw1, kd1, b1, v, kk, r, msk, tri3, hm)


def _readout_kernel(yf_ref, yb_ref, r_ref, kd0_ref, kd1_ref, v_ref, g_ref, x_ref, mod_ref, vec_ref,
                    wo_ref, hd_ref, hdt_ref, rwt_ref, rb_ref, u_ref,
                    x3_ref, h2_ref, pos_ref, gd_ref, off_ref, *, tiles_per_moe, n_real_tiles):
    i = pl.program_id(0)

    @pl.when(i >= n_real_tiles)
    def _():
        _route_padding(h2_ref, pos_ref, gd_ref)

    @pl.when(i < n_real_tiles)
    def _():
        vec = vec_ref[...]
        mod = mod_ref[0]
        headsum = lambda z: _headsum(z, hd_ref[...], hdt_ref[...])
        inv_k = 1.0 / RWKV_HEAD
        y = yf_ref[0] + yb_ref[0]
        yc = y - headsum(y) * inv_k
        var = headsum(yc * yc) * inv_k
        yn = yc * lax.rsqrt(var + GN_EPS) * vec[0:1] + vec[1:2]
        bonus = headsum(r_ref[0] * (kd0_ref[0] + kd1_ref[0]) * vec[2:3]) * v_ref[0]
        out = (yn + bonus) * g_ref[0]
        x3 = x_ref[...] + mod[2:3] * _dotb(out, wo_ref[...])
        x3_ref[...] = x3
        _route(x3, mod, vec[3:4], rwt_ref, rb_ref, u_ref, off_ref, h2_ref, pos_ref, gd_ref,
               i % tiles_per_moe == 0)


def _readout(yf, yb, r, kd0, kd1, v, g, x2d, mod, vec, wo, hd, hdt, rwt, rb, n_ctx_tiles, moe_tile,
             moe_span, tokens_per_batch):
    ntok, d = x2d.shape
    t = TOK_TILE
    n_e = rwt.shape[0]
    tpb = tokens_per_batch // t
    n_real = ntok // t
    n_pad = _padded_tokens(ntok, moe_span)
    route_specs, route_shapes = _route_outputs(n_pad, d, n_e, moe_tile)
    real = lambda i: jnp.minimum(i, n_real - 1)
    seq = pl.BlockSpec((1, t, d), lambda i: (real(i) // tpb, n_ctx_tiles + real(i) % tpb, 0))
    tokspec = pl.BlockSpec((t, d), lambda i: (real(i), 0))
    full = lambda a: pl.BlockSpec(a.shape, lambda i: (0,) * a.ndim)
    u = _route_consts(t)
    kern = functools.partial(_readout_kernel, tiles_per_moe=moe_tile // t, n_real_tiles=n_real)
    return pl.pallas_call(
        kern,
        grid=(n_pad // t,),
        in_specs=[seq] * 7 + [tokspec, pl.BlockSpec((1, 8, d), lambda i: (real(i) // tpb, 0, 0)),
                              full(vec), full(wo), full(hd), full(hdt), full(rwt), full(rb), full(u)],
        out_specs=[tokspec] + route_specs,
        out_shape=[jax.ShapeDtypeStruct((ntok, d), F32)] + route_shapes,
        scratch_shapes=[pltpu.VMEM((n_e, 128), F32)],
        compiler_params=_params("arbitrary"),
        name="readout",
    )(yf, yb, r, kd0, kd1, v, g, x2d, mod, vec, wo, hd, hdt, rwt, rb, u)


def _final_kernel(x_ref, ft_ref, mod_ref, g_ref, o_ref):
    x = x_ref[...] + mod_ref[0][5:6] * jnp.transpose(ft_ref[...])
    o_ref[...] = _rms(x) * g_ref[0:1]


def _final(x2d, ft, mod, gvec, tokens_per_batch):
    ntok, d = x2d.shape
    t = TOK_TILE
    tpb = tokens_per_batch // t
    return pl.pallas_call(
        _final_kernel,
        grid=(ntok // t,),
        in_specs=[pl.BlockSpec((t, d), lambda i: (i, 0)),
                  pl.BlockSpec((d, t), lambda i: (0, i)),
                  pl.BlockSpec((1, 8, d), lambda i: (i // tpb, 0, 0)),
                  pl.BlockSpec((8, d), lambda i: (0, 0))],
        out_specs=pl.BlockSpec((t, d), lambda i: (i, 0)),
        out_shape=jax.ShapeDtypeStruct((ntok, d), F32),
        compiler_params=_params("parallel"),
        name="final",
    )(x2d, ft, mod, gvec)


def _rows8(rows, d):
    n = -(-len(rows) // 8) * 8
    out = jnp.zeros((n, d), F32)
    return out.at[:len(rows)].set(jnp.stack([jnp.asarray(r, F32) for r in rows]))


def _pad_lora(w, total):
    two, r, d = w.shape
    out = jnp.zeros((two, total, d), w.dtype)
    for dd in range(two):
        out = out.at[dd, dd * r:(dd + 1) * r].set(w[dd])
    return out


def kernel(x, c, ctx, c_ctx, norm_g, ada_w, ada_b, pool_w, pool_ls, rwkv_mu, rwkv_w_rkv, rwkv_w0, rwkv_w1, rwkv_w2, rwkv_a0, rwkv_a1, rwkv_a2, rwkv_g1, rwkv_g2, rwkv_k_k, rwkv_k_a, rwkv_r_k, rwkv_ln_g, rwkv_ln_b, rwkv_w_o, moe_router_w, moe_router_b, moe_w_gu, moe_b_gu, moe_w_down, moe_b_down, final_g):
    nb, seq, d = x.shape
    n_ctx = ctx.shape[1]
    depth = norm_g.shape[0]
    n_e = moe_router_w.shape[2]
    t = TOK_TILE
    assert depth == 2 and nb <= 7 and seq % t == 0 and n_ctx % t == 0 and t % GRID_W == 0 and d % GROUP == 0
    n_lat, n_ctxtok = nb * seq, nb * n_ctx
    moe_tile_lat = min(MOE_TILE, n_lat)
    moe_tile_ctx = min(MOE_TILE, n_ctxtok)
    group_lat = min(MOE_GROUP, -(-n_lat // moe_tile_lat))
    group_ctx = min(MOE_GROUP, -(-n_ctxtok // moe_tile_ctx))
    n_ctx_tiles = n_ctx // t
    seq_total = n_ctx + seq

    cond8 = jnp.zeros((8, d), F32).at[:nb].set(c).at[nb].set(c_ctx)
    ada = _ada(cond8, ada_w, ada_b).reshape(depth, 8, N_MOD, d)
    pad8 = lambda m: jnp.pad(m, ((0, 0), (0, 8 - N_MOD), (0, 0)))
    mod_lat = [pad8(ada[i, :nb]) for i in range(depth)]
    mod_ctx = [pad8(jnp.broadcast_to(ada[i, nb][None], (nb, N_MOD, d))) for i in range(depth)]

    def expert_params(i):
        return (moe_w_gu[i].astype(BF16), moe_b_gu[i].reshape(n_e, 1, -1),
                jnp.swapaxes(moe_w_down[i], 1, 2).astype(BF16), moe_b_down[i].T)

    def router_params(i):
        return moe_router_w[i].T, moe_router_b[i].reshape(n_e, 1)

    x2d = x.reshape(n_lat, d)
    ctx2d = ctx.reshape(n_ctxtok, d)

    vec0 = _rows8([norm_g[0, 0], norm_g[0, 1], pool_ls[0]], d)
    pw = pool_w[0].astype(BF16)
    rwt0, rb0 = router_params(0)
    ex0 = expert_params(0)
    x1, h2, pos, gd = _pool_layer(x2d, mod_lat[0], vec0, pw, rwt0, rb0, GRID_W, moe_tile_lat,
                                  group_lat * moe_tile_lat, seq)
    ft_lat = _moe(h2, pos, gd, *ex0, group_lat)
    c1, h2c, posc, gdc = _pool_layer(ctx2d, mod_ctx[0], vec0, pw, rwt0, rb0, n_ctx, moe_tile_ctx,
                                     group_ctx * moe_tile_ctx, n_ctx)
    ft_ctx = _moe(h2c, posc, gdc, *ex0, group_ctx)

    vec1 = _rows8([norm_g[1, 0]], d)
    x2, h_all = _respre(x1, ft_lat, mod_lat[0], mod_lat[1], vec1, None, seq, n_ctx_tiles, seq_total)
    _, h_all = _respre(c1, ft_ctx, mod_ctx[0], mod_ctx[1], vec1, h_all, n_ctx, 0, seq_total)

    fv = _rows8([rwkv_mu[0, m] for m in range(6)]
                + [rwkv_w0[0, 0], rwkv_w0[0, 1], rwkv_a0[0, 0], rwkv_a0[0, 1], rwkv_k_k[0], rwkv_k_a[0]], d)
    lora = rwkv_w1.shape[3]
    cat2 = lambda w: jnp.concatenate([w[0, 0], w[0, 1]], axis=1).astype(BF16)
    hd, hdt = _head_indicators(d)
    feats = _features(h_all, fv, rwkv_w_rkv[0].astype(BF16),
                      cat2(rwkv_w1), _pad_lora(rwkv_w2[0], 2 * lora).astype(BF16),
                      cat2(rwkv_a1), _pad_lora(rwkv_a2[0], 2 * rwkv_a1.shape[3]).astype(BF16),
                      rwkv_g1[0].astype(BF16), rwkv_g2[0].astype(BF16), hd, hdt, n_ctx_tiles)
    lw0, lw1, kd0, kd1, b0, b1, v, kk, r, g = feats
    yf, yb = _scan((lw0, lw1, kd0, kd1, b0, b1, v, kk, r), n_ctx_tiles)

    vec_ro = _rows8([rwkv_ln_g[0], rwkv_ln_b[0], rwkv_r_k[0].reshape(-1), norm_g[1, 1]], d)
    rwt1, rb1 = router_params(1)
    x3, h2, pos, gd = _readout(yf, yb, r, kd0, kd1, v, g, x2, mod_lat[1], vec_ro,
                               rwkv_w_o[0].astype(BF16), hd, hdt, rwt1, rb1, n_ctx_tiles, moe_tile_lat,
                               group_lat * moe_tile_lat, seq)
    ft = _moe(h2, pos, gd, *expert_params(1), group_lat)
    out = _final(x3, ft, mod_lat[1], _rows8([final_g], d), seq)
    return out.reshape(nb, seq, d)
```

```python
import functools

import numpy as np
import jax
import jax.numpy as jnp
from jax import lax
from jax.experimental import pallas as pl
from jax.experimental.pallas import tpu as pltpu

F32 = jnp.float32
BF16 = jnp.bfloat16
NN = (((1,), (0,)), ((), ()))
NT = (((1,), (1,)), ((), ()))
TN = (((0,), (0,)), ((), ()))

N_MOD = 6
NORM_EPS = 1e-6
GRID_W = 64
POOL_WINDOWS = (2, 4, 8, 16)
RWKV_HEAD = 64
GN_EPS = 64e-5
TOP_K = 4
SWIGLU_ALPHA = 1.702
SWIGLU_LIMIT = 7.0

TOK_TILE = 256
MOE_TILE = 1792
MOE_GROUP = 4
MOE_ROWS = 256
MOE_CHUNK_BLOCKS = 4
CHUNK = 64
GROUP_HEADS = 2
GROUP = GROUP_HEADS * RWKV_HEAD
HEAD_PAD = 128
VMEM_LIMIT_BYTES = 56 * 1024 * 1024


def _dotx(a, b, dims=NN):
    return lax.dot_general(a, b, dims, precision=lax.Precision.HIGHEST, preferred_element_type=F32)


def _dotb(a, b, dims=NN):
    return lax.dot_general(a.astype(BF16), b.astype(BF16), dims, preferred_element_type=F32)


def _params(*sem):
    return pltpu.CompilerParams(dimension_semantics=sem, vmem_limit_bytes=VMEM_LIMIT_BYTES)


def _padded_tokens(ntok, tile):
    return -(-ntok // tile) * tile


def _rms(x):
    return x * lax.rsqrt(jnp.mean(x * x, axis=-1, keepdims=True) + NORM_EPS)


def _sigmoid(x):
    return 1.0 / (1.0 + jnp.exp(-x))


def _ada_kernel(c_ref, w_ref, b_ref, o_ref):
    c = c_ref[...]
    o_ref[0] = _dotx(c * _sigmoid(c), w_ref[0]) + b_ref[0]


def _ada(cond8, ada_w, ada_b):
    depth, d, nd = ada_w.shape
    return pl.pallas_call(
        _ada_kernel,
        grid=(depth, nd // d),
        in_specs=[pl.BlockSpec((8, d), lambda i, n: (0, 0)),
                  pl.BlockSpec((1, d, d), lambda i, n: (i, 0, n)),
                  pl.BlockSpec((1, 1, d), lambda i, n: (i, 0, n))],
        out_specs=pl.BlockSpec((1, 8, d), lambda i, n: (i, 0, n)),
        out_shape=jax.ShapeDtypeStruct((depth, 8, nd), F32),
        compiler_params=_params("parallel", "parallel"),
        name="ada",
    )(cond8, ada_w, ada_b.reshape(depth, 1, nd))


def _route_padding(h2_ref, pos_ref, gd_ref):
    h2_ref[...] = jnp.zeros_like(h2_ref)
    pos_ref[...] = jnp.full(pos_ref.shape, -1, jnp.int32)
    gd_ref[...] = jnp.zeros_like(gd_ref)


def _route(x_new, mod, g2, rwt_ref, rb_ref, u_ref, off_ref, h2_ref, pos_ref, gd_ref, reset):
    h2 = _rms(x_new) * g2 * (1.0 + mod[4:5]) + mod[3:4]
    h2_ref[...] = h2.astype(BF16)
    logits = _dotx(rwt_ref[...], h2, NT) + rb_ref[...]
    n_e = logits.shape[0]
    eio = lax.broadcasted_iota(jnp.int32, logits.shape, 0).astype(F32)
    live = logits
    sels, vals = [], []
    for _ in range(TOP_K):
        m = jnp.max(live, axis=0, keepdims=True)
        idx = jnp.min(jnp.where(live == m, eio, float(n_e)), axis=0, keepdims=True)
        sel = eio == idx
        sels.append(sel)
        vals.append(m)
        live = jnp.where(sel, -jnp.inf, live)
    exps = [jnp.exp(v - vals[0]) for v in vals]
    inv = 1.0 / (exps[0] + exps[1] + exps[2] + exps[3])
    gd = jnp.zeros_like(logits)
    maskf = jnp.zeros_like(logits)
    for k in range(TOP_K):
        gd = jnp.where(sels[k], exps[k] * inv, gd)
        maskf = jnp.where(sels[k], 1.0, maskf)

    @pl.when(reset)
    def _():
        off_ref[...] = jnp.zeros_like(off_ref)

    incl = _dotb(maskf, u_ref[...])
    off = off_ref[:, 0:1]
    pos_ref[0] = jnp.where(maskf > 0.5, off + incl - 1.0, -1.0).astype(jnp.int32)
    gd_ref[0] = gd
    off_ref[...] = off_ref[...] + jnp.sum(maskf, axis=1, keepdims=True)


def _route_outputs(n_pad, d, n_e, moe_tile):
    t = TOK_TILE
    tpm = moe_tile // t
    nt = n_pad // moe_tile
    spec = pl.BlockSpec((1, n_e, t), lambda i: (i // tpm, 0, i % tpm))
    shapes = [jax.ShapeDtypeStruct((n_pad, d), BF16),
              jax.ShapeDtypeStruct((nt, n_e, moe_tile), jnp.int32),
              jax.ShapeDtypeStruct((nt, n_e, moe_tile), F32)]
    return [pl.BlockSpec((t, d), lambda i: (i, 0)), spec, spec], shapes


def _route_consts(tile):
    s = np.arange(tile)
    return jnp.asarray((s[:, None] <= s[None, :]).astype(np.float32), BF16)


def _pool_kernel(x_ref, mod_ref, vec_ref, pm_ref, pw_ref, rwt_ref, rb_ref, u_ref,
                 x1_ref, h2_ref, pos_ref, gd_ref, off_ref, *, tiles_per_moe, n_real_tiles):
    i = pl.program_id(0)

    @pl.when(i >= n_real_tiles)
    def _():
        _route_padding(h2_ref, pos_ref, gd_ref)

    @pl.when(i < n_real_tiles)
    def _():
        x = x_ref[...]
        mod = mod_ref[0]
        vec = vec_ref[...]
        h = _rms(x) * vec[0:1] * (1.0 + mod[1:2]) + mod[0:1]
        gw = pw_ref.shape[1]
        ys = []
        for g in range(len(POOL_WINDOWS)):
            d = _dot3(pm_ref[g], h[:, g * gw:(g + 1) * gw])
            ys.append(_dotb(d, pw_ref[g]))
        y = jnp.concatenate(ys, axis=1) * vec[2:3]
        x1 = x + mod[2:3] * y
        x1_ref[...] = x1
        _route(x1, mod, vec[1:2], rwt_ref, rb_ref, u_ref, off_ref, h2_ref, pos_ref, gd_ref,
               i % tiles_per_moe == 0)


def _pool_matrices(tile, row_len):
    p = np.arange(tile)
    pp = p % row_len
    out = []
    for win in POOL_WINDOWS:
        lo = np.clip(pp - win // 2, 0, row_len - 1)
        hi = np.clip(pp + win // 2 - 1, 0, row_len - 1)
        cnt = (hi - lo + 1).astype(np.float64)
        same = (p[:, None] // row_len) == (p[None, :] // row_len)
        inwin = same & (pp[None, :] >= lo[:, None]) & (pp[None, :] <= hi[:, None])
        out.append(inwin / cnt[:, None] - np.eye(tile))
    return jnp.asarray(np.stack(out).astype(np.float32))


def _pool_layer(x2d, mod, vec, pool_w, rwt, rb, row_len, moe_tile, moe_span, tokens_per_batch):
    ntok, d = x2d.shape
    t = TOK_TILE
    n_e = rwt.shape[0]
    tiles_per_batch = tokens_per_batch // t
    n_real = ntok // t
    n_pad = _padded_tokens(ntok, moe_span)
    route_specs, route_shapes = _route_outputs(n_pad, d, n_e, moe_tile)
    pm = _pool_matrices(t, row_len)
    kern = functools.partial(_pool_kernel, tiles_per_moe=moe_tile // t, n_real_tiles=n_real)
    const2 = lambda i: (0, 0)
    const3 = lambda i: (0, 0, 0)
    real = lambda i: jnp.minimum(i, n_real - 1)
    return pl.pallas_call(
        kern,
        grid=(n_pad // t,),
        in_specs=[pl.BlockSpec((t, d), lambda i: (real(i), 0)),
                  pl.BlockSpec((1, 8, d), lambda i: (real(i) // tiles_per_batch, 0, 0)),
                  pl.BlockSpec((8, d), const2),
                  pl.BlockSpec(pm.shape, const3),
                  pl.BlockSpec(pool_w.shape, const3),
                  pl.BlockSpec(rwt.shape, const2),
                  pl.BlockSpec(rb.shape, const2),
                  pl.BlockSpec((t, t), const2)],
        out_specs=[pl.BlockSpec((t, d), lambda i: (real(i), 0))] + route_specs,
        out_shape=[jax.ShapeDtypeStruct((ntok, d), F32)] + route_shapes,
        scratch_shapes=[pltpu.VMEM((n_e, 128), F32)],
        compiler_params=_params("arbitrary"),
        name="pool",
    )(x2d, mod, vec, pm, pool_w, rwt, rb, _route_consts(t))


def _cast_kernel(w_ref, o_ref, *, transpose):
    w = w_ref[0, 0].astype(BF16)
    o_ref[0, 0] = jnp.transpose(w) if transpose else w


def _expert_weights_bf16(w, transpose):
    n_l, n_e, r, c = w.shape
    blk = min(r, c)
    out_idx = (lambda l, e, i, j: (l, e, j, i)) if transpose else (lambda l, e, i, j: (l, e, i, j))
    return pl.pallas_call(
        functools.partial(_cast_kernel, transpose=transpose),
        grid=(n_l, n_e, r // blk, c // blk),
        in_specs=[pl.BlockSpec((1, 1, blk, blk), lambda l, e, i, j: (l, e, i, j))],
        out_specs=pl.BlockSpec((1, 1, blk, blk), out_idx),
        out_shape=jax.ShapeDtypeStruct((n_l, n_e, c, r) if transpose else w.shape, BF16),
        compiler_params=_params("parallel", "parallel", "parallel", "parallel"),
        name="wprep",
    )(w)


def _one_hot_rows(prow, r0, n_rows):
    rows = lax.broadcasted_iota(jnp.int32, (n_rows, prow.shape[1]), 0) + r0
    return jnp.where(prow == rows, 1.0, 0.0).astype(BF16)


def _expert_kernel(grp_ref, sub_ref, e_ref, r0_ref, half_ref,
                   pos_ref, gd_ref, x_ref, wgu_ref, bgu_ref, wdt_ref, o_ref):
    b = pl.program_id(0)
    f = wdt_ref.shape[3]
    n_rows = o_ref.shape[2]
    ts = pos_ref.shape[2]
    e = e_ref[b]
    sub = sub_ref[b]
    prow = pos_ref[sub, pl.ds(e, 1), :]
    grow = gd_ref[sub, pl.ds(e, 1), :]
    g_hi = grow.astype(BF16)
    g_mid = (grow - g_hi.astype(F32)).astype(BF16)
    g_lo = (grow - g_hi.astype(F32) - g_mid.astype(F32)).astype(BF16)
    g3 = jnp.concatenate([g_hi, g_mid, g_lo, jnp.zeros((5, ts), BF16)], axis=0)

    def rows_out(n):
        oh = _one_hot_rows(prow, r0_ref[b], n)
        x = x_ref[pl.ds(pl.multiple_of(sub * ts, ts), ts), :]
        xg = jnp.dot(oh, x, preferred_element_type=F32).astype(BF16)
        gu = jnp.dot(xg, wgu_ref[0, 0], preferred_element_type=F32) + bgu_ref[0]
        gate = jnp.minimum(gu[:, :f], SWIGLU_LIMIT)
        up = jnp.clip(gu[:, f:], -SWIGLU_LIMIT, SWIGLU_LIMIT)
        act = (up + 1.0) * (gate * _sigmoid(SWIGLU_ALPHA * gate))
        yt = lax.dot_general(wdt_ref[0, 0], act.astype(BF16), NT, preferred_element_type=F32)
        g8 = lax.dot_general(g3, oh, NT, preferred_element_type=F32)
        return (yt * (g8[0:1] + g8[1:2] + g8[2:3])).astype(BF16)

    @pl.when(half_ref[b] == 0)
    def _():
        o_ref[0] = rows_out(n_rows)

    @pl.when(half_ref[b] == 1)
    def _():
        o_ref[0, :, 0:n_rows // 2] = rows_out(n_rows // 2)
        o_ref[0, :, n_rows // 2:n_rows] = jnp.zeros((o_ref.shape[1], n_rows // 2), BF16)


def _combine_kernel(tile_ref, first_ref, blk_ref, e_ref, r0_ref, pos_ref, gd_ref, *rest):
    yt_refs, bdt_ref, o_ref = rest[:MOE_CHUNK_BLOCKS], rest[-2], rest[-1]
    c = pl.program_id(0)
    n_rows = yt_refs[0].shape[2]

    @pl.when(first_ref[c] == 1)
    def _():
        o_ref[...] = _dot3(bdt_ref[...], gd_ref[0])

    slot = lambda q: c * MOE_CHUNK_BLOCKS + q
    oh = jnp.concatenate([_one_hot_rows(pos_ref[0, pl.ds(e_ref[slot(q)], 1), :], r0_ref[slot(q)], n_rows)
                          for q in range(MOE_CHUNK_BLOCKS)], axis=0)
    yt = jnp.concatenate([ref[0] for ref in yt_refs], axis=1)
    o_ref[...] += jnp.dot(yt, oh, preferred_element_type=F32)


def _moe_tables(pos, group):
    nt, n_e, ts = pos.shape
    ng = nt // group
    rb, cb = MOE_ROWS, MOE_CHUNK_BLOCKS
    i32 = lambda a: a.astype(jnp.int32)
    cnt = jnp.sum(i32(pos >= 0), axis=-1)
    nb = (cnt + rb - 1) // rb
    max_tile_blocks = ts * TOP_K // rb + n_e
    order = lambda a: a.reshape(ng, group, n_e).transpose(0, 2, 1).reshape(-1)
    flat = order(nb)
    cum = jnp.cumsum(flat)
    start = cum - flat
    n_blocks = cum[-1]
    b = jnp.minimum(jnp.arange(nt * max_tile_blocks, dtype=jnp.int32), n_blocks - 1)
    count_le = lambda sorted_, q: jnp.sum(i32(sorted_[None, :] <= q[:, None]), axis=1)
    idx = jnp.minimum(count_le(cum, b), flat.shape[0] - 1)
    r0 = (b - start[idx]) * rb
    expert_tables = (i32(idx // (n_e * group)), i32(idx % group), i32((idx // group) % n_e), i32(r0),
                     i32(order(cnt)[idx] - r0 <= rb // 2))
    ce = jnp.cumsum(nb, axis=1)
    nch = (ce[:, -1] + cb - 1) // cb
    cch = jnp.cumsum(nch)
    n_chunks = cch[-1]
    max_chunks = nt * (-(-max_tile_blocks // cb))
    c = jnp.minimum(jnp.arange(max_chunks, dtype=jnp.int32), n_chunks - 1)
    tile = jnp.minimum(count_le(cch, c), nt - 1)
    lc = c - (cch - nch)[tile]
    slot = (lc[:, None] * cb + jnp.arange(cb, dtype=jnp.int32)[None, :]).reshape(-1)
    tile_s = jnp.repeat(tile, cb)
    used = slot < ce[tile_s, -1]
    slot = jnp.where(used, slot, 0)
    ce_s = ce[tile_s]
    e_s = jnp.minimum(jnp.sum(i32(slot[:, None] >= ce_s), axis=1), n_e - 1)
    local = slot - jnp.take_along_axis(ce_s - nb[tile_s], e_s[:, None], axis=1)[:, 0]
    blk = start.reshape(ng, n_e, group)[tile_s // group, e_s, tile_s % group] + local
    combine_tables = (i32(tile), i32(lc == 0), i32(blk), i32(e_s), i32(jnp.where(used, local * rb, 1 << 20)))
    return expert_tables, i32(n_blocks), combine_tables, i32(n_chunks)


def _moe(h2, pos, gd, layer, wgu, bgu, wdt, bdt, group):
    ntok, d = h2.shape
    nt, n_e, ts = pos.shape
    f2 = wgu.shape[3]
    rb, cb = MOE_ROWS, MOE_CHUNK_BLOCKS
    expert_tables, n_blocks, combine_tables, n_chunks = _moe_tables(pos, group)
    max_blocks = expert_tables[0].shape[0]
    yt = pl.pallas_call(
        _expert_kernel,
        grid_spec=pltpu.PrefetchScalarGridSpec(
            num_scalar_prefetch=5,
            grid=(n_blocks,),
            in_specs=[pl.BlockSpec((group, n_e, ts), lambda b, gp, sb, ee, rr, hf: (gp[b], 0, 0)),
                      pl.BlockSpec((group, n_e, ts), lambda b, gp, sb, ee, rr, hf: (gp[b], 0, 0)),
                      pl.BlockSpec((group * ts, d), lambda b, gp, sb, ee, rr, hf: (gp[b], 0),
                                   pipeline_mode=pl.Buffered(1)),
                      pl.BlockSpec((1, 1, d, f2), lambda b, gp, sb, ee, rr, hf: (layer, ee[b], 0, 0)),
                      pl.BlockSpec((1, 1, f2), lambda b, gp, sb, ee, rr, hf: (ee[b], 0, 0)),
                      pl.BlockSpec((1, 1, d, f2 // 2), lambda b, gp, sb, ee, rr, hf: (layer, ee[b], 0, 0))],
            out_specs=pl.BlockSpec((1, d, rb), lambda b, gp, sb, ee, rr, hf: (b, 0, 0)),
        ),
        out_shape=jax.ShapeDtypeStruct((max_blocks, d, rb), BF16),
        compiler_params=_params("arbitrary"),
        name="expert",
    )(*expert_tables, pos, gd, h2, wgu, bgu, wdt)
    yt_spec = lambda q: pl.BlockSpec((1, d, rb), lambda c, tl, fs, bk, ee, rr: (bk[c * cb + q], 0, 0))
    return pl.pallas_call(
        _combine_kernel,
        grid_spec=pltpu.PrefetchScalarGridSpec(
            num_scalar_prefetch=5,
            grid=(n_chunks,),
            in_specs=[pl.BlockSpec((1, n_e, ts), lambda c, tl, fs, bk, ee, rr: (tl[c], 0, 0)),
                      pl.BlockSpec((1, n_e, ts), lambda c, tl, fs, bk, ee, rr: (tl[c], 0, 0))]
                     + [yt_spec(q) for q in range(cb)]
                     + [pl.BlockSpec((d, n_e), lambda c, tl, fs, bk, ee, rr: (0, 0))],
            out_specs=pl.BlockSpec((d, ts), lambda c, tl, fs, bk, ee, rr: (0, tl[c]),
                                   pipeline_mode=pl.Buffered(1)),
        ),
        out_shape=jax.ShapeDtypeStruct((d, ntok), F32),
        compiler_params=_params("arbitrary"),
        name="combine",
    )(*combine_tables, pos, gd, *([yt] * cb), bdt)


def _respre_kernel(x_ref, ft_ref, modp_ref, modc_ref, vec_ref, *rest):
    x2_ref, h_ref = rest[-2], rest[-1]
    x2 = x_ref[...] + modp_ref[0][5:6] * jnp.transpose(ft_ref[...])
    x2_ref[...] = x2
    modc = modc_ref[0]
    h_ref[0] = _rms(x2) * vec_ref[0:1] * (1.0 + modc[1:2]) + modc[0:1]


def _respre(x2d, ft, modp, modc, vec, h_all, tokens_per_batch, tile_offset, seq_total):
    ntok, d = x2d.shape
    t = TOK_TILE
    nb = modp.shape[0]
    tpb = tokens_per_batch // t
    in_specs = [pl.BlockSpec((t, d), lambda i: (i, 0)),
                pl.BlockSpec((d, t), lambda i: (0, i)),
                pl.BlockSpec((1, 8, d), lambda i: (i // tpb, 0, 0)),
                pl.BlockSpec((1, 8, d), lambda i: (i // tpb, 0, 0)),
                pl.BlockSpec((8, d), lambda i: (0, 0))]
    args = [x2d, ft, modp, modc, vec]
    aliases = {}
    if h_all is not None:
        in_specs.append(pl.BlockSpec(memory_space=pl.ANY))
        args.append(h_all)
        aliases = {5: 1}
    return pl.pallas_call(
        _respre_kernel,
        grid=(ntok // t,),
        in_specs=in_specs,
        out_specs=[pl.BlockSpec((t, d), lambda i: (i, 0)),
                   pl.BlockSpec((1, t, d), lambda i: (i // tpb, tile_offset + i % tpb, 0))],
        out_shape=[jax.ShapeDtypeStruct((ntok, d), F32),
                   jax.ShapeDtypeStruct((nb, seq_total, d), F32)],
        input_output_aliases=aliases,
        compiler_params=_params("parallel"),
        name="respre",
    )(*args)


def _feat_kernel(h_ref, hp_ref, hn_ref, fv_ref, wrkv_ref, w1_ref, w2_ref, a1_ref, a2_ref, g1_ref, g2_ref,
                 hd_ref, hdt_ref,
                 lw0_ref, lw1_ref, kd0_ref, kd1_ref, b0_ref, b1_ref, v_ref, kk_ref, r_ref, g_ref,
                 *, n_ctx_tiles, n_tiles):
    i = pl.program_id(1)
    h = h_ref[0]
    t = h.shape[0]
    fv = fv_ref[...]
    first = jnp.logical_or(i == 0, i == n_ctx_tiles)
    last = jnp.logical_or(i == n_ctx_tiles - 1, i == n_tiles - 1)
    prow = jnp.where(first, 0.0, hp_ref[0][7:8])
    nrow = jnp.where(last, 0.0, hn_ref[0][0:1])
    rio = lax.broadcasted_iota(jnp.int32, h.shape, 0)
    hdn = jnp.where(rio == 0, prow, pltpu.roll(h, 1, 0))
    hup = jnp.where(rio == t - 1, nrow, pltpu.roll(h, t - 1, 0))
    xx = 0.5 * (hdn + hup) - h

    mix = lambda m: h + xx * fv[m:m + 1]
    r = _dotb(mix(0), wrkv_ref[0])
    k = _dotb(mix(2), wrkv_ref[1])
    v = _dotb(mix(3), wrkv_ref[2])
    tw = jnp.tanh(_dotb(mix(1), w1_ref[...]))
    la = _dotb(mix(4), a1_ref[...])
    g = _dotb(_sigmoid(_dotb(mix(5), g1_ref[...])), g2_ref[...])

    kkraw = k * fv[10:11]
    ss = _headsum(kkraw * kkraw, hd_ref[...], hdt_ref[...])
    kk = kkraw / jnp.maximum(jnp.sqrt(ss), 1e-12)
    k_a = fv[11:12]
    decay_scale = float(np.exp(-0.5))
    for d, (lw_ref, kd_ref, b_ref) in enumerate(((lw0_ref, kd0_ref, b0_ref), (lw1_ref, kd1_ref, b1_ref))):
        zw = fv[6 + d:7 + d] + _dotb(tw, w2_ref[d])
        lw_ref[0] = -decay_scale * _sigmoid(zw)
        icl = _sigmoid(fv[8 + d:9 + d] + _dotb(la, a2_ref[d]))
        kd_ref[0] = k * (1.0 + (icl - 1.0) * k_a)
        b_ref[0] = kk * icl
    v_ref[0] = v
    kk_ref[0] = kk
    r_ref[0] = r
    g_ref[0] = g


def _head_indicators(d):
    hd = np.zeros((d, HEAD_PAD), np.float32)
    hd[np.arange(d), np.arange(d) // RWKV_HEAD] = 1.0
    return jnp.asarray(hd), jnp.asarray(hd.T.copy())


def _features(h_all, fv, wrkv, w1c, w2p, a1c, a2p, g1, g2, hd, hdt, n_ctx_tiles):
    nb, s, d = h_all.shape
    t = TOK_TILE
    nt = s // t
    r8 = t // 8
    kern = functools.partial(_feat_kernel, n_ctx_tiles=n_ctx_tiles, n_tiles=nt)
    full = lambda a: pl.BlockSpec(a.shape, lambda b, i: (0,) * a.ndim)
    tok = pl.BlockSpec((1, t, d), lambda b, i: (b, i, 0))
    return pl.pallas_call(
        kern,
        grid=(nb, nt),
        in_specs=[tok,
                  pl.BlockSpec((1, 8, d), lambda b, i: (b, jnp.maximum(i * r8 - 1, 0), 0)),
                  pl.BlockSpec((1, 8, d), lambda b, i: (b, jnp.minimum((i + 1) * r8, s // 8 - 1), 0)),
                  full(fv), full(wrkv), full(w1c), full(w2p), full(a1c), full(a2p), full(g1), full(g2),
                  full(hd), full(hdt)],
        out_specs=[tok] * 10,
        out_shape=[jax.ShapeDtypeStruct((nb, s, d), F32)] * 10,
        compiler_params=_params("parallel", "parallel"),
        name="feat",
    )(h_all, h_all, h_all, fv, wrkv, w1c, w2p, a1c, a2p, g1, g2, hd, hdt)


def _scan_masks():
    n = GROUP_HEADS * CHUNK
    rr = np.arange(n)[:, None]
    cc = np.arange(n)[None, :]
    bd = (rr // CHUNK) == (cc // CHUNK)
    r_, c_ = rr % CHUNK, cc % CHUNK
    out = []
    for rev in (False, True):
        strict = bd & ((c_ > r_) if rev else (c_ < r_))
        incl = bd & ((c_ >= r_) if rev else (c_ <= r_))
        d16 = strict & ((r_ // 16) == (c_ // 16))
        l1 = strict & ((r_ // 32) == (c_ // 32)) & ((r_ // 16) != (c_ // 16))
        l2 = strict & ((r_ // 32) != (c_ // 32))
        out.append([strict, incl, d16, l1, l2])
    masks = [out[0][m] for m in range(5)] + [out[1][m] for m in range(5)] + [bd, rr == cc]
    hm = np.zeros((8, GROUP), np.float32)
    for j in range(GROUP_HEADS):
        hm[j, j * RWKV_HEAD:(j + 1) * RWKV_HEAD] = 1.0
    tri3 = np.stack([np.tile(out[rev][1][0:CHUNK, 0:CHUNK], (1, 3)) for rev in (0, 1)]).astype(np.float32)
    return jnp.asarray(np.stack(masks).astype(np.float32)), jnp.asarray(tri3, BF16), jnp.asarray(hm)


def _split3(a):
    a1 = a.astype(BF16)
    r1 = a - a1.astype(F32)
    a2 = r1.astype(BF16)
    return a1, a2, (r1 - a2.astype(F32)).astype(BF16)


def _dot3(a, b, dims=NN):
    a1, a2, _ = _split3(a)
    b1, b2, _ = _split3(b)
    mm = lambda p, q: lax.dot_general(p, q, dims, preferred_element_type=F32)
    return mm(a1, b1) + (mm(a1, b2) + mm(a2, b1))


def _dot01(a, ind):
    ib = ind.astype(BF16)
    a1, a2, _ = _split3(a)
    mm = lambda p: jnp.dot(p, ib, preferred_element_type=F32)
    return mm(a1) + mm(a2)


def _headsum(z, hd, hdt):
    return _dot01(_dot01(z, hd), hdt)


def _chunk_terms(blocks, msk_ref, tri_ref, hm):
    c_len = blocks[0][0].shape[0]
    n4 = GROUP_HEADS * c_len
    revs = [blk[6] for blk in blocks]
    mask = lambda k: [msk_ref[(5 if rev else 0) + k] for rev in revs]
    bd, eye = msk_ref[10], msk_ref[11]
    each = lambda f, *cols: [f(*args) for args in zip(*cols)]
    stack4 = lambda z: jnp.concatenate([z * hm[j:j + 1] for j in range(GROUP_HEADS)], axis=0)
    tile4 = lambda z: jnp.concatenate([z] * GROUP_HEADS, axis=0)
    fold4 = lambda z: sum(z[j * c_len:(j + 1) * c_len] for j in range(GROUP_HEADS))
    lw, kd, b, v, kk, r = [[blk[k] for blk in blocks] for k in range(6)]

    tri3 = [tri_ref[1 if rev else 0] for rev in revs]
    parts = each(lambda z: jnp.concatenate(_split3(z), axis=0), lw)
    c = each(lambda t, p: jnp.dot(t, p, preferred_element_type=F32), tri3, parts)
    ctot = each(lambda c_, rev: c_[0:1] if rev else c_[c_len - 1:c_len], c, revs)
    en = each(lambda c_: jnp.exp(-c_), c)
    khat = each(lambda kk_, c_, lw_: kk_ * jnp.exp(c_ - lw_), kk, c, lw)
    rhat = each(lambda r_, c_: r_ * jnp.exp(c_), r, c)
    bch = each(jnp.multiply, b, en)
    kch = each(jnp.multiply, kd, en)
    et = each(jnp.exp, ctot)
    btil = each(jnp.multiply, bch, et)
    ktil = each(jnp.multiply, kch, et)
    kh4, rh4, v4 = each(stack4, khat), each(stack4, rhat), each(stack4, v)
    sc = each(lambda k_, r_, b_, c_: _dotb(jnp.concatenate([k_, r_], axis=0),
                                           jnp.concatenate([tile4(b_), tile4(c_)], axis=0), NT),
              kh4, rh4, bch, kch)
    a_bk = each(lambda s, m_: s[0:n4, 0:n4] * m_, sc, mask(0))
    a_kk = each(lambda s, m_: s[0:n4, n4:2 * n4] * m_, sc, mask(0))
    b_br = each(lambda s, m_: s[n4:2 * n4, 0:n4] * m_, sc, mask(1))
    b_kr = each(lambda s, m_: s[n4:2 * n4, n4:2 * n4] * m_, sc, mask(1))
    avp = each(lambda a_, b_, v_: _dotb(jnp.concatenate([a_, b_], axis=0), v_), a_kk, b_kr, v4)
    av4 = [z[0:n4] for z in avp]
    p2 = [z[n4:2 * n4] for z in avp]
    ad = each(jnp.multiply, a_bk, mask(2))
    a2 = each(_dotb, ad, ad)
    times_and_square = lambda x_, p_: _dotb(jnp.concatenate([x_, p_], axis=0), p_)
    x = [eye - ad_ for ad_ in ad]
    both = each(times_and_square, x, a2)
    x = each(lambda x_, z: x_ + z[0:n4], x, both)
    a4 = [z[n4:2 * n4] for z in both]
    both = each(times_and_square, x, a4)
    x = each(lambda x_, z: x_ + z[0:n4], x, both)
    a8 = [z[n4:2 * n4] for z in both]
    dinv = each(lambda x_, a8_: x_ + _dotb(x_, a8_), x, a8)
    t1 = each(lambda d_, a_, m_: _dotb(d_, a_ * m_), dinv, a_bk, mask(3))
    b1 = each(lambda d_, t_: d_ - _dotb(t_, d_), dinv, t1)
    t2 = each(lambda b_, a_, m_: _dotb(b_, a_ * m_), b1, a_bk, mask(4))
    tinv = each(lambda b_, t_: b_ - _dotb(t_, b_), b1, t2)
    wu = each(lambda t_, k_, a_: _dotb(t_, jnp.concatenate([k_, a_], axis=1)), tinv, kh4, av4)
    p1 = each(_dotb, b_br, wu)
    q = GROUP
    wy = each(lambda r_, p_: fold4(r_ - p_[:, 0:q]), rh4, p1)
    uy = each(lambda p2_, p_: fold4(p2_ - p_[:, q:2 * q]), p2, p1)
    w1 = each(lambda w_: fold4(w_[:, 0:q]), wu)
    u = each(lambda w_: fold4(w_[:, q:2 * q]), wu)
    om = each(lambda k_, b_, v_, u_, w_: _dotb(
        jnp.concatenate([k_, b_], axis=0),
        jnp.concatenate([jnp.concatenate([v_, jnp.zeros_like(v_)], axis=1),
                         jnp.concatenate([-u_, -w_], axis=1)], axis=0), TN), ktil, btil, v, u, w1)
    o = [bd * z[:, 0:q] for z in om]
    m = each(lambda e_, z: eye * e_ + bd * z[:, q:2 * q], et, om)
    return list(zip(wy, uy, m, o))


def _scan_kernel(lwf, kdf, bf, vf, kkf, rf, lwb, kdb, bb, vb, kkb, rb, msk_ref, tri_ref, hm_ref,
                 yf_ref, yb_ref, st_ref):
    @pl.when(pl.program_id(1) == 0)
    def _():
        st_ref[...] = jnp.zeros_like(st_ref)

    hm = hm_ref[...]
    n_batch = lwf.shape[0]
    n_chunks = lwf.shape[1] // CHUNK
    dirs = ((False, (lwf, kdf, bf, vf, kkf, rf), yf_ref), (True, (lwb, kdb, bb, vb, kkb, rb), yb_ref))
    seqs = [(rev, refs, y_ref, bi) for rev, refs, y_ref in dirs for bi in range(n_batch)]
    order = lambda rev: list(reversed(range(n_chunks))) if rev else list(range(n_chunks))
    sl = lambda rev, k: slice(order(rev)[k] * CHUNK, (order(rev)[k] + 1) * CHUNK)
    blocks = [tuple(ref[bi, sl(rev, k), :] for ref in refs) + (rev,)
              for k in range(n_chunks) for rev, refs, _, bi in seqs]
    terms = _chunk_terms(blocks, msk_ref, tri_ref, hm)
    st = [st_ref[si] for si in range(len(seqs))]
    c_len = CHUNK
    for k in range(n_chunks):
        lhs = [jnp.concatenate(terms[k * len(seqs) + si][0::2], axis=0) for si in range(len(seqs))]
        lp = [_split3(z) for z in lhs]
        sp = [_split3(z) for z in st]
        res = [jnp.dot(jnp.concatenate([l[0], l[1], l[0]], axis=1), jnp.concatenate([s[0], s[0], s[1]], axis=0),
                       preferred_element_type=F32) for l, s in zip(lp, sp)]
        n_l = c_len + GROUP
        for si, (rev, _, y_ref, bi) in enumerate(seqs):
            _, uy, _, o = terms[k * len(seqs) + si]
            y_ref[bi, sl(rev, k), :] = res[si][0:c_len] + uy
            st[si] = res[si][c_len:n_l] + o
    for si in range(len(seqs)):
        st_ref[si] = st[si]


def _scan(feats, n_ctx_tiles):
    lw0, lw1, kd0, kd1, b0, b1, v, kk, r = feats
    nb, s, d = v.shape
    t = TOK_TILE
    nt = s // t
    msk, tri3, hm = _scan_masks()
    fwd = pl.BlockSpec((nb, t, GROUP), lambda q, i: (0, i, q))
    bwd_idx = lambda i: jnp.where(i < n_ctx_tiles, n_ctx_tiles - 1 - i, nt - 1 - (i - n_ctx_tiles))
    bwd = pl.BlockSpec((nb, t, GROUP), lambda q, i: (0, bwd_idx(i), q))
    return pl.pallas_call(
        _scan_kernel,
        grid=(d // GROUP, nt),
        in_specs=[fwd] * 6 + [bwd] * 6 + [pl.BlockSpec(msk.shape, lambda q, i: (0, 0, 0)),
                                          pl.BlockSpec(tri3.shape, lambda q, i: (0, 0, 0)),
                                          pl.BlockSpec(hm.shape, lambda q, i: (0, 0))],
        out_specs=[fwd, bwd],
        out_shape=[jax.ShapeDtypeStruct((nb, s, d), F32)] * 2,
        scratch_shapes=[pltpu.VMEM((2 * nb, GROUP, GROUP), F32)],
        compiler_params=_params("parallel", "arbitrary"),
        name="scan",
    )(lw0, kd0, b0, v, kk, r, lw1, kd1, b1, v, kk, r, msk, tri3, hm)


def _readout_kernel(yf_ref, yb_ref, r_ref, kd0_ref, kd1_ref, v_ref, g_ref, x_ref, mod_ref, vec_ref,
                    wo_ref, hd_ref, hdt_ref, rwt_ref, rb_ref, u_ref,
                    x3_ref, h2_ref, pos_ref, gd_ref, off_ref, *, tiles_per_moe, n_real_tiles):
    i = pl.program_id(0)

    @pl.when(i >= n_real_tiles)
    def _():
        _route_padding(h2_ref, pos_ref, gd_ref)

    @pl.when(i < n_real_tiles)
    def _():
        vec = vec_ref[...]
        mod = mod_ref[0]
        headsum = lambda z: _headsum(z, hd_ref[...], hdt_ref[...])
        inv_k = 1.0 / RWKV_HEAD
        y = yf_ref[0] + yb_ref[0]
        yc = y - headsum(y) * inv_k
        var = headsum(yc * yc) * inv_k
        yn = yc * lax.rsqrt(var + GN_EPS) * vec[0:1] + vec[1:2]
        bonus = headsum(r_ref[0] * (kd0_ref[0] + kd1_ref[0]) * vec[2:3]) * v_ref[0]
        out = (yn + bonus) * g_ref[0]
        x3 = x_ref[...] + mod[2:3] * _dotb(out, wo_ref[...])
        x3_ref[...] = x3
        _route(x3, mod, vec[3:4], rwt_ref, rb_ref, u_ref, off_ref, h2_ref, pos_ref, gd_ref,
               i % tiles_per_moe == 0)


def _readout(yf, yb, r, kd0, kd1, v, g, x2d, mod, vec, wo, hd, hdt, rwt, rb, n_ctx_tiles, moe_tile,
             moe_span, tokens_per_batch):
    ntok, d = x2d.shape
    t = TOK_TILE
    n_e = rwt.shape[0]
    tpb = tokens_per_batch // t
    n_real = ntok // t
    n_pad = _padded_tokens(ntok, moe_span)
    route_specs, route_shapes = _route_outputs(n_pad, d, n_e, moe_tile)
    real = lambda i: jnp.minimum(i, n_real - 1)
    seq = pl.BlockSpec((1, t, d), lambda i: (real(i) // tpb, n_ctx_tiles + real(i) % tpb, 0))
    tokspec = pl.BlockSpec((t, d), lambda i: (real(i), 0))
    full = lambda a: pl.BlockSpec(a.shape, lambda i: (0,) * a.ndim)
    u = _route_consts(t)
    kern = functools.partial(_readout_kernel, tiles_per_moe=moe_tile // t, n_real_tiles=n_real)
    return pl.pallas_call(
        kern,
        grid=(n_pad // t,),
        in_specs=[seq] * 7 + [tokspec, pl.BlockSpec((1, 8, d), lambda i: (real(i) // tpb, 0, 0)),
                              full(vec), full(wo), full(hd), full(hdt), full(rwt), full(rb), full(u)],
        out_specs=[tokspec] + route_specs,
        out_shape=[jax.ShapeDtypeStruct((ntok, d), F32)] + route_shapes,
        scratch_shapes=[pltpu.VMEM((n_e, 128), F32)],
        compiler_params=_params("arbitrary"),
        name="readout",
    )(yf, yb, r, kd0, kd1, v, g, x2d, mod, vec, wo, hd, hdt, rwt, rb, u)


def _final_kernel(x_ref, ft_ref, mod_ref, g_ref, o_ref):
    x = x_ref[...] + mod_ref[0][5:6] * jnp.transpose(ft_ref[...])
    o_ref[...] = _rms(x) * g_ref[0:1]


def _final(x2d, ft, mod, gvec, tokens_per_batch):
    ntok, d = x2d.shape
    t = TOK_TILE
    tpb = tokens_per_batch // t
    return pl.pallas_call(
        _final_kernel,
        grid=(ntok // t,),
        in_specs=[pl.BlockSpec((t, d), lambda i: (i, 0)),
                  pl.BlockSpec((d, t), lambda i: (0, i)),
                  pl.BlockSpec((1, 8, d), lambda i: (i // tpb, 0, 0)),
                  pl.BlockSpec((8, d), lambda i: (0, 0))],
        out_specs=pl.BlockSpec((t, d), lambda i: (i, 0)),
        out_shape=jax.ShapeDtypeStruct((ntok, d), F32),
        compiler_params=_params("parallel"),
        name="final",
    )(x2d, ft, mod, gvec)


def _rows8(rows, d):
    n = -(-len(rows) // 8) * 8
    out = jnp.zeros((n, d), F32)
    return out.at[:len(rows)].set(jnp.stack([jnp.asarray(r, F32) for r in rows]))


def _pad_lora(w, total):
    two, r, d = w.shape
    out = jnp.zeros((two, total, d), w.dtype)
    for dd in range(two):
        out = out.at[dd, dd * r:(dd + 1) * r].set(w[dd])
    return out


def kernel(x, c, ctx, c_ctx, norm_g, ada_w, ada_b, pool_w, pool_ls, rwkv_mu, rwkv_w_rkv, rwkv_w0, rwkv_w1, rwkv_w2, rwkv_a0, rwkv_a1, rwkv_a2, rwkv_g1, rwkv_g2, rwkv_k_k, rwkv_k_a, rwkv_r_k, rwkv_ln_g, rwkv_ln_b, rwkv_w_o, moe_router_w, moe_router_b, moe_w_gu, moe_b_gu, moe_w_down, moe_b_down, final_g):
    nb, seq, d = x.shape
    n_ctx = ctx.shape[1]
    depth = norm_g.shape[0]
    n_e = moe_router_w.shape[2]
    t = TOK_TILE
    assert depth == 2 and nb <= 7 and seq % t == 0 and n_ctx % t == 0 and t % GRID_W == 0 and d % GROUP == 0
    n_lat, n_ctxtok = nb * seq, nb * n_ctx
    moe_tile_lat = min(MOE_TILE, n_lat)
    moe_tile_ctx = min(MOE_TILE, n_ctxtok)
    group_lat = min(MOE_GROUP, -(-n_lat // moe_tile_lat))
    group_ctx = min(MOE_GROUP, -(-n_ctxtok // moe_tile_ctx))
    n_ctx_tiles = n_ctx // t
    seq_total = n_ctx + seq

    cond8 = jnp.zeros((8, d), F32).at[:nb].set(c).at[nb].set(c_ctx)
    ada = _ada(cond8, ada_w, ada_b).reshape(depth, 8, N_MOD, d)
    pad8 = lambda m: jnp.pad(m, ((0, 0), (0, 8 - N_MOD), (0, 0)))
    mod_lat = [pad8(ada[i, :nb]) for i in range(depth)]
    mod_ctx = [pad8(jnp.broadcast_to(ada[i, nb][None], (nb, N_MOD, d))) for i in range(depth)]

    wgu_all = _expert_weights_bf16(moe_w_gu, transpose=False)
    wdt_all = _expert_weights_bf16(moe_w_down, transpose=True)

    def expert_params(i):
        return i, wgu_all, moe_b_gu[i].reshape(n_e, 1, -1), wdt_all, moe_b_down[i].T

    def router_params(i):
        return moe_router_w[i].T, moe_router_b[i].reshape(n_e, 1)

    x2d = x.reshape(n_lat, d)
    ctx2d = ctx.reshape(n_ctxtok, d)

    vec0 = _rows8([norm_g[0, 0], norm_g[0, 1], pool_ls[0]], d)
    pw = pool_w[0].astype(BF16)
    rwt0, rb0 = router_params(0)
    ex0 = expert_params(0)
    x1, h2, pos, gd = _pool_layer(x2d, mod_lat[0], vec0, pw, rwt0, rb0, GRID_W, moe_tile_lat,
                                  group_lat * moe_tile_lat, seq)
    ft_lat = _moe(h2, pos, gd, *ex0, group_lat)
    c1, h2c, posc, gdc = _pool_layer(ctx2d, mod_ctx[0], vec0, pw, rwt0, rb0, n_ctx, moe_tile_ctx,
                                     group_ctx * moe_tile_ctx, n_ctx)
    ft_ctx = _moe(h2c, posc, gdc, *ex0, group_ctx)

    vec1 = _rows8([norm_g[1, 0]], d)
    x2, h_all = _respre(x1, ft_lat, mod_lat[0], mod_lat[1], vec1, None, seq, n_ctx_tiles, seq_total)
    _, h_all = _respre(c1, ft_ctx, mod_ctx[0], mod_ctx[1], vec1, h_all, n_ctx, 0, seq_total)

    fv = _rows8([rwkv_mu[0, m] for m in range(6)]
                + [rwkv_w0[0, 0], rwkv_w0[0, 1], rwkv_a0[0, 0], rwkv_a0[0, 1], rwkv_k_k[0], rwkv_k_a[0]], d)
    lora = rwkv_w1.shape[3]
    cat2 = lambda w: jnp.concatenate([w[0, 0], w[0, 1]], axis=1).astype(BF16)
    hd, hdt = _head_indicators(d)
    feats = _features(h_all, fv, rwkv_w_rkv[0].astype(BF16),
                      cat2(rwkv_w1), _pad_lora(rwkv_w2[0], 2 * lora).astype(BF16),
                      cat2(rwkv_a1), _pad_lora(rwkv_a2[0], 2 * rwkv_a1.shape[3]).astype(BF16),
                      rwkv_g1[0].astype(BF16), rwkv_g2[0].astype(BF16), hd, hdt, n_ctx_tiles)
    lw0, lw1, kd0, kd1, b0, b1, v, kk, r, g = feats
    yf, yb = _scan((lw0, lw1, kd0, kd1, b0, b1, v, kk, r), n_ctx_tiles)

    vec_ro = _rows8([rwkv_ln_g[0], rwkv_ln_b[0], rwkv_r_k[0].reshape(-1), norm_g[1, 1]], d)
    rwt1, rb1 = router_params(1)
    x3, h2, pos, gd = _readout(yf, yb, r, kd0, kd1, v, g, x2, mod_lat[1], vec_ro,
                               rwkv_w_o[0].astype(BF16), hd, hdt, rwt1, rb1, n_ctx_tiles, moe_tile_lat,
                               group_lat * moe_tile_lat, seq)
    ft = _moe(h2, pos, gd, *expert_params(1), group_lat)
    out = _final(x3, ft, mod_lat[1], _rows8([final_g], d), seq)
    return out.reshape(nb, seq, d)
```

```python
import functools

import numpy as np
import jax
import jax.numpy as jnp
from jax import lax
from jax.experimental import pallas as pl
from jax.experimental.pallas import tpu as pltpu

F32 = jnp.float32
BF16 = jnp.bfloat16
NN = (((1,), (0,)), ((), ()))
NT = (((1,), (1,)), ((), ()))
TN = (((0,), (0,)), ((), ()))

N_MOD = 6
NORM_EPS = 1e-6
GRID_W = 64
POOL_WINDOWS = (2, 4, 8, 16)
RWKV_HEAD = 64
GN_EPS = 64e-5
TOP_K = 4
SWIGLU_ALPHA = 1.702
SWIGLU_LIMIT = 7.0

TOK_TILE = 256
TILE_SPLIT = 2
MOE_TILE = 1792
MOE_GROUP = 4
MOE_ROWS = 256
MOE_CHUNK_BLOCKS = 4
CHUNK = 64
GROUP_HEADS = 2
GROUP = GROUP_HEADS * RWKV_HEAD
HEAD_PAD = 128
VMEM_LIMIT_BYTES = 56 * 1024 * 1024


def _dotx(a, b, dims=NN):
    return lax.dot_general(a, b, dims, precision=lax.Precision.HIGHEST, preferred_element_type=F32)


def _dotb(a, b, dims=NN):
    return lax.dot_general(a.astype(BF16), b.astype(BF16), dims, preferred_element_type=F32)


def _params(*sem):
    return pltpu.CompilerParams(dimension_semantics=sem, vmem_limit_bytes=VMEM_LIMIT_BYTES)


def _padded_tokens(ntok, tile):
    return -(-ntok // tile) * tile


def _rms(x):
    return x * lax.rsqrt(jnp.mean(x * x, axis=-1, keepdims=True) + NORM_EPS)


def _sigmoid(x):
    return 1.0 / (1.0 + jnp.exp(-x))


def _ada_kernel(c_ref, w_ref, b_ref, o_ref):
    c = c_ref[...]
    o_ref[0] = _dotx(c * _sigmoid(c), w_ref[0]) + b_ref[0]


def _ada(cond8, ada_w, ada_b):
    depth, d, nd = ada_w.shape
    return pl.pallas_call(
        _ada_kernel,
        grid=(depth, nd // d),
        in_specs=[pl.BlockSpec((8, d), lambda i, n: (0, 0)),
                  pl.BlockSpec((1, d, d), lambda i, n: (i, 0, n)),
                  pl.BlockSpec((1, 1, d), lambda i, n: (i, 0, n))],
        out_specs=pl.BlockSpec((1, 8, d), lambda i, n: (i, 0, n)),
        out_shape=jax.ShapeDtypeStruct((depth, 8, nd), F32),
        compiler_params=_params("parallel", "parallel"),
        name="ada",
    )(cond8, ada_w, ada_b.reshape(depth, 1, nd))


def _route_padding(h2_ref, pos_ref, gd_ref):
    h2_ref[...] = jnp.zeros_like(h2_ref)
    pos_ref[...] = jnp.full(pos_ref.shape, -1, jnp.int32)
    gd_ref[...] = jnp.zeros_like(gd_ref)


def _interleave(*gens):
    out = [None] * len(gens)
    live = dict(enumerate(gens))
    while live:
        for k, g in list(live.items()):
            try:
                next(g)
            except StopIteration as stop:
                out[k] = stop.value
                del live[k]
    return out


def _row_splits(n_rows, n_split):
    step = n_rows // n_split
    return [slice(k * step, (k + 1) * step) for k in range(n_split)]


def _route_rows(x_new, mod, g2, rwt_ref, rb_ref, h2_ref, rows):
    h2 = _rms(x_new) * g2 * (1.0 + mod[4:5]) + mod[3:4]
    h2_ref[rows, :] = h2.astype(BF16)
    yield
    logits = _dotx(rwt_ref[...], h2, NT) + rb_ref[...]
    yield
    n_e = logits.shape[0]
    eio = lax.broadcasted_iota(jnp.int32, logits.shape, 0).astype(F32)
    live = logits
    sels, vals = [], []
    for _ in range(TOP_K):
        m = jnp.max(live, axis=0, keepdims=True)
        idx = jnp.min(jnp.where(live == m, eio, float(n_e)), axis=0, keepdims=True)
        sel = eio == idx
        sels.append(sel)
        vals.append(m)
        live = jnp.where(sel, -jnp.inf, live)
        yield
    exps = [jnp.exp(v - vals[0]) for v in vals]
    inv = 1.0 / (exps[0] + exps[1] + exps[2] + exps[3])
    gd = jnp.zeros_like(logits)
    maskf = jnp.zeros_like(logits)
    for k in range(TOP_K):
        gd = jnp.where(sels[k], exps[k] * inv, gd)
        maskf = jnp.where(sels[k], 1.0, maskf)
    return maskf, gd


def _route_finish(parts, u_ref, off_ref, pos_ref, gd_ref, reset):
    @pl.when(reset)
    def _():
        off_ref[...] = jnp.zeros_like(off_ref)

    off = off_ref[:, 0:1]
    lane0 = 0
    for maskf, gd in parts:
        n = maskf.shape[1]
        incl = _dotb(maskf, u_ref[0:n, 0:n])
        pos_ref[0, :, lane0:lane0 + n] = jnp.where(maskf > 0.5, off + incl - 1.0, -1.0).astype(jnp.int32)
        gd_ref[0, :, lane0:lane0 + n] = gd
        off = off + jnp.sum(maskf, axis=1, keepdims=True)
        lane0 += n
    off_ref[...] = jnp.broadcast_to(off, off_ref.shape)


def _route_outputs(n_pad, d, n_e, moe_tile):
    t = TOK_TILE
    tpm = moe_tile // t
    nt = n_pad // moe_tile
    spec = pl.BlockSpec((1, n_e, t), lambda i: (i // tpm, 0, i % tpm))
    shapes = [jax.ShapeDtypeStruct((n_pad, d), BF16),
              jax.ShapeDtypeStruct((nt, n_e, moe_tile), jnp.int32),
              jax.ShapeDtypeStruct((nt, n_e, moe_tile), F32)]
    return [pl.BlockSpec((t, d), lambda i: (i, 0)), spec, spec], shapes


def _route_consts(tile):
    s = np.arange(tile)
    return jnp.asarray((s[:, None] <= s[None, :]).astype(np.float32), BF16)


def _pool_kernel(x_ref, mod_ref, vec_ref, pm_ref, pw_ref, rwt_ref, rb_ref, u_ref,
                 x1_ref, h2_ref, pos_ref, gd_ref, off_ref, *, tiles_per_moe, n_real_tiles, n_split):
    i = pl.program_id(0)

    @pl.when(i >= n_real_tiles)
    def _():
        _route_padding(h2_ref, pos_ref, gd_ref)

    def rows_chain(rows, mod, vec):
        x = x_ref[rows, :]
        h = _rms(x) * vec[0:1] * (1.0 + mod[1:2]) + mod[0:1]
        yield
        gw = pw_ref.shape[1]
        ys = []
        for g in range(len(POOL_WINDOWS)):
            d = _dot3(pm_ref[g, rows, rows], h[:, g * gw:(g + 1) * gw])
            yield
            ys.append(_dotb(d, pw_ref[g]))
            yield
        x1 = x + mod[2:3] * (jnp.concatenate(ys, axis=1) * vec[2:3])
        x1_ref[rows, :] = x1
        return (yield from _route_rows(x1, mod, vec[1:2], rwt_ref, rb_ref, h2_ref, rows))

    @pl.when(i < n_real_tiles)
    def _():
        mod = mod_ref[0]
        vec = vec_ref[...]
        parts = _interleave(*[rows_chain(rows, mod, vec) for rows in _row_splits(x_ref.shape[0], n_split)])
        _route_finish(parts, u_ref, off_ref, pos_ref, gd_ref, i % tiles_per_moe == 0)


def _pool_matrices(tile, row_len):
    p = np.arange(tile)
    pp = p % row_len
    out = []
    for win in POOL_WINDOWS:
        lo = np.clip(pp - win // 2, 0, row_len - 1)
        hi = np.clip(pp + win // 2 - 1, 0, row_len - 1)
        cnt = (hi - lo + 1).astype(np.float64)
        same = (p[:, None] // row_len) == (p[None, :] // row_len)
        inwin = same & (pp[None, :] >= lo[:, None]) & (pp[None, :] <= hi[:, None])
        out.append(inwin / cnt[:, None] - np.eye(tile))
    return jnp.asarray(np.stack(out).astype(np.float32))


def _pool_layer(x2d, mod, vec, pool_w, rwt, rb, row_len, moe_tile, moe_span, tokens_per_batch):
    ntok, d = x2d.shape
    t = TOK_TILE
    n_e = rwt.shape[0]
    tiles_per_batch = tokens_per_batch // t
    n_real = ntok // t
    n_pad = _padded_tokens(ntok, moe_span)
    route_specs, route_shapes = _route_outputs(n_pad, d, n_e, moe_tile)
    pm = _pool_matrices(t, row_len)
    n_split = TILE_SPLIT if (t // TILE_SPLIT) % row_len == 0 else 1
    kern = functools.partial(_pool_kernel, tiles_per_moe=moe_tile // t, n_real_tiles=n_real, n_split=n_split)
    const2 = lambda i: (0, 0)
    const3 = lambda i: (0, 0, 0)
    real = lambda i: jnp.minimum(i, n_real - 1)
    return pl.pallas_call(
        kern,
        grid=(n_pad // t,),
        in_specs=[pl.BlockSpec((t, d), lambda i: (real(i), 0)),
                  pl.BlockSpec((1, 8, d), lambda i: (real(i) // tiles_per_batch, 0, 0)),
                  pl.BlockSpec((8, d), const2),
                  pl.BlockSpec(pm.shape, const3),
                  pl.BlockSpec(pool_w.shape, const3),
                  pl.BlockSpec(rwt.shape, const2),
                  pl.BlockSpec(rb.shape, const2),
                  pl.BlockSpec((t, t), const2)],
        out_specs=[pl.BlockSpec((t, d), lambda i: (real(i), 0))] + route_specs,
        out_shape=[jax.ShapeDtypeStruct((ntok, d), F32)] + route_shapes,
        scratch_shapes=[pltpu.VMEM((n_e, 128), F32)],
        compiler_params=_params("arbitrary"),
        name="pool",
    )(x2d, mod, vec, pm, pool_w, rwt, rb, _route_consts(t))


def _cast_kernel(w_ref, o_ref, *, transpose):
    w = w_ref[0, 0].astype(BF16)
    o_ref[0, 0] = jnp.transpose(w) if transpose else w


def _expert_weights_bf16(w, transpose):
    n_l, n_e, r, c = w.shape
    blk = min(r, c)
    out_idx = (lambda l, e, i, j: (l, e, j, i)) if transpose else (lambda l, e, i, j: (l, e, i, j))
    return pl.pallas_call(
        functools.partial(_cast_kernel, transpose=transpose),
        grid=(n_l, n_e, r // blk, c // blk),
        in_specs=[pl.BlockSpec((1, 1, blk, blk), lambda l, e, i, j: (l, e, i, j))],
        out_specs=pl.BlockSpec((1, 1, blk, blk), out_idx),
        out_shape=jax.ShapeDtypeStruct((n_l, n_e, c, r) if transpose else w.shape, BF16),
        compiler_params=_params("parallel", "parallel", "parallel", "parallel"),
        name="wprep",
    )(w)


def _one_hot_rows(prow, r0, n_rows):
    rows = lax.broadcasted_iota(jnp.int32, (n_rows, prow.shape[1]), 0) + r0
    return jnp.where(prow == rows, 1.0, 0.0).astype(BF16)


def _expert_kernel(grp_ref, sub_ref, e_ref, r0_ref, half_ref,
                   pos_ref, gd_ref, x_ref, wgu_ref, bgu_ref, wdt_ref, o_ref):
    b = pl.program_id(0)
    f = wdt_ref.shape[3]
    n_rows = o_ref.shape[2]
    ts = pos_ref.shape[2]
    e = e_ref[b]
    sub = sub_ref[b]
    prow = pos_ref[sub, pl.ds(e, 1), :]
    grow = gd_ref[sub, pl.ds(e, 1), :]
    g_hi = grow.astype(BF16)
    g_mid = (grow - g_hi.astype(F32)).astype(BF16)
    g_lo = (grow - g_hi.astype(F32) - g_mid.astype(F32)).astype(BF16)
    g3 = jnp.concatenate([g_hi, g_mid, g_lo, jnp.zeros((5, ts), BF16)], axis=0)

    def rows_out(n):
        oh = _one_hot_rows(prow, r0_ref[b], n)
        x = x_ref[pl.ds(pl.multiple_of(sub * ts, ts), ts), :]
        xg = jnp.dot(oh, x, preferred_element_type=F32).astype(BF16)
        gu = jnp.dot(xg, wgu_ref[0, 0], preferred_element_type=F32) + bgu_ref[0]
        gate = jnp.minimum(gu[:, :f], SWIGLU_LIMIT)
        up = jnp.clip(gu[:, f:], -SWIGLU_LIMIT, SWIGLU_LIMIT)
        act = (up + 1.0) * (gate * _sigmoid(SWIGLU_ALPHA * gate))
        yt = lax.dot_general(wdt_ref[0, 0], act.astype(BF16), NT, preferred_element_type=F32)
        g8 = lax.dot_general(g3, oh, NT, preferred_element_type=F32)
        return (yt * (g8[0:1] + g8[1:2] + g8[2:3])).astype(BF16)

    @pl.when(half_ref[b] == 0)
    def _():
        o_ref[0] = rows_out(n_rows)

    @pl.when(half_ref[b] == 1)
    def _():
        o_ref[0, :, 0:n_rows // 2] = rows_out(n_rows // 2)
        o_ref[0, :, n_rows // 2:n_rows] = jnp.zeros((o_ref.shape[1], n_rows // 2), BF16)


def _combine_kernel(tile_ref, first_ref, blk_ref, e_ref, r0_ref, pos_ref, gd_ref, *rest):
    yt_refs, bdt_ref, o_ref = rest[:MOE_CHUNK_BLOCKS], rest[-2], rest[-1]
    c = pl.program_id(0)
    n_rows = yt_refs[0].shape[2]

    @pl.when(first_ref[c] == 1)
    def _():
        o_ref[...] = _dot3(bdt_ref[...], gd_ref[0])

    slot = lambda q: c * MOE_CHUNK_BLOCKS + q
    oh = jnp.concatenate([_one_hot_rows(pos_ref[0, pl.ds(e_ref[slot(q)], 1), :], r0_ref[slot(q)], n_rows)
                          for q in range(MOE_CHUNK_BLOCKS)], axis=0)
    yt = jnp.concatenate([ref[0] for ref in yt_refs], axis=1)
    o_ref[...] += jnp.dot(yt, oh, preferred_element_type=F32)


def _moe_tables(pos, group):
    nt, n_e, ts = pos.shape
    ng = nt // group
    rb, cb = MOE_ROWS, MOE_CHUNK_BLOCKS
    i32 = lambda a: a.astype(jnp.int32)
    cnt = jnp.sum(i32(pos >= 0), axis=-1)
    nb = (cnt + rb - 1) // rb
    max_tile_blocks = ts * TOP_K // rb + n_e
    order = lambda a: a.reshape(ng, group, n_e).transpose(0, 2, 1).reshape(-1)
    flat = order(nb)
    cum = jnp.cumsum(flat)
    start = cum - flat
    n_blocks = cum[-1]
    b = jnp.minimum(jnp.arange(nt * max_tile_blocks, dtype=jnp.int32), n_blocks - 1)
    count_le = lambda sorted_, q: jnp.sum(i32(sorted_[None, :] <= q[:, None]), axis=1)
    idx = jnp.minimum(count_le(cum, b), flat.shape[0] - 1)
    r0 = (b - start[idx]) * rb
    expert_tables = (i32(idx // (n_e * group)), i32(idx % group), i32((idx // group) % n_e), i32(r0),
                     i32(order(cnt)[idx] - r0 <= rb // 2))
    ce = jnp.cumsum(nb, axis=1)
    nch = (ce[:, -1] + cb - 1) // cb
    cch = jnp.cumsum(nch)
    n_chunks = cch[-1]
    max_chunks = nt * (-(-max_tile_blocks // cb))
    c = jnp.minimum(jnp.arange(max_chunks, dtype=jnp.int32), n_chunks - 1)
    tile = jnp.minimum(count_le(cch, c), nt - 1)
    lc = c - (cch - nch)[tile]
    slot = (lc[:, None] * cb + jnp.arange(cb, dtype=jnp.int32)[None, :]).reshape(-1)
    tile_s = jnp.repeat(tile, cb)
    used = slot < ce[tile_s, -1]
    slot = jnp.where(used, slot, 0)
    ce_s = ce[tile_s]
    e_s = jnp.minimum(jnp.sum(i32(slot[:, None] >= ce_s), axis=1), n_e - 1)
    local = slot - jnp.take_along_axis(ce_s - nb[tile_s], e_s[:, None], axis=1)[:, 0]
    blk = start.reshape(ng, n_e, group)[tile_s // group, e_s, tile_s % group] + local
    combine_tables = (i32(tile), i32(lc == 0), i32(blk), i32(e_s), i32(jnp.where(used, local * rb, 1 << 20)))
    return expert_tables, i32(n_blocks), combine_tables, i32(n_chunks)


def _moe(h2, pos, gd, layer, wgu, bgu, wdt, bdt, group):
    ntok, d = h2.shape
    nt, n_e, ts = pos.shape
    f2 = wgu.shape[3]
    rb, cb = MOE_ROWS, MOE_CHUNK_BLOCKS
    expert_tables, n_blocks, combine_tables, n_chunks = _moe_tables(pos, group)
    max_blocks = expert_tables[0].shape[0]
    yt = pl.pallas_call(
        _expert_kernel,
        grid_spec=pltpu.PrefetchScalarGridSpec(
            num_scalar_prefetch=5,
            grid=(n_blocks,),
            in_specs=[pl.BlockSpec((group, n_e, ts), lambda b, gp, sb, ee, rr, hf: (gp[b], 0, 0)),
                      pl.BlockSpec((group, n_e, ts), lambda b, gp, sb, ee, rr, hf: (gp[b], 0, 0)),
                      pl.BlockSpec((group * ts, d), lambda b, gp, sb, ee, rr, hf: (gp[b], 0),
                                   pipeline_mode=pl.Buffered(1)),
                      pl.BlockSpec((1, 1, d, f2), lambda b, gp, sb, ee, rr, hf: (layer, ee[b], 0, 0)),
                      pl.BlockSpec((1, 1, f2), lambda b, gp, sb, ee, rr, hf: (ee[b], 0, 0)),
                      pl.BlockSpec((1, 1, d, f2 // 2), lambda b, gp, sb, ee, rr, hf: (layer, ee[b], 0, 0))],
            out_specs=pl.BlockSpec((1, d, rb), lambda b, gp, sb, ee, rr, hf: (b, 0, 0)),
        ),
        out_shape=jax.ShapeDtypeStruct((max_blocks, d, rb), BF16),
        compiler_params=_params("arbitrary"),
        name="expert",
    )(*expert_tables, pos, gd, h2, wgu, bgu, wdt)
    yt_spec = lambda q: pl.BlockSpec((1, d, rb), lambda c, tl, fs, bk, ee, rr: (bk[c * cb + q], 0, 0))
    return pl.pallas_call(
        _combine_kernel,
        grid_spec=pltpu.PrefetchScalarGridSpec(
            num_scalar_prefetch=5,
            grid=(n_chunks,),
            in_specs=[pl.BlockSpec((1, n_e, ts), lambda c, tl, fs, bk, ee, rr: (tl[c], 0, 0)),
                      pl.BlockSpec((1, n_e, ts), lambda c, tl, fs, bk, ee, rr: (tl[c], 0, 0))]
                     + [yt_spec(q) for q in range(cb)]
                     + [pl.BlockSpec((d, n_e), lambda c, tl, fs, bk, ee, rr: (0, 0))],
            out_specs=pl.BlockSpec((d, ts), lambda c, tl, fs, bk, ee, rr: (0, tl[c]),
                                   pipeline_mode=pl.Buffered(1)),
        ),
        out_shape=jax.ShapeDtypeStruct((d, ntok), F32),
        compiler_params=_params("arbitrary"),
        name="combine",
    )(*combine_tables, pos, gd, *([yt] * cb), bdt)


def _respre_kernel(x_ref, ft_ref, modp_ref, modc_ref, vec_ref, *rest):
    x2_ref, h_ref = rest[-2], rest[-1]
    x2 = x_ref[...] + modp_ref[0][5:6] * jnp.transpose(ft_ref[...])
    x2_ref[...] = x2
    modc = modc_ref[0]
    h_ref[0] = _rms(x2) * vec_ref[0:1] * (1.0 + modc[1:2]) + modc[0:1]


def _respre(x2d, ft, modp, modc, vec, h_all, tokens_per_batch, tile_offset, seq_total):
    ntok, d = x2d.shape
    t = TOK_TILE
    nb = modp.shape[0]
    tpb = tokens_per_batch // t
    in_specs = [pl.BlockSpec((t, d), lambda i: (i, 0)),
                pl.BlockSpec((d, t), lambda i: (0, i)),
                pl.BlockSpec((1, 8, d), lambda i: (i // tpb, 0, 0)),
                pl.BlockSpec((1, 8, d), lambda i: (i // tpb, 0, 0)),
                pl.BlockSpec((8, d), lambda i: (0, 0))]
    args = [x2d, ft, modp, modc, vec]
    aliases = {}
    if h_all is not None:
        in_specs.append(pl.BlockSpec(memory_space=pl.ANY))
        args.append(h_all)
        aliases = {5: 1}
    return pl.pallas_call(
        _respre_kernel,
        grid=(ntok // t,),
        in_specs=in_specs,
        out_specs=[pl.BlockSpec((t, d), lambda i: (i, 0)),
                   pl.BlockSpec((1, t, d), lambda i: (i // tpb, tile_offset + i % tpb, 0))],
        out_shape=[jax.ShapeDtypeStruct((ntok, d), F32),
                   jax.ShapeDtypeStruct((nb, seq_total, d), F32)],
        input_output_aliases=aliases,
        compiler_params=_params("parallel"),
        name="respre",
    )(*args)


def _feat_kernel(h_ref, hp_ref, hn_ref, fv_ref, wrkv_ref, w1_ref, w2_ref, a1_ref, a2_ref, g1_ref, g2_ref,
                 hd_ref, hdt_ref,
                 lw0_ref, lw1_ref, kd0_ref, kd1_ref, b0_ref, b1_ref, v_ref, kk_ref, r_ref, g_ref,
                 *, n_ctx_tiles, n_tiles):
    i = pl.program_id(1)
    h = h_ref[0]
    t = h.shape[0]
    fv = fv_ref[...]
    first = jnp.logical_or(i == 0, i == n_ctx_tiles)
    last = jnp.logical_or(i == n_ctx_tiles - 1, i == n_tiles - 1)
    prow = jnp.where(first, 0.0, hp_ref[0][7:8])
    nrow = jnp.where(last, 0.0, hn_ref[0][0:1])
    rio = lax.broadcasted_iota(jnp.int32, h.shape, 0)
    hdn = jnp.where(rio == 0, prow, pltpu.roll(h, 1, 0))
    hup = jnp.where(rio == t - 1, nrow, pltpu.roll(h, t - 1, 0))
    xx = 0.5 * (hdn + hup) - h

    mix = lambda m: h + xx * fv[m:m + 1]
    r = _dotb(mix(0), wrkv_ref[0])
    k = _dotb(mix(2), wrkv_ref[1])
    v = _dotb(mix(3), wrkv_ref[2])
    tw = jnp.tanh(_dotb(mix(1), w1_ref[...]))
    la = _dotb(mix(4), a1_ref[...])
    g = _dotb(_sigmoid(_dotb(mix(5), g1_ref[...])), g2_ref[...])

    kkraw = k * fv[10:11]
    ss = _headsum(kkraw * kkraw, hd_ref[...], hdt_ref[...])
    kk = kkraw / jnp.maximum(jnp.sqrt(ss), 1e-12)
    k_a = fv[11:12]
    decay_scale = float(np.exp(-0.5))
    for d, (lw_ref, kd_ref, b_ref) in enumerate(((lw0_ref, kd0_ref, b0_ref), (lw1_ref, kd1_ref, b1_ref))):
        zw = fv[6 + d:7 + d] + _dotb(tw, w2_ref[d])
        lw_ref[0] = -decay_scale * _sigmoid(zw)
        icl = _sigmoid(fv[8 + d:9 + d] + _dotb(la, a2_ref[d]))
        kd_ref[0] = k * (1.0 + (icl - 1.0) * k_a)
        b_ref[0] = kk * icl
    v_ref[0] = v
    kk_ref[0] = kk
    r_ref[0] = r
    g_ref[0] = g


def _head_indicators(d):
    hd = np.zeros((d, HEAD_PAD), np.float32)
    hd[np.arange(d), np.arange(d) // RWKV_HEAD] = 1.0
    return jnp.asarray(hd), jnp.asarray(hd.T.copy())


def _features(h_all, fv, wrkv, w1c, w2p, a1c, a2p, g1, g2, hd, hdt, n_ctx_tiles):
    nb, s, d = h_all.shape
    t = TOK_TILE
    nt = s // t
    r8 = t // 8
    kern = functools.partial(_feat_kernel, n_ctx_tiles=n_ctx_tiles, n_tiles=nt)
    full = lambda a: pl.BlockSpec(a.shape, lambda b, i: (0,) * a.ndim)
    tok = pl.BlockSpec((1, t, d), lambda b, i: (b, i, 0))
    return pl.pallas_call(
        kern,
        grid=(nb, nt),
        in_specs=[tok,
                  pl.BlockSpec((1, 8, d), lambda b, i: (b, jnp.maximum(i * r8 - 1, 0), 0)),
                  pl.BlockSpec((1, 8, d), lambda b, i: (b, jnp.minimum((i + 1) * r8, s // 8 - 1), 0)),
                  full(fv), full(wrkv), full(w1c), full(w2p), full(a1c), full(a2p), full(g1), full(g2),
                  full(hd), full(hdt)],
        out_specs=[tok] * 10,
        out_shape=[jax.ShapeDtypeStruct((nb, s, d), F32)] * 10,
        compiler_params=_params("parallel", "parallel"),
        name="feat",
    )(h_all, h_all, h_all, fv, wrkv, w1c, w2p, a1c, a2p, g1, g2, hd, hdt)


def _scan_masks():
    n = GROUP_HEADS * CHUNK
    rr = np.arange(n)[:, None]
    cc = np.arange(n)[None, :]
    bd = (rr // CHUNK) == (cc // CHUNK)
    r_, c_ = rr % CHUNK, cc % CHUNK
    out = []
    for rev in (False, True):
        strict = bd & ((c_ > r_) if rev else (c_ < r_))
        incl = bd & ((c_ >= r_) if rev else (c_ <= r_))
        d16 = strict & ((r_ // 16) == (c_ // 16))
        l1 = strict & ((r_ // 32) == (c_ // 32)) & ((r_ // 16) != (c_ // 16))
        l2 = strict & ((r_ // 32) != (c_ // 32))
        out.append([strict, incl, d16, l1, l2])
    masks = [out[0][m] for m in range(5)] + [out[1][m] for m in range(5)] + [bd, rr == cc]
    hm = np.zeros((8, GROUP), np.float32)
    for j in range(GROUP_HEADS):
        hm[j, j * RWKV_HEAD:(j + 1) * RWKV_HEAD] = 1.0
    tri3 = np.stack([np.tile(out[rev][1][0:CHUNK, 0:CHUNK], (1, 3)) for rev in (0, 1)]).astype(np.float32)
    return jnp.asarray(np.stack(masks).astype(np.float32)), jnp.asarray(tri3, BF16), jnp.asarray(hm)


def _split3(a):
    a1 = a.astype(BF16)
    r1 = a - a1.astype(F32)
    a2 = r1.astype(BF16)
    return a1, a2, (r1 - a2.astype(F32)).astype(BF16)


def _dot3(a, b, dims=NN):
    a1, a2, _ = _split3(a)
    b1, b2, _ = _split3(b)
    mm = lambda p, q: lax.dot_general(p, q, dims, preferred_element_type=F32)
    return mm(a1, b1) + (mm(a1, b2) + mm(a2, b1))


def _dot01(a, ind):
    ib = ind.astype(BF16)
    a1, a2, _ = _split3(a)
    mm = lambda p: jnp.dot(p, ib, preferred_element_type=F32)
    return mm(a1) + mm(a2)


def _headsum(z, hd, hdt):
    return _dot01(_dot01(z, hd), hdt)


def _chunk_terms(blocks, msk_ref, tri_ref, hm):
    c_len = blocks[0][0].shape[0]
    n4 = GROUP_HEADS * c_len
    revs = [blk[6] for blk in blocks]
    mask = lambda k: [msk_ref[(5 if rev else 0) + k] for rev in revs]
    bd, eye = msk_ref[10], msk_ref[11]
    each = lambda f, *cols: [f(*args) for args in zip(*cols)]
    stack4 = lambda z: jnp.concatenate([z * hm[j:j + 1] for j in range(GROUP_HEADS)], axis=0)
    tile4 = lambda z: jnp.concatenate([z] * GROUP_HEADS, axis=0)
    fold4 = lambda z: sum(z[j * c_len:(j + 1) * c_len] for j in range(GROUP_HEADS))
    lw, kd, b, v, kk, r = [[blk[k] for blk in blocks] for k in range(6)]

    tri3 = [tri_ref[1 if rev else 0] for rev in revs]
    parts = each(lambda z: jnp.concatenate(_split3(z), axis=0), lw)
    c = each(lambda t, p: jnp.dot(t, p, preferred_element_type=F32), tri3, parts)
    ctot = each(lambda c_, rev: c_[0:1] if rev else c_[c_len - 1:c_len], c, revs)
    en = each(lambda c_: jnp.exp(-c_), c)
    khat = each(lambda kk_, c_, lw_: kk_ * jnp.exp(c_ - lw_), kk, c, lw)
    rhat = each(lambda r_, c_: r_ * jnp.exp(c_), r, c)
    bch = each(jnp.multiply, b, en)
    kch = each(jnp.multiply, kd, en)
    et = each(jnp.exp, ctot)
    btil = each(jnp.multiply, bch, et)
    ktil = each(jnp.multiply, kch, et)
    kh4, rh4, v4 = each(stack4, khat), each(stack4, rhat), each(stack4, v)
    sc = each(lambda k_, r_, b_, c_: _dotb(jnp.concatenate([k_, r_], axis=0),
                                           jnp.concatenate([tile4(b_), tile4(c_)], axis=0), NT),
              kh4, rh4, bch, kch)
    a_bk = each(lambda s, m_: s[0:n4, 0:n4] * m_, sc, mask(0))
    a_kk = each(lambda s, m_: s[0:n4, n4:2 * n4] * m_, sc, mask(0))
    b_br = each(lambda s, m_: s[n4:2 * n4, 0:n4] * m_, sc, mask(1))
    b_kr = each(lambda s, m_: s[n4:2 * n4, n4:2 * n4] * m_, sc, mask(1))
    avp = each(lambda a_, b_, v_: _dotb(jnp.concatenate([a_, b_], axis=0), v_), a_kk, b_kr, v4)
    av4 = [z[0:n4] for z in avp]
    p2 = [z[n4:2 * n4] for z in avp]
    ad = each(jnp.multiply, a_bk, mask(2))
    a2 = each(_dotb, ad, ad)
    times_and_square = lambda x_, p_: _dotb(jnp.concatenate([x_, p_], axis=0), p_)
    x = [eye - ad_ for ad_ in ad]
    both = each(times_and_square, x, a2)
    x = each(lambda x_, z: x_ + z[0:n4], x, both)
    a4 = [z[n4:2 * n4] for z in both]
    both = each(times_and_square, x, a4)
    x = each(lambda x_, z: x_ + z[0:n4], x, both)
    a8 = [z[n4:2 * n4] for z in both]
    dinv = each(lambda x_, a8_: x_ + _dotb(x_, a8_), x, a8)
    t1 = each(lambda d_, a_, m_: _dotb(d_, a_ * m_), dinv, a_bk, mask(3))
    b1 = each(lambda d_, t_: d_ - _dotb(t_, d_), dinv, t1)
    t2 = each(lambda b_, a_, m_: _dotb(b_, a_ * m_), b1, a_bk, mask(4))
    tinv = each(lambda b_, t_: b_ - _dotb(t_, b_), b1, t2)
    wu = each(lambda t_, k_, a_: _dotb(t_, jnp.concatenate([k_, a_], axis=1)), tinv, kh4, av4)
    p1 = each(_dotb, b_br, wu)
    q = GROUP
    wy = each(lambda r_, p_: fold4(r_ - p_[:, 0:q]), rh4, p1)
    uy = each(lambda p2_, p_: fold4(p2_ - p_[:, q:2 * q]), p2, p1)
    w1 = each(lambda w_: fold4(w_[:, 0:q]), wu)
    u = each(lambda w_: fold4(w_[:, q:2 * q]), wu)
    om = each(lambda k_, b_, v_, u_, w_: _dotb(
        jnp.concatenate([k_, b_], axis=0),
        jnp.concatenate([jnp.concatenate([v_, jnp.zeros_like(v_)], axis=1),
                         jnp.concatenate([-u_, -w_], axis=1)], axis=0), TN), ktil, btil, v, u, w1)
    o = [bd * z[:, 0:q] for z in om]
    m = each(lambda e_, z: eye * e_ + bd * z[:, q:2 * q], et, om)
    return list(zip(wy, uy, m, o))


def _scan_kernel(lwf, kdf, bf, vf, kkf, rf, lwb, kdb, bb, vb, kkb, rb, msk_ref, tri_ref, hm_ref,
                 yf_ref, yb_ref, st_ref):
    @pl.when(pl.program_id(1) == 0)
    def _():
        st_ref[...] = jnp.zeros_like(st_ref)

    hm = hm_ref[...]
    n_batch = lwf.shape[0]
    n_chunks = lwf.shape[1] // CHUNK
    dirs = ((False, (lwf, kdf, bf, vf, kkf, rf), yf_ref), (True, (lwb, kdb, bb, vb, kkb, rb), yb_ref))
    seqs = [(rev, refs, y_ref, bi) for rev, refs, y_ref in dirs for bi in range(n_batch)]
    order = lambda rev: list(reversed(range(n_chunks))) if rev else list(range(n_chunks))
    sl = lambda rev, k: slice(order(rev)[k] * CHUNK, (order(rev)[k] + 1) * CHUNK)
    blocks = [tuple(ref[bi, sl(rev, k), :] for ref in refs) + (rev,)
              for k in range(n_chunks) for rev, refs, _, bi in seqs]
    terms = _chunk_terms(blocks, msk_ref, tri_ref, hm)
    st = [st_ref[si] for si in range(len(seqs))]
    c_len = CHUNK
    for k in range(n_chunks):
        lhs = [jnp.concatenate(terms[k * len(seqs) + si][0::2], axis=0) for si in range(len(seqs))]
        lp = [_split3(z) for z in lhs]
        sp = [_split3(z) for z in st]
        res = [jnp.dot(jnp.concatenate([l[0], l[1], l[0]], axis=1), jnp.concatenate([s[0], s[0], s[1]], axis=0),
                       preferred_element_type=F32) for l, s in zip(lp, sp)]
        n_l = c_len + GROUP
        for si, (rev, _, y_ref, bi) in enumerate(seqs):
            _, uy, _, o = terms[k * len(seqs) + si]
            y_ref[bi, sl(rev, k), :] = res[si][0:c_len] + uy
            st[si] = res[si][c_len:n_l] + o
    for si in range(len(seqs)):
        st_ref[si] = st[si]


def _scan(feats, n_ctx_tiles):
    lw0, lw1, kd0, kd1, b0, b1, v, kk, r = feats
    nb, s, d = v.shape
    t = TOK_TILE
    nt = s // t
    msk, tri3, hm = _scan_masks()
    fwd = pl.BlockSpec((nb, t, GROUP), lambda q, i: (0, i, q))
    bwd_idx = lambda i: jnp.where(i < n_ctx_tiles, n_ctx_tiles - 1 - i, nt - 1 - (i - n_ctx_tiles))
    bwd = pl.BlockSpec((nb, t, GROUP), lambda q, i: (0, bwd_idx(i), q))
    return pl.pallas_call(
        _scan_kernel,
        grid=(d // GROUP, nt),
        in_specs=[fwd] * 6 + [bwd] * 6 + [pl.BlockSpec(msk.shape, lambda q, i: (0, 0, 0)),
                                          pl.BlockSpec(tri3.shape, lambda q, i: (0, 0, 0)),
                                          pl.BlockSpec(hm.shape, lambda q, i: (0, 0))],
        out_specs=[fwd, bwd],
        out_shape=[jax.ShapeDtypeStruct((nb, s, d), F32)] * 2,
        scratch_shapes=[pltpu.VMEM((2 * nb, GROUP, GROUP), F32)],
        compiler_params=_params("parallel", "arbitrary"),
        name="scan",
    )(lw0, kd0, b0, v, kk, r, lw1, kd1, b1, v, kk, r, msk, tri3, hm)


def _readout_kernel(yf_ref, yb_ref, r_ref, kd0_ref, kd1_ref, v_ref, g_ref, x_ref, mod_ref, vec_ref,
                    wo_ref, hd_ref, hdt_ref, rwt_ref, rb_ref, u_ref,
                    x3_ref, h2_ref, pos_ref, gd_ref, off_ref, *, tiles_per_moe, n_real_tiles):
    i = pl.program_id(0)

    @pl.when(i >= n_real_tiles)
    def _():
        _route_padding(h2_ref, pos_ref, gd_ref)

    def rows_chain(rows, mod, vec):
        headsum = lambda z: _headsum(z, hd_ref[...], hdt_ref[...])
        inv_k = 1.0 / RWKV_HEAD
        y = yf_ref[0, rows, :] + yb_ref[0, rows, :]
        bsum = headsum(r_ref[0, rows, :] * (kd0_ref[0, rows, :] + kd1_ref[0, rows, :]) * vec[2:3])
        yield
        yc = y - headsum(y) * inv_k
        yield
        var = headsum(yc * yc) * inv_k
        yield
        yn = yc * lax.rsqrt(var + GN_EPS) * vec[0:1] + vec[1:2]
        out = (yn + bsum * v_ref[0, rows, :]) * g_ref[0, rows, :]
        x3 = x_ref[rows, :] + mod[2:3] * _dotb(out, wo_ref[...])
        x3_ref[rows, :] = x3
        yield
        return (yield from _route_rows(x3, mod, vec[3:4], rwt_ref, rb_ref, h2_ref, rows))

    @pl.when(i < n_real_tiles)
    def _():
        vec = vec_ref[...]
        mod = mod_ref[0]
        parts = _interleave(*[rows_chain(rows, mod, vec) for rows in _row_splits(x_ref.shape[0], TILE_SPLIT)])
        _route_finish(parts, u_ref, off_ref, pos_ref, gd_ref, i % tiles_per_moe == 0)


def _readout(yf, yb, r, kd0, kd1, v, g, x2d, mod, vec, wo, hd, hdt, rwt, rb, n_ctx_tiles, moe_tile,
             moe_span, tokens_per_batch):
    ntok, d = x2d.shape
    t = TOK_TILE
    n_e = rwt.shape[0]
    tpb = tokens_per_batch // t
    n_real = ntok // t
    n_pad = _padded_tokens(ntok, moe_span)
    route_specs, route_shapes = _route_outputs(n_pad, d, n_e, moe_tile)
    real = lambda i: jnp.minimum(i, n_real - 1)
    seq = pl.BlockSpec((1, t, d), lambda i: (real(i) // tpb, n_ctx_tiles + real(i) % tpb, 0))
    tokspec = pl.BlockSpec((t, d), lambda i: (real(i), 0))
    full = lambda a: pl.BlockSpec(a.shape, lambda i: (0,) * a.ndim)
    u = _route_consts(t)
    kern = functools.partial(_readout_kernel, tiles_per_moe=moe_tile // t, n_real_tiles=n_real)
    return pl.pallas_call(
        kern,
        grid=(n_pad // t,),
        in_specs=[seq] * 7 + [tokspec, pl.BlockSpec((1, 8, d), lambda i: (real(i) // tpb, 0, 0)),
                              full(vec), full(wo), full(hd), full(hdt), full(rwt), full(rb), full(u)],
        out_specs=[tokspec] + route_specs,
        out_shape=[jax.ShapeDtypeStruct((ntok, d), F32)] + route_shapes,
        scratch_shapes=[pltpu.VMEM((n_e, 128), F32)],
        compiler_params=_params("arbitrary"),
        name="readout",
    )(yf, yb, r, kd0, kd1, v, g, x2d, mod, vec, wo, hd, hdt, rwt, rb, u)


def _final_kernel(x_ref, ft_ref, mod_ref, g_ref, o_ref):
    x = x_ref[...] + mod_ref[0][5:6] * jnp.transpose(ft_ref[...])
    o_ref[...] = _rms(x) * g_ref[0:1]


def _final(x2d, ft, mod, gvec, tokens_per_batch):
    ntok, d = x2d.shape
    t = TOK_TILE
    tpb = tokens_per_batch // t
    return pl.pallas_call(
        _final_kernel,
        grid=(ntok // t,),
        in_specs=[pl.BlockSpec((t, d), lambda i: (i, 0)),
                  pl.BlockSpec((d, t), lambda i: (0, i)),
                  pl.BlockSpec((1, 8, d), lambda i: (i // tpb, 0, 0)),
                  pl.BlockSpec((8, d), lambda i: (0, 0))],
        out_specs=pl.BlockSpec((t, d), lambda i: (i, 0)),
        out_shape=jax.ShapeDtypeStruct((ntok, d), F32),
        compiler_params=_params("parallel"),
        name="final",
    )(x2d, ft, mod, gvec)


def _rows8(rows, d):
    n = -(-len(rows) // 8) * 8
    out = jnp.zeros((n, d), F32)
    return out.at[:len(rows)].set(jnp.stack([jnp.asarray(r, F32) for r in rows]))


def _pad_lora(w, total):
    two, r, d = w.shape
    out = jnp.zeros((two, total, d), w.dtype)
    for dd in range(two):
        out = out.at[dd, dd * r:(dd + 1) * r].set(w[dd])
    return out


def kernel(x, c, ctx, c_ctx, norm_g, ada_w, ada_b, pool_w, pool_ls, rwkv_mu, rwkv_w_rkv, rwkv_w0, rwkv_w1, rwkv_w2, rwkv_a0, rwkv_a1, rwkv_a2, rwkv_g1, rwkv_g2, rwkv_k_k, rwkv_k_a, rwkv_r_k, rwkv_ln_g, rwkv_ln_b, rwkv_w_o, moe_router_w, moe_router_b, moe_w_gu, moe_b_gu, moe_w_down, moe_b_down, final_g):
    nb, seq, d = x.shape
    n_ctx = ctx.shape[1]
    depth = norm_g.shape[0]
    n_e = moe_router_w.shape[2]
    t = TOK_TILE
    assert depth == 2 and nb <= 7 and seq % t == 0 and n_ctx % t == 0 and t % GRID_W == 0 and d % GROUP == 0
    n_lat, n_ctxtok = nb * seq, nb * n_ctx
    moe_tile_lat = min(MOE_TILE, n_lat)
    moe_tile_ctx = min(MOE_TILE, n_ctxtok)
    group_lat = min(MOE_GROUP, -(-n_lat // moe_tile_lat))
    group_ctx = min(MOE_GROUP, -(-n_ctxtok // moe_tile_ctx))
    n_ctx_tiles = n_ctx // t
    seq_total = n_ctx + seq

    cond8 = jnp.zeros((8, d), F32).at[:nb].set(c).at[nb].set(c_ctx)
    ada = _ada(cond8, ada_w, ada_b).reshape(depth, 8, N_MOD, d)
    pad8 = lambda m: jnp.pad(m, ((0, 0), (0, 8 - N_MOD), (0, 0)))
    mod_lat = [pad8(ada[i, :nb]) for i in range(depth)]
    mod_ctx = [pad8(jnp.broadcast_to(ada[i, nb][None], (nb, N_MOD, d))) for i in range(depth)]

    wgu_all = _expert_weights_bf16(moe_w_gu, transpose=False)
    wdt_all = _expert_weights_bf16(moe_w_down, transpose=True)

    def expert_params(i):
        return i, wgu_all, moe_b_gu[i].reshape(n_e, 1, -1), wdt_all, moe_b_down[i].T

    def router_params(i):
        return moe_router_w[i].T, moe_router_b[i].reshape(n_e, 1)

    x2d = x.reshape(n_lat, d)
    ctx2d = ctx.reshape(n_ctxtok, d)

    vec0 = _rows8([norm_g[0, 0], norm_g[0, 1], pool_ls[0]], d)
    pw = pool_w[0].astype(BF16)
    rwt0, rb0 = router_params(0)
    ex0 = expert_params(0)
    x1, h2, pos, gd = _pool_layer(x2d, mod_lat[0], vec0, pw, rwt0, rb0, GRID_W, moe_tile_lat,
                                  group_lat * moe_tile_lat, seq)
    ft_lat = _moe(h2, pos, gd, *ex0, group_lat)
    c1, h2c, posc, gdc = _pool_layer(ctx2d, mod_ctx[0], vec0, pw, rwt0, rb0, n_ctx, moe_tile_ctx,
                                     group_ctx * moe_tile_ctx, n_ctx)
    ft_ctx = _moe(h2c, posc, gdc, *ex0, group_ctx)

    vec1 = _rows8([norm_g[1, 0]], d)
    x2, h_all = _respre(x1, ft_lat, mod_lat[0], mod_lat[1], vec1, None, seq, n_ctx_tiles, seq_total)
    _, h_all = _respre(c1, ft_ctx, mod_ctx[0], mod_ctx[1], vec1, h_all, n_ctx, 0, seq_total)

    fv = _rows8([rwkv_mu[0, m] for m in range(6)]
                + [rwkv_w0[0, 0], rwkv_w0[0, 1], rwkv_a0[0, 0], rwkv_a0[0, 1], rwkv_k_k[0], rwkv_k_a[0]], d)
    lora = rwkv_w1.shape[3]
    cat2 = lambda w: jnp.concatenate([w[0, 0], w[0, 1]], axis=1).astype(BF16)
    hd, hdt = _head_indicators(d)
    feats = _features(h_all, fv, rwkv_w_rkv[0].astype(BF16),
                      cat2(rwkv_w1), _pad_lora(rwkv_w2[0], 2 * lora).astype(BF16),
                      cat2(rwkv_a1), _pad_lora(rwkv_a2[0], 2 * rwkv_a1.shape[3]).astype(BF16),
                      rwkv_g1[0].astype(BF16), rwkv_g2[0].astype(BF16), hd, hdt, n_ctx_tiles)
    lw0, lw1, kd0, kd1, b0, b1, v, kk, r, g = feats
    yf, yb = _scan((lw0, lw1, kd0, kd1, b0, b1, v, kk, r), n_ctx_tiles)

    vec_ro = _rows8([rwkv_ln_g[0], rwkv_ln_b[0], rwkv_r_k[0].reshape(-1), norm_g[1, 1]], d)
    rwt1, rb1 = router_params(1)
    x3, h2, pos, gd = _readout(yf, yb, r, kd0, kd1, v, g, x2, mod_lat[1], vec_ro,
                               rwkv_w_o[0].astype(BF16), hd, hdt, rwt1, rb1, n_ctx_tiles, moe_tile_lat,
                               group_lat * moe_tile_lat, seq)
    ft = _moe(h2, pos, gd, *expert_params(1), group_lat)
    out = _final(x3, ft, mod_lat[1], _rows8([final_g], d), seq)
    return out.reshape(nb, seq, d)
```

```python
import functools

import numpy as np
import jax
import jax.numpy as jnp
from jax import lax
from jax.experimental import pallas as pl
from jax.experimental.pallas import tpu as pltpu

F32 = jnp.float32
BF16 = jnp.bfloat16
NN = (((1,), (0,)), ((), ()))
NT = (((1,), (1,)), ((), ()))
TN = (((0,), (0,)), ((), ()))

N_MOD = 6
NORM_EPS = 1e-6
GRID_W = 64
POOL_WINDOWS = (2, 4, 8, 16)
RWKV_HEAD = 64
GN_EPS = 64e-5
TOP_K = 4
SWIGLU_ALPHA = 1.702
SWIGLU_LIMIT = 7.0

TOK_TILE = 256
TILE_SPLIT = 2
MOE_TILE = 1792
MOE_GROUP = 4
MOE_ROWS = 256
MOE_CHUNK_UNITS = 8
CHUNK = 64
GROUP_HEADS = 2
GROUP = GROUP_HEADS * RWKV_HEAD
HEAD_PAD = 128
VMEM_LIMIT_BYTES = 56 * 1024 * 1024


def _dotx(a, b, dims=NN):
    return lax.dot_general(a, b, dims, precision=lax.Precision.HIGHEST, preferred_element_type=F32)


def _dotb(a, b, dims=NN):
    return lax.dot_general(a.astype(BF16), b.astype(BF16), dims, preferred_element_type=F32)


def _params(*sem):
    return pltpu.CompilerParams(dimension_semantics=sem, vmem_limit_bytes=VMEM_LIMIT_BYTES)


def _padded_tokens(ntok, tile):
    return -(-ntok // tile) * tile


def _rms(x):
    return x * lax.rsqrt(jnp.mean(x * x, axis=-1, keepdims=True) + NORM_EPS)


def _sigmoid(x):
    return 1.0 / (1.0 + jnp.exp(-x))


def _ada_kernel(c_ref, w_ref, b_ref, o_ref):
    c = c_ref[...]
    o_ref[0] = _dotx(c * _sigmoid(c), w_ref[0]) + b_ref[0]


def _ada(cond8, ada_w, ada_b):
    depth, d, nd = ada_w.shape
    return pl.pallas_call(
        _ada_kernel,
        grid=(depth, nd // d),
        in_specs=[pl.BlockSpec((8, d), lambda i, n: (0, 0)),
                  pl.BlockSpec((1, d, d), lambda i, n: (i, 0, n)),
                  pl.BlockSpec((1, 1, d), lambda i, n: (i, 0, n))],
        out_specs=pl.BlockSpec((1, 8, d), lambda i, n: (i, 0, n)),
        out_shape=jax.ShapeDtypeStruct((depth, 8, nd), F32),
        compiler_params=_params("parallel", "parallel"),
        name="ada",
    )(cond8, ada_w, ada_b.reshape(depth, 1, nd))


def _route_padding(h2_ref, pos_ref, gd_ref):
    h2_ref[...] = jnp.zeros_like(h2_ref)
    pos_ref[...] = jnp.full(pos_ref.shape, -1, jnp.int32)
    gd_ref[...] = jnp.zeros_like(gd_ref)


def _interleave(*gens):
    out = [None] * len(gens)
    live = dict(enumerate(gens))
    while live:
        for k, g in list(live.items()):
            try:
                next(g)
            except StopIteration as stop:
                out[k] = stop.value
                del live[k]
    return out


def _row_splits(n_rows, n_split):
    step = n_rows // n_split
    return [slice(k * step, (k + 1) * step) for k in range(n_split)]


def _route_rows(x_new, mod, g2, rwt_ref, rb_ref, h2_ref, rows):
    h2 = _rms(x_new) * g2 * (1.0 + mod[4:5]) + mod[3:4]
    h2_ref[rows, :] = h2.astype(BF16)
    yield
    logits = _dotx(rwt_ref[...], h2, NT) + rb_ref[...]
    yield
    n_e = logits.shape[0]
    eio = lax.broadcasted_iota(jnp.int32, logits.shape, 0).astype(F32)
    live = logits
    sels, vals = [], []
    for _ in range(TOP_K):
        m = jnp.max(live, axis=0, keepdims=True)
        idx = jnp.min(jnp.where(live == m, eio, float(n_e)), axis=0, keepdims=True)
        sel = eio == idx
        sels.append(sel)
        vals.append(m)
        live = jnp.where(sel, -jnp.inf, live)
        yield
    exps = [jnp.exp(v - vals[0]) for v in vals]
    inv = 1.0 / (exps[0] + exps[1] + exps[2] + exps[3])
    gd = jnp.zeros_like(logits)
    maskf = jnp.zeros_like(logits)
    for k in range(TOP_K):
        gd = jnp.where(sels[k], exps[k] * inv, gd)
        maskf = jnp.where(sels[k], 1.0, maskf)
    return maskf, gd


def _route_finish(parts, u_ref, off_ref, pos_ref, gd_ref, reset):
    @pl.when(reset)
    def _():
        off_ref[...] = jnp.zeros_like(off_ref)

    off = off_ref[:, 0:1]
    lane0 = 0
    for maskf, gd in parts:
        n = maskf.shape[1]
        incl = _dotb(maskf, u_ref[0:n, 0:n])
        pos_ref[0, :, lane0:lane0 + n] = jnp.where(maskf > 0.5, off + incl - 1.0, -1.0).astype(jnp.int32)
        gd_ref[0, :, lane0:lane0 + n] = gd
        off = off + jnp.sum(maskf, axis=1, keepdims=True)
        lane0 += n
    off_ref[...] = jnp.broadcast_to(off, off_ref.shape)


def _route_outputs(n_pad, d, n_e, moe_tile):
    t = TOK_TILE
    tpm = moe_tile // t
    nt = n_pad // moe_tile
    spec = pl.BlockSpec((1, n_e, t), lambda i: (i // tpm, 0, i % tpm))
    shapes = [jax.ShapeDtypeStruct((n_pad, d), BF16),
              jax.ShapeDtypeStruct((nt, n_e, moe_tile), jnp.int32),
              jax.ShapeDtypeStruct((nt, n_e, moe_tile), F32)]
    return [pl.BlockSpec((t, d), lambda i: (i, 0)), spec, spec], shapes


def _route_consts(tile):
    s = np.arange(tile)
    return jnp.asarray((s[:, None] <= s[None, :]).astype(np.float32), BF16)


def _pool_kernel(x_ref, mod_ref, vec_ref, pm_ref, pw_ref, rwt_ref, rb_ref, u_ref,
                 x1_ref, h2_ref, pos_ref, gd_ref, off_ref, *, tiles_per_moe, n_real_tiles, n_split):
    i = pl.program_id(0)

    @pl.when(i >= n_real_tiles)
    def _():
        _route_padding(h2_ref, pos_ref, gd_ref)

    def rows_chain(rows, mod, vec):
        x = x_ref[rows, :]
        h = _rms(x) * vec[0:1] * (1.0 + mod[1:2]) + mod[0:1]
        yield
        gw = pw_ref.shape[1]
        ys = []
        for g in range(len(POOL_WINDOWS)):
            d = _dot3(pm_ref[g, rows, rows], h[:, g * gw:(g + 1) * gw])
            yield
            ys.append(_dotb(d, pw_ref[g]))
            yield
        x1 = x + mod[2:3] * (jnp.concatenate(ys, axis=1) * vec[2:3])
        x1_ref[rows, :] = x1
        return (yield from _route_rows(x1, mod, vec[1:2], rwt_ref, rb_ref, h2_ref, rows))

    @pl.when(i < n_real_tiles)
    def _():
        mod = mod_ref[0]
        vec = vec_ref[...]
        parts = _interleave(*[rows_chain(rows, mod, vec) for rows in _row_splits(x_ref.shape[0], n_split)])
        _route_finish(parts, u_ref, off_ref, pos_ref, gd_ref, i % tiles_per_moe == 0)


def _pool_matrices(tile, row_len):
    p = np.arange(tile)
    pp = p % row_len
    out = []
    for win in POOL_WINDOWS:
        lo = np.clip(pp - win // 2, 0, row_len - 1)
        hi = np.clip(pp + win // 2 - 1, 0, row_len - 1)
        cnt = (hi - lo + 1).astype(np.float64)
        same = (p[:, None] // row_len) == (p[None, :] // row_len)
        inwin = same & (pp[None, :] >= lo[:, None]) & (pp[None, :] <= hi[:, None])
        out.append(inwin / cnt[:, None] - np.eye(tile))
    return jnp.asarray(np.stack(out).astype(np.float32))


def _pool_layer(x2d, mod, vec, pool_w, rwt, rb, row_len, moe_tile, moe_span, tokens_per_batch):
    ntok, d = x2d.shape
    t = TOK_TILE
    n_e = rwt.shape[0]
    tiles_per_batch = tokens_per_batch // t
    n_real = ntok // t
    n_pad = _padded_tokens(ntok, moe_span)
    route_specs, route_shapes = _route_outputs(n_pad, d, n_e, moe_tile)
    pm = _pool_matrices(t, row_len)
    n_split = TILE_SPLIT if (t // TILE_SPLIT) % row_len == 0 else 1
    kern = functools.partial(_pool_kernel, tiles_per_moe=moe_tile // t, n_real_tiles=n_real, n_split=n_split)
    const2 = lambda i: (0, 0)
    const3 = lambda i: (0, 0, 0)
    real = lambda i: jnp.minimum(i, n_real - 1)
    return pl.pallas_call(
        kern,
        grid=(n_pad // t,),
        in_specs=[pl.BlockSpec((t, d), lambda i: (real(i), 0)),
                  pl.BlockSpec((1, 8, d), lambda i: (real(i) // tiles_per_batch, 0, 0)),
                  pl.BlockSpec((8, d), const2),
                  pl.BlockSpec(pm.shape, const3),
                  pl.BlockSpec(pool_w.shape, const3),
                  pl.BlockSpec(rwt.shape, const2),
                  pl.BlockSpec(rb.shape, const2),
                  pl.BlockSpec((t, t), const2)],
        out_specs=[pl.BlockSpec((t, d), lambda i: (real(i), 0))] + route_specs,
        out_shape=[jax.ShapeDtypeStruct((ntok, d), F32)] + route_shapes,
        scratch_shapes=[pltpu.VMEM((n_e, 128), F32)],
        compiler_params=_params("arbitrary"),
        name="pool",
    )(x2d, mod, vec, pm, pool_w, rwt, rb, _route_consts(t))


def _cast_kernel(w_ref, o_ref, *, transpose):
    w = w_ref[0, 0].astype(BF16)
    o_ref[0, 0] = jnp.transpose(w) if transpose else w


def _expert_weights_bf16(w, transpose):
    n_l, n_e, r, c = w.shape
    blk = min(r, c)
    out_idx = (lambda l, e, i, j: (l, e, j, i)) if transpose else (lambda l, e, i, j: (l, e, i, j))
    return pl.pallas_call(
        functools.partial(_cast_kernel, transpose=transpose),
        grid=(n_l, n_e, r // blk, c // blk),
        in_specs=[pl.BlockSpec((1, 1, blk, blk), lambda l, e, i, j: (l, e, i, j))],
        out_specs=pl.BlockSpec((1, 1, blk, blk), out_idx),
        out_shape=jax.ShapeDtypeStruct((n_l, n_e, c, r) if transpose else w.shape, BF16),
        compiler_params=_params("parallel", "parallel", "parallel", "parallel"),
        name="wprep",
    )(w)


def _one_hot_rows(prow, r0, n_rows):
    rows = lax.broadcasted_iota(jnp.int32, (n_rows, prow.shape[1]), 0) + r0
    return jnp.where(prow == rows, 1.0, 0.0).astype(BF16)


def _expert_kernel(grp_ref, sub_ref, e_ref, r0_ref, half_ref,
                   pos_ref, gd_ref, x_ref, wgu_ref, bgu_ref, wdt_ref, o_ref):
    b = pl.program_id(0)
    f = wdt_ref.shape[3]
    n_rows = o_ref.shape[1] * o_ref.shape[3]
    ts = pos_ref.shape[2]
    e = e_ref[b]
    sub = sub_ref[b]
    prow = pos_ref[sub, pl.ds(e, 1), :]
    grow = gd_ref[sub, pl.ds(e, 1), :]
    g_hi = grow.astype(BF16)
    g_mid = (grow - g_hi.astype(F32)).astype(BF16)
    g_lo = (grow - g_hi.astype(F32) - g_mid.astype(F32)).astype(BF16)
    g3 = jnp.concatenate([g_hi, g_mid, g_lo, jnp.zeros((5, ts), BF16)], axis=0)

    def rows_out(n):
        oh = _one_hot_rows(prow, r0_ref[b], n)
        x = x_ref[pl.ds(pl.multiple_of(sub * ts, ts), ts), :]
        xg = jnp.dot(oh, x, preferred_element_type=F32).astype(BF16)
        gu = jnp.dot(xg, wgu_ref[0, 0], preferred_element_type=F32) + bgu_ref[0]
        gate = jnp.minimum(gu[:, :f], SWIGLU_LIMIT)
        up = jnp.clip(gu[:, f:], -SWIGLU_LIMIT, SWIGLU_LIMIT)
        act = (up + 1.0) * (gate * _sigmoid(SWIGLU_ALPHA * gate))
        yt = lax.dot_general(wdt_ref[0, 0], act.astype(BF16), NT, preferred_element_type=F32)
        g8 = lax.dot_general(g3, oh, NT, preferred_element_type=F32)
        return (yt * (g8[0:1] + g8[1:2] + g8[2:3])).astype(BF16)

    @pl.when(half_ref[b] == 0)
    def _():
        yt = rows_out(n_rows)
        o_ref[0, 0] = yt[:, 0:n_rows // 2]
        o_ref[0, 1] = yt[:, n_rows // 2:n_rows]

    @pl.when(half_ref[b] == 1)
    def _():
        o_ref[0, 0] = rows_out(n_rows // 2)
        o_ref[0, 1] = jnp.zeros(o_ref.shape[2:], BF16)


def _combine_kernel(tile_ref, first_ref, blk_ref, unit_ref, e_ref, r0_ref, pos_ref, gd_ref, *rest):
    yt_refs, bdt_ref, o_ref = rest[:MOE_CHUNK_UNITS], rest[-2], rest[-1]
    c = pl.program_id(0)
    n_rows = yt_refs[0].shape[3]

    @pl.when(first_ref[c] == 1)
    def _():
        o_ref[...] = _dot3(bdt_ref[...], gd_ref[0])

    slot = lambda q: c * MOE_CHUNK_UNITS + q
    oh = jnp.concatenate([_one_hot_rows(pos_ref[0, pl.ds(e_ref[slot(q)], 1), :], r0_ref[slot(q)], n_rows)
                          for q in range(MOE_CHUNK_UNITS)], axis=0)
    yt = jnp.concatenate([ref[0, 0] for ref in yt_refs], axis=1)
    o_ref[...] += jnp.dot(yt, oh, preferred_element_type=F32)


def _moe_tables(pos, group):
    nt, n_e, ts = pos.shape
    ng = nt // group
    rb, cb = MOE_ROWS, MOE_CHUNK_UNITS
    i32 = lambda a: a.astype(jnp.int32)
    cnt = jnp.sum(i32(pos >= 0), axis=-1)
    nb = (cnt + rb - 1) // rb
    max_tile_blocks = ts * TOP_K // rb + n_e
    order = lambda a: a.reshape(ng, group, n_e).transpose(0, 2, 1).reshape(-1)
    flat = order(nb)
    cum = jnp.cumsum(flat)
    start = cum - flat
    n_blocks = cum[-1]
    b = jnp.minimum(jnp.arange(nt * max_tile_blocks, dtype=jnp.int32), n_blocks - 1)
    count_le = lambda sorted_, q: jnp.sum(i32(sorted_[None, :] <= q[:, None]), axis=1)
    idx = jnp.minimum(count_le(cum, b), flat.shape[0] - 1)
    r0 = (b - start[idx]) * rb
    expert_tables = (i32(idx // (n_e * group)), i32(idx % group), i32((idx // group) % n_e), i32(r0),
                     i32(order(cnt)[idx] - r0 <= rb // 2))
    ru = rb // 2
    nu = (cnt + ru - 1) // ru
    ce = jnp.cumsum(nu, axis=1)
    nch = (ce[:, -1] + cb - 1) // cb
    cch = jnp.cumsum(nch)
    n_chunks = cch[-1]
    max_chunks = nt * (-(-(ts * TOP_K // ru + n_e) // cb))
    c = jnp.minimum(jnp.arange(max_chunks, dtype=jnp.int32), n_chunks - 1)
    tile = jnp.minimum(count_le(cch, c), nt - 1)
    lc = c - (cch - nch)[tile]
    slot = (lc[:, None] * cb + jnp.arange(cb, dtype=jnp.int32)[None, :]).reshape(-1)
    tile_s = jnp.repeat(tile, cb)
    used = slot < ce[tile_s, -1]
    slot = jnp.where(used, slot, 0)
    ce_s = ce[tile_s]
    e_s = jnp.minimum(jnp.sum(i32(slot[:, None] >= ce_s), axis=1), n_e - 1)
    local = slot - jnp.take_along_axis(ce_s - nu[tile_s], e_s[:, None], axis=1)[:, 0]
    blk = start.reshape(ng, n_e, group)[tile_s // group, e_s, tile_s % group] + local // 2
    combine_tables = (i32(tile), i32(lc == 0), i32(blk), i32(local % 2), i32(e_s),
                      i32(jnp.where(used, local * ru, 1 << 20)))
    return expert_tables, i32(n_blocks), combine_tables, i32(n_chunks)


def _moe(h2, pos, gd, layer, wgu, bgu, wdt, bdt, group):
    ntok, d = h2.shape
    nt, n_e, ts = pos.shape
    f2 = wgu.shape[3]
    rb, cb = MOE_ROWS, MOE_CHUNK_UNITS
    expert_tables, n_blocks, combine_tables, n_chunks = _moe_tables(pos, group)
    max_blocks = expert_tables[0].shape[0]
    yt = pl.pallas_call(
        _expert_kernel,
        grid_spec=pltpu.PrefetchScalarGridSpec(
            num_scalar_prefetch=5,
            grid=(n_blocks,),
            in_specs=[pl.BlockSpec((group, n_e, ts), lambda b, gp, sb, ee, rr, hf: (gp[b], 0, 0)),
                      pl.BlockSpec((group, n_e, ts), lambda b, gp, sb, ee, rr, hf: (gp[b], 0, 0)),
                      pl.BlockSpec((group * ts, d), lambda b, gp, sb, ee, rr, hf: (gp[b], 0),
                                   pipeline_mode=pl.Buffered(1)),
                      pl.BlockSpec((1, 1, d, f2), lambda b, gp, sb, ee, rr, hf: (layer, ee[b], 0, 0)),
                      pl.BlockSpec((1, 1, f2), lambda b, gp, sb, ee, rr, hf: (ee[b], 0, 0)),
                      pl.BlockSpec((1, 1, d, f2 // 2), lambda b, gp, sb, ee, rr, hf: (layer, ee[b], 0, 0))],
            out_specs=pl.BlockSpec((1, 2, d, rb // 2), lambda b, gp, sb, ee, rr, hf: (b, 0, 0, 0)),
        ),
        out_shape=jax.ShapeDtypeStruct((max_blocks, 2, d, rb // 2), BF16),
        compiler_params=_params("arbitrary"),
        name="expert",
    )(*expert_tables, pos, gd, h2, wgu, bgu, wdt)
    yt_spec = lambda q: pl.BlockSpec((1, 1, d, rb // 2),
                                     lambda c, tl, fs, bk, un, ee, rr: (bk[c * cb + q], un[c * cb + q], 0, 0))
    return pl.pallas_call(
        _combine_kernel,
        grid_spec=pltpu.PrefetchScalarGridSpec(
            num_scalar_prefetch=6,
            grid=(n_chunks,),
            in_specs=[pl.BlockSpec((1, n_e, ts), lambda c, tl, fs, bk, un, ee, rr: (tl[c], 0, 0)),
                      pl.BlockSpec((1, n_e, ts), lambda c, tl, fs, bk, un, ee, rr: (tl[c], 0, 0))]
                     + [yt_spec(q) for q in range(cb)]
                     + [pl.BlockSpec((d, n_e), lambda c, tl, fs, bk, un, ee, rr: (0, 0))],
            out_specs=pl.BlockSpec((d, ts), lambda c, tl, fs, bk, un, ee, rr: (0, tl[c]),
                                   pipeline_mode=pl.Buffered(1)),
        ),
        out_shape=jax.ShapeDtypeStruct((d, ntok), F32),
        compiler_params=_params("arbitrary"),
        name="combine",
    )(*combine_tables, pos, gd, *([yt] * cb), bdt)


def _respre_kernel(x_ref, ft_ref, modp_ref, modc_ref, vec_ref, *rest):
    x2_ref, h_ref = rest[-2], rest[-1]
    x2 = x_ref[...] + modp_ref[0][5:6] * jnp.transpose(ft_ref[...])
    x2_ref[...] = x2
    modc = modc_ref[0]
    h_ref[0] = _rms(x2) * vec_ref[0:1] * (1.0 + modc[1:2]) + modc[0:1]


def _respre(x2d, ft, modp, modc, vec, h_all, tokens_per_batch, tile_offset, seq_total):
    ntok, d = x2d.shape
    t = TOK_TILE
    nb = modp.shape[0]
    tpb = tokens_per_batch // t
    in_specs = [pl.BlockSpec((t, d), lambda i: (i, 0)),
                pl.BlockSpec((d, t), lambda i: (0, i)),
                pl.BlockSpec((1, 8, d), lambda i: (i // tpb, 0, 0)),
                pl.BlockSpec((1, 8, d), lambda i: (i // tpb, 0, 0)),
                pl.BlockSpec((8, d), lambda i: (0, 0))]
    args = [x2d, ft, modp, modc, vec]
    aliases = {}
    if h_all is not None:
        in_specs.append(pl.BlockSpec(memory_space=pl.ANY))
        args.append(h_all)
        aliases = {5: 1}
    return pl.pallas_call(
        _respre_kernel,
        grid=(ntok // t,),
        in_specs=in_specs,
        out_specs=[pl.BlockSpec((t, d), lambda i: (i, 0)),
                   pl.BlockSpec((1, t, d), lambda i: (i // tpb, tile_offset + i % tpb, 0))],
        out_shape=[jax.ShapeDtypeStruct((ntok, d), F32),
                   jax.ShapeDtypeStruct((nb, seq_total, d), F32)],
        input_output_aliases=aliases,
        compiler_params=_params("parallel"),
        name="respre",
    )(*args)


def _feat_kernel(h_ref, hp_ref, hn_ref, fv_ref, wrkv_ref, w1_ref, w2_ref, a1_ref, a2_ref, g1_ref, g2_ref,
                 hd_ref, hdt_ref,
                 lw0_ref, lw1_ref, kd0_ref, kd1_ref, b0_ref, b1_ref, v_ref, kk_ref, r_ref, g_ref,
                 *, n_ctx_tiles, n_tiles):
    i = pl.program_id(1)
    h = h_ref[0]
    t = h.shape[0]
    fv = fv_ref[...]
    first = jnp.logical_or(i == 0, i == n_ctx_tiles)
    last = jnp.logical_or(i == n_ctx_tiles - 1, i == n_tiles - 1)
    prow = jnp.where(first, 0.0, hp_ref[0][7:8])
    nrow = jnp.where(last, 0.0, hn_ref[0][0:1])
    rio = lax.broadcasted_iota(jnp.int32, h.shape, 0)
    hdn = jnp.where(rio == 0, prow, pltpu.roll(h, 1, 0))
    hup = jnp.where(rio == t - 1, nrow, pltpu.roll(h, t - 1, 0))
    xx = 0.5 * (hdn + hup) - h

    mix = lambda m: h + xx * fv[m:m + 1]
    r = _dotb(mix(0), wrkv_ref[0])
    k = _dotb(mix(2), wrkv_ref[1])
    v = _dotb(mix(3), wrkv_ref[2])
    tw = jnp.tanh(_dotb(mix(1), w1_ref[...]))
    la = _dotb(mix(4), a1_ref[...])
    g = _dotb(_sigmoid(_dotb(mix(5), g1_ref[...])), g2_ref[...])

    kkraw = k * fv[10:11]
    ss = _headsum(kkraw * kkraw, hd_ref[...], hdt_ref[...])
    kk = kkraw / jnp.maximum(jnp.sqrt(ss), 1e-12)
    k_a = fv[11:12]
    decay_scale = float(np.exp(-0.5))
    for d, (lw_ref, kd_ref, b_ref) in enumerate(((lw0_ref, kd0_ref, b0_ref), (lw1_ref, kd1_ref, b1_ref))):
        zw = fv[6 + d:7 + d] + _dotb(tw, w2_ref[d])
        lw_ref[0] = -decay_scale * _sigmoid(zw)
        icl = _sigmoid(fv[8 + d:9 + d] + _dotb(la, a2_ref[d]))
        kd_ref[0] = k * (1.0 + (icl - 1.0) * k_a)
        b_ref[0] = kk * icl
    v_ref[0] = v
    kk_ref[0] = kk
    r_ref[0] = r
    g_ref[0] = g


def _head_indicators(d):
    hd = np.zeros((d, HEAD_PAD), np.float32)
    hd[np.arange(d), np.arange(d) // RWKV_HEAD] = 1.0
    return jnp.asarray(hd), jnp.asarray(hd.T.copy())


def _features(h_all, fv, wrkv, w1c, w2p, a1c, a2p, g1, g2, hd, hdt, n_ctx_tiles):
    nb, s, d = h_all.shape
    t = TOK_TILE
    nt = s // t
    r8 = t // 8
    kern = functools.partial(_feat_kernel, n_ctx_tiles=n_ctx_tiles, n_tiles=nt)
    full = lambda a: pl.BlockSpec(a.shape, lambda b, i: (0,) * a.ndim)
    tok = pl.BlockSpec((1, t, d), lambda b, i: (b, i, 0))
    return pl.pallas_call(
        kern,
        grid=(nb, nt),
        in_specs=[tok,
                  pl.BlockSpec((1, 8, d), lambda b, i: (b, jnp.maximum(i * r8 - 1, 0), 0)),
                  pl.BlockSpec((1, 8, d), lambda b, i: (b, jnp.minimum((i + 1) * r8, s // 8 - 1), 0)),
                  full(fv), full(wrkv), full(w1c), full(w2p), full(a1c), full(a2p), full(g1), full(g2),
                  full(hd), full(hdt)],
        out_specs=[tok] * 10,
        out_shape=[jax.ShapeDtypeStruct((nb, s, d), F32)] * 10,
        compiler_params=_params("parallel", "parallel"),
        name="feat",
    )(h_all, h_all, h_all, fv, wrkv, w1c, w2p, a1c, a2p, g1, g2, hd, hdt)


def _scan_masks():
    n = GROUP_HEADS * CHUNK
    rr = np.arange(n)[:, None]
    cc = np.arange(n)[None, :]
    bd = (rr // CHUNK) == (cc // CHUNK)
    r_, c_ = rr % CHUNK, cc % CHUNK
    out = []
    for rev in (False, True):
        strict = bd & ((c_ > r_) if rev else (c_ < r_))
        incl = bd & ((c_ >= r_) if rev else (c_ <= r_))
        d16 = strict & ((r_ // 16) == (c_ // 16))
        l1 = strict & ((r_ // 32) == (c_ // 32)) & ((r_ // 16) != (c_ // 16))
        l2 = strict & ((r_ // 32) != (c_ // 32))
        out.append([strict, incl, d16, l1, l2])
    masks = [out[0][m] for m in range(5)] + [out[1][m] for m in range(5)] + [bd, rr == cc]
    hm = np.zeros((8, GROUP), np.float32)
    for j in range(GROUP_HEADS):
        hm[j, j * RWKV_HEAD:(j + 1) * RWKV_HEAD] = 1.0
    tri3 = np.stack([np.tile(out[rev][1][0:CHUNK, 0:CHUNK], (1, 3)) for rev in (0, 1)]).astype(np.float32)
    return jnp.asarray(np.stack(masks).astype(np.float32)), jnp.asarray(tri3, BF16), jnp.asarray(hm)


def _split3(a):
    a1 = a.astype(BF16)
    r1 = a - a1.astype(F32)
    a2 = r1.astype(BF16)
    return a1, a2, (r1 - a2.astype(F32)).astype(BF16)


def _dot3(a, b, dims=NN):
    a1, a2, _ = _split3(a)
    b1, b2, _ = _split3(b)
    mm = lambda p, q: lax.dot_general(p, q, dims, preferred_element_type=F32)
    return mm(a1, b1) + (mm(a1, b2) + mm(a2, b1))


def _dot01(a, ind):
    ib = ind.astype(BF16)
    a1, a2, _ = _split3(a)
    mm = lambda p: jnp.dot(p, ib, preferred_element_type=F32)
    return mm(a1) + mm(a2)


def _headsum(z, hd, hdt):
    return _dot01(_dot01(z, hd), hdt)


def _chunk_terms(blocks, msk_ref, tri_ref, hm):
    c_len = blocks[0][0].shape[0]
    n4 = GROUP_HEADS * c_len
    revs = [blk[6] for blk in blocks]
    mask = lambda k: [msk_ref[(5 if rev else 0) + k] for rev in revs]
    bd, eye = msk_ref[10], msk_ref[11]
    each = lambda f, *cols: [f(*args) for args in zip(*cols)]
    stack4 = lambda z: jnp.concatenate([z * hm[j:j + 1] for j in range(GROUP_HEADS)], axis=0)
    tile4 = lambda z: jnp.concatenate([z] * GROUP_HEADS, axis=0)
    fold4 = lambda z: sum(z[j * c_len:(j + 1) * c_len] for j in range(GROUP_HEADS))
    lw, kd, b, v, kk, r = [[blk[k] for blk in blocks] for k in range(6)]

    tri3 = [tri_ref[1 if rev else 0] for rev in revs]
    parts = each(lambda z: jnp.concatenate(_split3(z), axis=0), lw)
    c = each(lambda t, p: jnp.dot(t, p, preferred_element_type=F32), tri3, parts)
    ctot = each(lambda c_, rev: c_[0:1] if rev else c_[c_len - 1:c_len], c, revs)
    en = each(lambda c_: jnp.exp(-c_), c)
    khat = each(lambda kk_, c_, lw_: kk_ * jnp.exp(c_ - lw_), kk, c, lw)
    rhat = each(lambda r_, c_: r_ * jnp.exp(c_), r, c)
    bch = each(jnp.multiply, b, en)
    kch = each(jnp.multiply, kd, en)
    et = each(jnp.exp, ctot)
    btil = each(jnp.multiply, bch, et)
    ktil = each(jnp.multiply, kch, et)
    kh4, rh4, v4 = each(stack4, khat), each(stack4, rhat), each(stack4, v)
    sc = each(lambda k_, r_, b_, c_: _dotb(jnp.concatenate([k_, r_], axis=0),
                                           jnp.concatenate([tile4(b_), tile4(c_)], axis=0), NT),
              kh4, rh4, bch, kch)
    a_bk = each(lambda s, m_: s[0:n4, 0:n4] * m_, sc, mask(0))
    a_kk = each(lambda s, m_: s[0:n4, n4:2 * n4] * m_, sc, mask(0))
    b_br = each(lambda s, m_: s[n4:2 * n4, 0:n4] * m_, sc, mask(1))
    b_kr = each(lambda s, m_: s[n4:2 * n4, n4:2 * n4] * m_, sc, mask(1))
    avp = each(lambda a_, b_, v_: _dotb(jnp.concatenate([a_, b_], axis=0), v_), a_kk, b_kr, v4)
    av4 = [z[0:n4] for z in avp]
    p2 = [z[n4:2 * n4] for z in avp]
    ad = each(jnp.multiply, a_bk, mask(2))
    a2 = each(_dotb, ad, ad)
    times_and_square = lambda x_, p_: _dotb(jnp.concatenate([x_, p_], axis=0), p_)
    x = [eye - ad_ for ad_ in ad]
    both = each(times_and_square, x, a2)
    x = each(lambda x_, z: x_ + z[0:n4], x, both)
    a4 = [z[n4:2 * n4] for z in both]
    both = each(times_and_square, x, a4)
    x = each(lambda x_, z: x_ + z[0:n4], x, both)
    a8 = [z[n4:2 * n4] for z in both]
    dinv = each(lambda x_, a8_: x_ + _dotb(x_, a8_), x, a8)
    t1 = each(lambda d_, a_, m_: _dotb(d_, a_ * m_), dinv, a_bk, mask(3))
    b1 = each(lambda d_, t_: d_ - _dotb(t_, d_), dinv, t1)
    t2 = each(lambda b_, a_, m_: _dotb(b_, a_ * m_), b1, a_bk, mask(4))
    tinv = each(lambda b_, t_: b_ - _dotb(t_, b_), b1, t2)
    wu = each(lambda t_, k_, a_: _dotb(t_, jnp.concatenate([k_, a_], axis=1)), tinv, kh4, av4)
    p1 = each(_dotb, b_br, wu)
    q = GROUP
    wy = each(lambda r_, p_: fold4(r_ - p_[:, 0:q]), rh4, p1)
    uy = each(lambda p2_, p_: fold4(p2_ - p_[:, q:2 * q]), p2, p1)
    w1 = each(lambda w_: fold4(w_[:, 0:q]), wu)
    u = each(lambda w_: fold4(w_[:, q:2 * q]), wu)
    om = each(lambda k_, b_, v_, u_, w_: _dotb(
        jnp.concatenate([k_, b_], axis=0),
        jnp.concatenate([jnp.concatenate([v_, jnp.zeros_like(v_)], axis=1),
                         jnp.concatenate([-u_, -w_], axis=1)], axis=0), TN), ktil, btil, v, u, w1)
    o = [bd * z[:, 0:q] for z in om]
    m = each(lambda e_, z: eye * e_ + bd * z[:, q:2 * q], et, om)
    return list(zip(wy, uy, m, o))


def _scan_kernel(lwf, kdf, bf, vf, kkf, rf, lwb, kdb, bb, vb, kkb, rb, msk_ref, tri_ref, hm_ref,
                 yf_ref, yb_ref, st_ref):
    @pl.when(pl.program_id(1) == 0)
    def _():
        st_ref[...] = jnp.zeros_like(st_ref)

    hm = hm_ref[...]
    n_batch = lwf.shape[0]
    n_chunks = lwf.shape[1] // CHUNK
    dirs = ((False, (lwf, kdf, bf, vf, kkf, rf), yf_ref), (True, (lwb, kdb, bb, vb, kkb, rb), yb_ref))
    seqs = [(rev, refs, y_ref, bi) for rev, refs, y_ref in dirs for bi in range(n_batch)]
    order = lambda rev: list(reversed(range(n_chunks))) if rev else list(range(n_chunks))
    sl = lambda rev, k: slice(order(rev)[k] * CHUNK, (order(rev)[k] + 1) * CHUNK)
    blocks = [tuple(ref[bi, sl(rev, k), :] for ref in refs) + (rev,)
              for k in range(n_chunks) for rev, refs, _, bi in seqs]
    terms = _chunk_terms(blocks, msk_ref, tri_ref, hm)
    st = [st_ref[si] for si in range(len(seqs))]
    c_len = CHUNK
    for k in range(n_chunks):
        lhs = [jnp.concatenate(terms[k * len(seqs) + si][0::2], axis=0) for si in range(len(seqs))]
        lp = [_split3(z) for z in lhs]
        sp = [_split3(z) for z in st]
        res = [jnp.dot(jnp.concatenate([l[0], l[1], l[0]], axis=1), jnp.concatenate([s[0], s[0], s[1]], axis=0),
                       preferred_element_type=F32) for l, s in zip(lp, sp)]
        n_l = c_len + GROUP
        for si, (rev, _, y_ref, bi) in enumerate(seqs):
            _, uy, _, o = terms[k * len(seqs) + si]
            y_ref[bi, sl(rev, k), :] = res[si][0:c_len] + uy
            st[si] = res[si][c_len:n_l] + o
    for si in range(len(seqs)):
        st_ref[si] = st[si]


def _scan(feats, n_ctx_tiles):
    lw0, lw1, kd0, kd1, b0, b1, v, kk, r = feats
    nb, s, d = v.shape
    t = TOK_TILE
    nt = s // t
    msk, tri3, hm = _scan_masks()
    fwd = pl.BlockSpec((nb, t, GROUP), lambda q, i: (0, i, q))
    bwd_idx = lambda i: jnp.where(i < n_ctx_tiles, n_ctx_tiles - 1 - i, nt - 1 - (i - n_ctx_tiles))
    bwd = pl.BlockSpec((nb, t, GROUP), lambda q, i: (0, bwd_idx(i), q))
    return pl.pallas_call(
        _scan_kernel,
        grid=(d // GROUP, nt),
        in_specs=[fwd] * 6 + [bwd] * 6 + [pl.BlockSpec(msk.shape, lambda q, i: (0, 0, 0)),
                                          pl.BlockSpec(tri3.shape, lambda q, i: (0, 0, 0)),
                                          pl.BlockSpec(hm.shape, lambda q, i: (0, 0))],
        out_specs=[fwd, bwd],
        out_shape=[jax.ShapeDtypeStruct((nb, s, d), F32)] * 2,
        scratch_shapes=[pltpu.VMEM((2 * nb, GROUP, GROUP), F32)],
        compiler_params=_params("parallel", "arbitrary"),
        name="scan",
    )(lw0, kd0, b0, v, kk, r, lw1, kd1, b1, v, kk, r, msk, tri3, hm)


def _readout_kernel(yf_ref, yb_ref, r_ref, kd0_ref, kd1_ref, v_ref, g_ref, x_ref, mod_ref, vec_ref,
                    wo_ref, hd_ref, hdt_ref, rwt_ref, rb_ref, u_ref,
                    x3_ref, h2_ref, pos_ref, gd_ref, off_ref, *, tiles_per_moe, n_real_tiles):
    i = pl.program_id(0)

    @pl.when(i >= n_real_tiles)
    def _():
        _route_padding(h2_ref, pos_ref, gd_ref)

    def rows_chain(rows, mod, vec):
        headsum = lambda z: _headsum(z, hd_ref[...], hdt_ref[...])
        inv_k = 1.0 / RWKV_HEAD
        y = yf_ref[0, rows, :] + yb_ref[0, rows, :]
        bsum = headsum(r_ref[0, rows, :] * (kd0_ref[0, rows, :] + kd1_ref[0, rows, :]) * vec[2:3])
        yield
        yc = y - headsum(y) * inv_k
        yield
        var = headsum(yc * yc) * inv_k
        yield
        yn = yc * lax.rsqrt(var + GN_EPS) * vec[0:1] + vec[1:2]
        out = (yn + bsum * v_ref[0, rows, :]) * g_ref[0, rows, :]
        x3 = x_ref[rows, :] + mod[2:3] * _dotb(out, wo_ref[...])
        x3_ref[rows, :] = x3
        yield
        return (yield from _route_rows(x3, mod, vec[3:4], rwt_ref, rb_ref, h2_ref, rows))

    @pl.when(i < n_real_tiles)
    def _():
        vec = vec_ref[...]
        mod = mod_ref[0]
        parts = _interleave(*[rows_chain(rows, mod, vec) for rows in _row_splits(x_ref.shape[0], TILE_SPLIT)])
        _route_finish(parts, u_ref, off_ref, pos_ref, gd_ref, i % tiles_per_moe == 0)


def _readout(yf, yb, r, kd0, kd1, v, g, x2d, mod, vec, wo, hd, hdt, rwt, rb, n_ctx_tiles, moe_tile,
             moe_span, tokens_per_batch):
    ntok, d = x2d.shape
    t = TOK_TILE
    n_e = rwt.shape[0]
    tpb = tokens_per_batch // t
    n_real = ntok // t
    n_pad = _padded_tokens(ntok, moe_span)
    route_specs, route_shapes = _route_outputs(n_pad, d, n_e, moe_tile)
    real = lambda i: jnp.minimum(i, n_real - 1)
    seq = pl.BlockSpec((1, t, d), lambda i: (real(i) // tpb, n_ctx_tiles + real(i) % tpb, 0))
    tokspec = pl.BlockSpec((t, d), lambda i: (real(i), 0))
    full = lambda a: pl.BlockSpec(a.shape, lambda i: (0,) * a.ndim)
    u = _route_consts(t)
    kern = functools.partial(_readout_kernel, tiles_per_moe=moe_tile // t, n_real_tiles=n_real)
    return pl.pallas_call(
        kern,
        grid=(n_pad // t,),
        in_specs=[seq] * 7 + [tokspec, pl.BlockSpec((1, 8, d), lambda i: (real(i) // tpb, 0, 0)),
                              full(vec), full(wo), full(hd), full(hdt), full(rwt), full(rb), full(u)],
        out_specs=[tokspec] + route_specs,
        out_shape=[jax.ShapeDtypeStruct((ntok, d), F32)] + route_shapes,
        scratch_shapes=[pltpu.VMEM((n_e, 128), F32)],
        compiler_params=_params("arbitrary"),
        name="readout",
    )(yf, yb, r, kd0, kd1, v, g, x2d, mod, vec, wo, hd, hdt, rwt, rb, u)


def _final_kernel(x_ref, ft_ref, mod_ref, g_ref, o_ref):
    x = x_ref[...] + mod_ref[0][5:6] * jnp.transpose(ft_ref[...])
    o_ref[...] = _rms(x) * g_ref[0:1]


def _final(x2d, ft, mod, gvec, tokens_per_batch):
    ntok, d = x2d.shape
    t = TOK_TILE
    tpb = tokens_per_batch // t
    return pl.pallas_call(
        _final_kernel,
        grid=(ntok // t,),
        in_specs=[pl.BlockSpec((t, d), lambda i: (i, 0)),
                  pl.BlockSpec((d, t), lambda i: (0, i)),
                  pl.BlockSpec((1, 8, d), lambda i: (i // tpb, 0, 0)),
                  pl.BlockSpec((8, d), lambda i: (0, 0))],
        out_specs=pl.BlockSpec((t, d), lambda i: (i, 0)),
        out_shape=jax.ShapeDtypeStruct((ntok, d), F32),
        compiler_params=_params("parallel"),
        name="final",
    )(x2d, ft, mod, gvec)


def _rows8(rows, d):
    n = -(-len(rows) // 8) * 8
    out = jnp.zeros((n, d), F32)
    return out.at[:len(rows)].set(jnp.stack([jnp.asarray(r, F32) for r in rows]))


def _pad_lora(w, total):
    two, r, d = w.shape
    out = jnp.zeros((two, total, d), w.dtype)
    for dd in range(two):
        out = out.at[dd, dd * r:(dd + 1) * r].set(w[dd])
    return out


def kernel(x, c, ctx, c_ctx, norm_g, ada_w, ada_b, pool_w, pool_ls, rwkv_mu, rwkv_w_rkv, rwkv_w0, rwkv_w1, rwkv_w2, rwkv_a0, rwkv_a1, rwkv_a2, rwkv_g1, rwkv_g2, rwkv_k_k, rwkv_k_a, rwkv_r_k, rwkv_ln_g, rwkv_ln_b, rwkv_w_o, moe_router_w, moe_router_b, moe_w_gu, moe_b_gu, moe_w_down, moe_b_down, final_g):
    nb, seq, d = x.shape
    n_ctx = ctx.shape[1]
    depth = norm_g.shape[0]
    n_e = moe_router_w.shape[2]
    t = TOK_TILE
    assert depth == 2 and nb <= 7 and seq % t == 0 and n_ctx % t == 0 and t % GRID_W == 0 and d % GROUP == 0
    n_lat, n_ctxtok = nb * seq, nb * n_ctx
    moe_tile_lat = min(MOE_TILE, n_lat)
    moe_tile_ctx = min(MOE_TILE, n_ctxtok)
    group_lat = min(MOE_GROUP, -(-n_lat // moe_tile_lat))
    group_ctx = min(MOE_GROUP, -(-n_ctxtok // moe_tile_ctx))
    n_ctx_tiles = n_ctx // t
    seq_total = n_ctx + seq

    cond8 = jnp.zeros((8, d), F32).at[:nb].set(c).at[nb].set(c_ctx)
    ada = _ada(cond8, ada_w, ada_b).reshape(depth, 8, N_MOD, d)
    pad8 = lambda m: jnp.pad(m, ((0, 0), (0, 8 - N_MOD), (0, 0)))
    mod_lat = [pad8(ada[i, :nb]) for i in range(depth)]
    mod_ctx = [pad8(jnp.broadcast_to(ada[i, nb][None], (nb, N_MOD, d))) for i in range(depth)]

    wgu_all = _expert_weights_bf16(moe_w_gu, transpose=False)
    wdt_all = _expert_weights_bf16(moe_w_down, transpose=True)

    def expert_params(i):
        return i, wgu_all, moe_b_gu[i].reshape(n_e, 1, -1), wdt_all, moe_b_down[i].T

    def router_params(i):
        return moe_router_w[i].T, moe_router_b[i].reshape(n_e, 1)

    x2d = x.reshape(n_lat, d)
    ctx2d = ctx.reshape(n_ctxtok, d)

    vec0 = _rows8([norm_g[0, 0], norm_g[0, 1], pool_ls[0]], d)
    pw = pool_w[0].astype(BF16)
    rwt0, rb0 = router_params(0)
    ex0 = expert_params(0)
    x1, h2, pos, gd = _pool_layer(x2d, mod_lat[0], vec0, pw, rwt0, rb0, GRID_W, moe_tile_lat,
                                  group_lat * moe_tile_lat, seq)
    ft_lat = _moe(h2, pos, gd, *ex0, group_lat)
    c1, h2c, posc, gdc = _pool_layer(ctx2d, mod_ctx[0], vec0, pw, rwt0, rb0, n_ctx, moe_tile_ctx,
                                     group_ctx * moe_tile_ctx, n_ctx)
    ft_ctx = _moe(h2c, posc, gdc, *ex0, group_ctx)

    vec1 = _rows8([norm_g[1, 0]], d)
    x2, h_all = _respre(x1, ft_lat, mod_lat[0], mod_lat[1], vec1, None, seq, n_ctx_tiles, seq_total)
    _, h_all = _respre(c1, ft_ctx, mod_ctx[0], mod_ctx[1], vec1, h_all, n_ctx, 0, seq_total)

    fv = _rows8([rwkv_mu[0, m] for m in range(6)]
                + [rwkv_w0[0, 0], rwkv_w0[0, 1], rwkv_a0[0, 0], rwkv_a0[0, 1], rwkv_k_k[0], rwkv_k_a[0]], d)
    lora = rwkv_w1.shape[3]
    cat2 = lambda w: jnp.concatenate([w[0, 0], w[0, 1]], axis=1).astype(BF16)
    hd, hdt = _head_indicators(d)
    feats = _features(h_all, fv, rwkv_w_rkv[0].astype(BF16),
                      cat2(rwkv_w1), _pad_lora(rwkv_w2[0], 2 * lora).astype(BF16),
                      cat2(rwkv_a1), _pad_lora(rwkv_a2[0], 2 * rwkv_a1.shape[3]).astype(BF16),
                      rwkv_g1[0].astype(BF16), rwkv_g2[0].astype(BF16), hd, hdt, n_ctx_tiles)
    lw0, lw1, kd0, kd1, b0, b1, v, kk, r, g = feats
    yf, yb = _scan((lw0, lw1, kd0, kd1, b0, b1, v, kk, r), n_ctx_tiles)

    vec_ro = _rows8([rwkv_ln_g[0], rwkv_ln_b[0], rwkv_r_k[0].reshape(-1), norm_g[1, 1]], d)
    rwt1, rb1 = router_params(1)
    x3, h2, pos, gd = _readout(yf, yb, r, kd0, kd1, v, g, x2, mod_lat[1], vec_ro,
                               rwkv_w_o[0].astype(BF16), hd, hdt, rwt1, rb1, n_ctx_tiles, moe_tile_lat,
                               group_lat * moe_tile_lat, seq)
    ft = _moe(h2, pos, gd, *expert_params(1), group_lat)
    out = _final(x3, ft, mod_lat[1], _rows8([final_g], d), seq)
    return out.reshape(nb, seq, d)
```

```python
import functools

import numpy as np
import jax
import jax.numpy as jnp
from jax import lax
from jax.experimental import pallas as pl
from jax.experimental.pallas import tpu as pltpu

F32 = jnp.float32
BF16 = jnp.bfloat16
NN = (((1,), (0,)), ((), ()))
NT = (((1,), (1,)), ((), ()))
TN = (((0,), (0,)), ((), ()))

N_MOD = 6
NORM_EPS = 1e-6
GRID_W = 64
POOL_WINDOWS = (2, 4, 8, 16)
RWKV_HEAD = 64
GN_EPS = 64e-5
TOP_K = 4
SWIGLU_ALPHA = 1.702
SWIGLU_LIMIT = 7.0

TOK_TILE = 256
TILE_SPLIT = 2
MOE_TILE = 1792
MOE_GROUP = 4
MOE_ROWS = 256
MOE_CHUNK_UNITS = 8
CHUNK = 64
GROUP_HEADS = 2
GROUP = GROUP_HEADS * RWKV_HEAD
SCAN_GROUPS = 2
HEAD_PAD = 128
VMEM_LIMIT_BYTES = 56 * 1024 * 1024


def _dotx(a, b, dims=NN):
    return lax.dot_general(a, b, dims, precision=lax.Precision.HIGHEST, preferred_element_type=F32)


def _dotb(a, b, dims=NN):
    return lax.dot_general(a.astype(BF16), b.astype(BF16), dims, preferred_element_type=F32)


def _params(*sem):
    return pltpu.CompilerParams(dimension_semantics=sem, vmem_limit_bytes=VMEM_LIMIT_BYTES)


def _padded_tokens(ntok, tile):
    return -(-ntok // tile) * tile


def _rms(x):
    return x * lax.rsqrt(jnp.mean(x * x, axis=-1, keepdims=True) + NORM_EPS)


def _sigmoid(x):
    return 1.0 / (1.0 + jnp.exp(-x))


def _ada_kernel(c_ref, w_ref, b_ref, o_ref):
    c = c_ref[...]
    o_ref[0] = _dotx(c * _sigmoid(c), w_ref[0]) + b_ref[0]


def _ada(cond8, ada_w, ada_b):
    depth, d, nd = ada_w.shape
    return pl.pallas_call(
        _ada_kernel,
        grid=(depth, nd // d),
        in_specs=[pl.BlockSpec((8, d), lambda i, n: (0, 0)),
                  pl.BlockSpec((1, d, d), lambda i, n: (i, 0, n)),
                  pl.BlockSpec((1, 1, d), lambda i, n: (i, 0, n))],
        out_specs=pl.BlockSpec((1, 8, d), lambda i, n: (i, 0, n)),
        out_shape=jax.ShapeDtypeStruct((depth, 8, nd), F32),
        compiler_params=_params("parallel", "parallel"),
        name="ada",
    )(cond8, ada_w, ada_b.reshape(depth, 1, nd))


def _route_padding(h2_ref, pos_ref, gd_ref):
    h2_ref[...] = jnp.zeros_like(h2_ref)
    pos_ref[...] = jnp.full(pos_ref.shape, -1, jnp.int32)
    gd_ref[...] = jnp.zeros_like(gd_ref)


def _interleave(*gens):
    out = [None] * len(gens)
    live = dict(enumerate(gens))
    while live:
        for k, g in list(live.items()):
            try:
                next(g)
            except StopIteration as stop:
                out[k] = stop.value
                del live[k]
    return out


def _row_splits(n_rows, n_split):
    step = n_rows // n_split
    return [slice(k * step, (k + 1) * step) for k in range(n_split)]


def _route_rows(x_new, mod, g2, rwt_ref, rb_ref, h2_ref, rows):
    h2 = _rms(x_new) * g2 * (1.0 + mod[4:5]) + mod[3:4]
    h2_ref[rows, :] = h2.astype(BF16)
    yield
    logits = _dotx(rwt_ref[...], h2, NT) + rb_ref[...]
    yield
    n_e = logits.shape[0]
    eio = lax.broadcasted_iota(jnp.int32, logits.shape, 0).astype(F32)
    live = logits
    sels, vals = [], []
    for _ in range(TOP_K):
        m = jnp.max(live, axis=0, keepdims=True)
        idx = jnp.min(jnp.where(live == m, eio, float(n_e)), axis=0, keepdims=True)
        sel = eio == idx
        sels.append(sel)
        vals.append(m)
        live = jnp.where(sel, -jnp.inf, live)
        yield
    exps = [jnp.exp(v - vals[0]) for v in vals]
    inv = 1.0 / (exps[0] + exps[1] + exps[2] + exps[3])
    gd = jnp.zeros_like(logits)
    maskf = jnp.zeros_like(logits)
    for k in range(TOP_K):
        gd = jnp.where(sels[k], exps[k] * inv, gd)
        maskf = jnp.where(sels[k], 1.0, maskf)
    return maskf, gd


def _route_finish(parts, u_ref, off_ref, pos_ref, gd_ref, reset):
    @pl.when(reset)
    def _():
        off_ref[...] = jnp.zeros_like(off_ref)

    off = off_ref[:, 0:1]
    lane0 = 0
    for maskf, gd in parts:
        n = maskf.shape[1]
        incl = _dotb(maskf, u_ref[0:n, 0:n])
        pos_ref[0, :, lane0:lane0 + n] = jnp.where(maskf > 0.5, off + incl - 1.0, -1.0).astype(jnp.int32)
        gd_ref[0, :, lane0:lane0 + n] = gd
        off = off + jnp.sum(maskf, axis=1, keepdims=True)
        lane0 += n
    off_ref[...] = jnp.broadcast_to(off, off_ref.shape)


def _route_outputs(n_pad, d, n_e, moe_tile):
    t = TOK_TILE
    tpm = moe_tile // t
    nt = n_pad // moe_tile
    spec = pl.BlockSpec((1, n_e, t), lambda i: (i // tpm, 0, i % tpm))
    shapes = [jax.ShapeDtypeStruct((n_pad, d), BF16),
              jax.ShapeDtypeStruct((nt, n_e, moe_tile), jnp.int32),
              jax.ShapeDtypeStruct((nt, n_e, moe_tile), F32)]
    return [pl.BlockSpec((t, d), lambda i: (i, 0)), spec, spec], shapes


def _route_consts(tile):
    s = np.arange(tile)
    return jnp.asarray((s[:, None] <= s[None, :]).astype(np.float32), BF16)


def _pool_kernel(x_ref, mod_ref, vec_ref, pm_ref, pw_ref, rwt_ref, rb_ref, u_ref,
                 x1_ref, h2_ref, pos_ref, gd_ref, off_ref, *, tiles_per_moe, n_real_tiles, n_split):
    i = pl.program_id(0)

    @pl.when(i >= n_real_tiles)
    def _():
        _route_padding(h2_ref, pos_ref, gd_ref)

    def rows_chain(rows, mod, vec):
        x = x_ref[rows, :]
        h = _rms(x) * vec[0:1] * (1.0 + mod[1:2]) + mod[0:1]
        yield
        gw = pw_ref.shape[1]
        ys = []
        for g in range(len(POOL_WINDOWS)):
            d = _dot3(pm_ref[g, rows, rows], h[:, g * gw:(g + 1) * gw])
            yield
            ys.append(_dotb(d, pw_ref[g]))
            yield
        x1 = x + mod[2:3] * (jnp.concatenate(ys, axis=1) * vec[2:3])
        x1_ref[rows, :] = x1
        return (yield from _route_rows(x1, mod, vec[1:2], rwt_ref, rb_ref, h2_ref, rows))

    @pl.when(i < n_real_tiles)
    def _():
        mod = mod_ref[0]
        vec = vec_ref[...]
        parts = _interleave(*[rows_chain(rows, mod, vec) for rows in _row_splits(x_ref.shape[0], n_split)])
        _route_finish(parts, u_ref, off_ref, pos_ref, gd_ref, i % tiles_per_moe == 0)


def _pool_matrices(tile, row_len):
    p = np.arange(tile)
    pp = p % row_len
    out = []
    for win in POOL_WINDOWS:
        lo = np.clip(pp - win // 2, 0, row_len - 1)
        hi = np.clip(pp + win // 2 - 1, 0, row_len - 1)
        cnt = (hi - lo + 1).astype(np.float64)
        same = (p[:, None] // row_len) == (p[None, :] // row_len)
        inwin = same & (pp[None, :] >= lo[:, None]) & (pp[None, :] <= hi[:, None])
        out.append(inwin / cnt[:, None] - np.eye(tile))
    return jnp.asarray(np.stack(out).astype(np.float32))


def _pool_layer(x2d, mod, vec, pool_w, rwt, rb, row_len, moe_tile, moe_span, tokens_per_batch):
    ntok, d = x2d.shape
    t = TOK_TILE
    n_e = rwt.shape[0]
    tiles_per_batch = tokens_per_batch // t
    n_real = ntok // t
    n_pad = _padded_tokens(ntok, moe_span)
    route_specs, route_shapes = _route_outputs(n_pad, d, n_e, moe_tile)
    pm = _pool_matrices(t, row_len)
    n_split = TILE_SPLIT if (t // TILE_SPLIT) % row_len == 0 else 1
    kern = functools.partial(_pool_kernel, tiles_per_moe=moe_tile // t, n_real_tiles=n_real, n_split=n_split)
    const2 = lambda i: (0, 0)
    const3 = lambda i: (0, 0, 0)
    real = lambda i: jnp.minimum(i, n_real - 1)
    return pl.pallas_call(
        kern,
        grid=(n_pad // t,),
        in_specs=[pl.BlockSpec((t, d), lambda i: (real(i), 0)),
                  pl.BlockSpec((1, 8, d), lambda i: (real(i) // tiles_per_batch, 0, 0)),
                  pl.BlockSpec((8, d), const2),
                  pl.BlockSpec(pm.shape, const3),
                  pl.BlockSpec(pool_w.shape, const3),
                  pl.BlockSpec(rwt.shape, const2),
                  pl.BlockSpec(rb.shape, const2),
                  pl.BlockSpec((t, t), const2)],
        out_specs=[pl.BlockSpec((t, d), lambda i: (real(i), 0))] + route_specs,
        out_shape=[jax.ShapeDtypeStruct((ntok, d), F32)] + route_shapes,
        scratch_shapes=[pltpu.VMEM((n_e, 128), F32)],
        compiler_params=_params("arbitrary"),
        name="pool",
    )(x2d, mod, vec, pm, pool_w, rwt, rb, _route_consts(t))


def _cast_kernel(w_ref, o_ref, *, transpose):
    w = w_ref[0, 0].astype(BF16)
    o_ref[0, 0] = jnp.transpose(w) if transpose else w


def _expert_weights_bf16(w, transpose):
    n_l, n_e, r, c = w.shape
    blk = min(r, c)
    out_idx = (lambda l, e, i, j: (l, e, j, i)) if transpose else (lambda l, e, i, j: (l, e, i, j))
    return pl.pallas_call(
        functools.partial(_cast_kernel, transpose=transpose),
        grid=(n_l, n_e, r // blk, c // blk),
        in_specs=[pl.BlockSpec((1, 1, blk, blk), lambda l, e, i, j: (l, e, i, j))],
        out_specs=pl.BlockSpec((1, 1, blk, blk), out_idx),
        out_shape=jax.ShapeDtypeStruct((n_l, n_e, c, r) if transpose else w.shape, BF16),
        compiler_params=_params("parallel", "parallel", "parallel", "parallel"),
        name="wprep",
    )(w)


def _one_hot_rows(prow, r0, n_rows):
    rows = lax.broadcasted_iota(jnp.int32, (n_rows, prow.shape[1]), 0) + r0
    return jnp.where(prow == rows, 1.0, 0.0).astype(BF16)


def _expert_kernel(grp_ref, sub_ref, e_ref, r0_ref, half_ref,
                   pos_ref, gd_ref, x_ref, wgu_ref, bgu_ref, wdt_ref, o_ref):
    b = pl.program_id(0)
    f = wdt_ref.shape[3]
    n_rows = o_ref.shape[1] * o_ref.shape[3]
    ts = pos_ref.shape[2]
    e = e_ref[b]
    sub = sub_ref[b]
    prow = pos_ref[sub, pl.ds(e, 1), :]
    grow = gd_ref[sub, pl.ds(e, 1), :]
    g_hi = grow.astype(BF16)
    g_mid = (grow - g_hi.astype(F32)).astype(BF16)
    g_lo = (grow - g_hi.astype(F32) - g_mid.astype(F32)).astype(BF16)
    g3 = jnp.concatenate([g_hi, g_mid, g_lo, jnp.zeros((5, ts), BF16)], axis=0)

    def rows_out(n):
        oh = _one_hot_rows(prow, r0_ref[b], n)
        x = x_ref[pl.ds(pl.multiple_of(sub * ts, ts), ts), :]
        xg = jnp.dot(oh, x, preferred_element_type=F32).astype(BF16)
        gu = jnp.dot(xg, wgu_ref[0, 0], preferred_element_type=F32) + bgu_ref[0]
        gate = jnp.minimum(gu[:, :f], SWIGLU_LIMIT)
        up = jnp.clip(gu[:, f:], -SWIGLU_LIMIT, SWIGLU_LIMIT)
        act = (up + 1.0) * (gate * _sigmoid(SWIGLU_ALPHA * gate))
        yt = lax.dot_general(wdt_ref[0, 0], act.astype(BF16), NT, preferred_element_type=F32)
        g8 = lax.dot_general(g3, oh, NT, preferred_element_type=F32)
        return (yt * (g8[0:1] + g8[1:2] + g8[2:3])).astype(BF16)

    @pl.when(half_ref[b] == 0)
    def _():
        yt = rows_out(n_rows)
        o_ref[0, 0] = yt[:, 0:n_rows // 2]
        o_ref[0, 1] = yt[:, n_rows // 2:n_rows]

    @pl.when(half_ref[b] == 1)
    def _():
        o_ref[0, 0] = rows_out(n_rows // 2)
        o_ref[0, 1] = jnp.zeros(o_ref.shape[2:], BF16)


def _combine_kernel(tile_ref, first_ref, blk_ref, unit_ref, e_ref, r0_ref, pos_ref, gd_ref, *rest):
    yt_refs, bdt_ref, o_ref = rest[:MOE_CHUNK_UNITS], rest[-2], rest[-1]
    c = pl.program_id(0)
    n_rows = yt_refs[0].shape[3]

    @pl.when(first_ref[c] == 1)
    def _():
        o_ref[...] = _dot3(bdt_ref[...], gd_ref[0])

    slot = lambda q: c * MOE_CHUNK_UNITS + q
    oh = jnp.concatenate([_one_hot_rows(pos_ref[0, pl.ds(e_ref[slot(q)], 1), :], r0_ref[slot(q)], n_rows)
                          for q in range(MOE_CHUNK_UNITS)], axis=0)
    yt = jnp.concatenate([ref[0, 0] for ref in yt_refs], axis=1)
    o_ref[...] += jnp.dot(yt, oh, preferred_element_type=F32)


def _moe_tables(pos, group):
    nt, n_e, ts = pos.shape
    ng = nt // group
    rb, cb = MOE_ROWS, MOE_CHUNK_UNITS
    i32 = lambda a: a.astype(jnp.int32)
    cnt = jnp.sum(i32(pos >= 0), axis=-1)
    nb = (cnt + rb - 1) // rb
    max_tile_blocks = ts * TOP_K // rb + n_e
    order = lambda a: a.reshape(ng, group, n_e).transpose(0, 2, 1).reshape(-1)
    flat = order(nb)
    cum = jnp.cumsum(flat)
    start = cum - flat
    n_blocks = cum[-1]
    b = jnp.minimum(jnp.arange(nt * max_tile_blocks, dtype=jnp.int32), n_blocks - 1)
    count_le = lambda sorted_, q: jnp.sum(i32(sorted_[None, :] <= q[:, None]), axis=1)
    idx = jnp.minimum(count_le(cum, b), flat.shape[0] - 1)
    r0 = (b - start[idx]) * rb
    expert_tables = (i32(idx // (n_e * group)), i32(idx % group), i32((idx // group) % n_e), i32(r0),
                     i32(order(cnt)[idx] - r0 <= rb // 2))
    ru = rb // 2
    nu = (cnt + ru - 1) // ru
    ce = jnp.cumsum(nu, axis=1)
    nch = (ce[:, -1] + cb - 1) // cb
    cch = jnp.cumsum(nch)
    n_chunks = cch[-1]
    max_chunks = nt * (-(-(ts * TOP_K // ru + n_e) // cb))
    c = jnp.minimum(jnp.arange(max_chunks, dtype=jnp.int32), n_chunks - 1)
    tile = jnp.minimum(count_le(cch, c), nt - 1)
    lc = c - (cch - nch)[tile]
    slot = (lc[:, None] * cb + jnp.arange(cb, dtype=jnp.int32)[None, :]).reshape(-1)
    tile_s = jnp.repeat(tile, cb)
    used = slot < ce[tile_s, -1]
    slot = jnp.where(used, slot, 0)
    ce_s = ce[tile_s]
    e_s = jnp.minimum(jnp.sum(i32(slot[:, None] >= ce_s), axis=1), n_e - 1)
    local = slot - jnp.take_along_axis(ce_s - nu[tile_s], e_s[:, None], axis=1)[:, 0]
    blk = start.reshape(ng, n_e, group)[tile_s // group, e_s, tile_s % group] + local // 2
    combine_tables = (i32(tile), i32(lc == 0), i32(blk), i32(local % 2), i32(e_s),
                      i32(jnp.where(used, local * ru, 1 << 20)))
    return expert_tables, i32(n_blocks), combine_tables, i32(n_chunks)


def _moe(h2, pos, gd, layer, wgu, bgu, wdt, bdt, group):
    ntok, d = h2.shape
    nt, n_e, ts = pos.shape
    f2 = wgu.shape[3]
    rb, cb = MOE_ROWS, MOE_CHUNK_UNITS
    expert_tables, n_blocks, combine_tables, n_chunks = _moe_tables(pos, group)
    max_blocks = expert_tables[0].shape[0]
    yt = pl.pallas_call(
        _expert_kernel,
        grid_spec=pltpu.PrefetchScalarGridSpec(
            num_scalar_prefetch=5,
            grid=(n_blocks,),
            in_specs=[pl.BlockSpec((group, n_e, ts), lambda b, gp, sb, ee, rr, hf: (gp[b], 0, 0)),
                      pl.BlockSpec((group, n_e, ts), lambda b, gp, sb, ee, rr, hf: (gp[b], 0, 0)),
                      pl.BlockSpec((group * ts, d), lambda b, gp, sb, ee, rr, hf: (gp[b], 0),
                                   pipeline_mode=pl.Buffered(1)),
                      pl.BlockSpec((1, 1, d, f2), lambda b, gp, sb, ee, rr, hf: (layer, ee[b], 0, 0)),
                      pl.BlockSpec((1, 1, f2), lambda b, gp, sb, ee, rr, hf: (ee[b], 0, 0)),
                      pl.BlockSpec((1, 1, d, f2 // 2), lambda b, gp, sb, ee, rr, hf: (layer, ee[b], 0, 0))],
            out_specs=pl.BlockSpec((1, 2, d, rb // 2), lambda b, gp, sb, ee, rr, hf: (b, 0, 0, 0)),
        ),
        out_shape=jax.ShapeDtypeStruct((max_blocks, 2, d, rb // 2), BF16),
        compiler_params=_params("arbitrary"),
        name="expert",
    )(*expert_tables, pos, gd, h2, wgu, bgu, wdt)
    yt_spec = lambda q: pl.BlockSpec((1, 1, d, rb // 2),
                                     lambda c, tl, fs, bk, un, ee, rr: (bk[c * cb + q], un[c * cb + q], 0, 0))
    return pl.pallas_call(
        _combine_kernel,
        grid_spec=pltpu.PrefetchScalarGridSpec(
            num_scalar_prefetch=6,
            grid=(n_chunks,),
            in_specs=[pl.BlockSpec((1, n_e, ts), lambda c, tl, fs, bk, un, ee, rr: (tl[c], 0, 0)),
                      pl.BlockSpec((1, n_e, ts), lambda c, tl, fs, bk, un, ee, rr: (tl[c], 0, 0))]
                     + [yt_spec(q) for q in range(cb)]
                     + [pl.BlockSpec((d, n_e), lambda c, tl, fs, bk, un, ee, rr: (0, 0))],
            out_specs=pl.BlockSpec((d, ts), lambda c, tl, fs, bk, un, ee, rr: (0, tl[c]),
                                   pipeline_mode=pl.Buffered(1)),
        ),
        out_shape=jax.ShapeDtypeStruct((d, ntok), F32),
        compiler_params=_params("arbitrary"),
        name="combine",
    )(*combine_tables, pos, gd, *([yt] * cb), bdt)


def _respre_kernel(x_ref, ft_ref, modp_ref, modc_ref, vec_ref, *rest):
    x2_ref, h_ref = rest[-2], rest[-1]
    x2 = x_ref[...] + modp_ref[0][5:6] * jnp.transpose(ft_ref[...])
    x2_ref[...] = x2
    modc = modc_ref[0]
    h_ref[0] = _rms(x2) * vec_ref[0:1] * (1.0 + modc[1:2]) + modc[0:1]


def _respre(x2d, ft, modp, modc, vec, h_all, tokens_per_batch, tile_offset, seq_total):
    ntok, d = x2d.shape
    t = TOK_TILE
    nb = modp.shape[0]
    tpb = tokens_per_batch // t
    in_specs = [pl.BlockSpec((t, d), lambda i: (i, 0)),
                pl.BlockSpec((d, t), lambda i: (0, i)),
                pl.BlockSpec((1, 8, d), lambda i: (i // tpb, 0, 0)),
                pl.BlockSpec((1, 8, d), lambda i: (i // tpb, 0, 0)),
                pl.BlockSpec((8, d), lambda i: (0, 0))]
    args = [x2d, ft, modp, modc, vec]
    aliases = {}
    if h_all is not None:
        in_specs.append(pl.BlockSpec(memory_space=pl.ANY))
        args.append(h_all)
        aliases = {5: 1}
    return pl.pallas_call(
        _respre_kernel,
        grid=(ntok // t,),
        in_specs=in_specs,
        out_specs=[pl.BlockSpec((t, d), lambda i: (i, 0)),
                   pl.BlockSpec((1, t, d), lambda i: (i // tpb, tile_offset + i % tpb, 0))],
        out_shape=[jax.ShapeDtypeStruct((ntok, d), F32),
                   jax.ShapeDtypeStruct((nb, seq_total, d), F32)],
        input_output_aliases=aliases,
        compiler_params=_params("parallel"),
        name="respre",
    )(*args)


def _feat_kernel(h_ref, hp_ref, hn_ref, fv_ref, wrkv_ref, w1_ref, w2_ref, a1_ref, a2_ref, g1_ref, g2_ref,
                 hd_ref, hdt_ref,
                 lw0_ref, lw1_ref, kd0_ref, kd1_ref, b0_ref, b1_ref, v_ref, kk_ref, r_ref, g_ref,
                 *, n_ctx_tiles, n_tiles):
    i = pl.program_id(1)
    h = h_ref[0]
    t = h.shape[0]
    fv = fv_ref[...]
    first = jnp.logical_or(i == 0, i == n_ctx_tiles)
    last = jnp.logical_or(i == n_ctx_tiles - 1, i == n_tiles - 1)
    prow = jnp.where(first, 0.0, hp_ref[0][7:8])
    nrow = jnp.where(last, 0.0, hn_ref[0][0:1])
    rio = lax.broadcasted_iota(jnp.int32, h.shape, 0)
    hdn = jnp.where(rio == 0, prow, pltpu.roll(h, 1, 0))
    hup = jnp.where(rio == t - 1, nrow, pltpu.roll(h, t - 1, 0))
    xx = 0.5 * (hdn + hup) - h

    mix = lambda m: h + xx * fv[m:m + 1]
    r = _dotb(mix(0), wrkv_ref[0])
    k = _dotb(mix(2), wrkv_ref[1])
    v = _dotb(mix(3), wrkv_ref[2])
    tw = jnp.tanh(_dotb(mix(1), w1_ref[...]))
    la = _dotb(mix(4), a1_ref[...])
    g = _dotb(_sigmoid(_dotb(mix(5), g1_ref[...])), g2_ref[...])

    kkraw = k * fv[10:11]
    ss = _headsum(kkraw * kkraw, hd_ref[...], hdt_ref[...])
    kk = kkraw / jnp.maximum(jnp.sqrt(ss), 1e-12)
    k_a = fv[11:12]
    decay_scale = float(np.exp(-0.5))
    for d, (lw_ref, kd_ref, b_ref) in enumerate(((lw0_ref, kd0_ref, b0_ref), (lw1_ref, kd1_ref, b1_ref))):
        zw = fv[6 + d:7 + d] + _dotb(tw, w2_ref[d])
        lw_ref[0] = -decay_scale * _sigmoid(zw)
        icl = _sigmoid(fv[8 + d:9 + d] + _dotb(la, a2_ref[d]))
        kd_ref[0] = k * (1.0 + (icl - 1.0) * k_a)
        b_ref[0] = kk * icl
    v_ref[0] = v
    kk_ref[0] = kk
    r_ref[0] = r
    g_ref[0] = g


def _head_indicators(d):
    hd = np.zeros((d, HEAD_PAD), np.float32)
    hd[np.arange(d), np.arange(d) // RWKV_HEAD] = 1.0
    return jnp.asarray(hd), jnp.asarray(hd.T.copy())


def _features(h_all, fv, wrkv, w1c, w2p, a1c, a2p, g1, g2, hd, hdt, n_ctx_tiles):
    nb, s, d = h_all.shape
    t = TOK_TILE
    nt = s // t
    r8 = t // 8
    kern = functools.partial(_feat_kernel, n_ctx_tiles=n_ctx_tiles, n_tiles=nt)
    full = lambda a: pl.BlockSpec(a.shape, lambda b, i: (0,) * a.ndim)
    tok = pl.BlockSpec((1, t, d), lambda b, i: (b, i, 0))
    return pl.pallas_call(
        kern,
        grid=(nb, nt),
        in_specs=[tok,
                  pl.BlockSpec((1, 8, d), lambda b, i: (b, jnp.maximum(i * r8 - 1, 0), 0)),
                  pl.BlockSpec((1, 8, d), lambda b, i: (b, jnp.minimum((i + 1) * r8, s // 8 - 1), 0)),
                  full(fv), full(wrkv), full(w1c), full(w2p), full(a1c), full(a2p), full(g1), full(g2),
                  full(hd), full(hdt)],
        out_specs=[tok] * 10,
        out_shape=[jax.ShapeDtypeStruct((nb, s, d), F32)] * 10,
        compiler_params=_params("parallel", "parallel"),
        name="feat",
    )(h_all, h_all, h_all, fv, wrkv, w1c, w2p, a1c, a2p, g1, g2, hd, hdt)


def _scan_masks():
    n = GROUP_HEADS * CHUNK
    rr = np.arange(n)[:, None]
    cc = np.arange(n)[None, :]
    bd = (rr // CHUNK) == (cc // CHUNK)
    r_, c_ = rr % CHUNK, cc % CHUNK
    out = []
    for rev in (False, True):
        strict = bd & ((c_ > r_) if rev else (c_ < r_))
        incl = bd & ((c_ >= r_) if rev else (c_ <= r_))
        d16 = strict & ((r_ // 16) == (c_ // 16))
        l1 = strict & ((r_ // 32) == (c_ // 32)) & ((r_ // 16) != (c_ // 16))
        l2 = strict & ((r_ // 32) != (c_ // 32))
        out.append([strict, incl, d16, l1, l2])
    masks = [out[0][m] for m in range(5)] + [out[1][m] for m in range(5)] + [bd, rr == cc]
    hm = np.zeros((8, GROUP), np.float32)
    for j in range(GROUP_HEADS):
        hm[j, j * RWKV_HEAD:(j + 1) * RWKV_HEAD] = 1.0
    tri3 = np.stack([np.tile(out[rev][1][0:CHUNK, 0:CHUNK], (1, 3)) for rev in (0, 1)]).astype(np.float32)
    return jnp.asarray(np.stack(masks).astype(np.float32)), jnp.asarray(tri3, BF16), jnp.asarray(hm)


def _split3(a):
    a1 = a.astype(BF16)
    r1 = a - a1.astype(F32)
    a2 = r1.astype(BF16)
    return a1, a2, (r1 - a2.astype(F32)).astype(BF16)


def _dot3(a, b, dims=NN):
    a1, a2, _ = _split3(a)
    b1, b2, _ = _split3(b)
    mm = lambda p, q: lax.dot_general(p, q, dims, preferred_element_type=F32)
    return mm(a1, b1) + (mm(a1, b2) + mm(a2, b1))


def _dot01(a, ind):
    ib = ind.astype(BF16)
    a1, a2, _ = _split3(a)
    mm = lambda p: jnp.dot(p, ib, preferred_element_type=F32)
    return mm(a1) + mm(a2)


def _headsum(z, hd, hdt):
    return _dot01(_dot01(z, hd), hdt)


def _chunk_terms(blocks, msk_ref, tri_ref, hm):
    c_len = blocks[0][0].shape[0]
    n4 = GROUP_HEADS * c_len
    revs = [blk[6] for blk in blocks]
    mask = lambda k: [msk_ref[(5 if rev else 0) + k] for rev in revs]
    bd, eye = msk_ref[10], msk_ref[11]
    each = lambda f, *cols: [f(*args) for args in zip(*cols)]
    stack4 = lambda z: jnp.concatenate([z * hm[j:j + 1] for j in range(GROUP_HEADS)], axis=0)
    tile4 = lambda z: jnp.concatenate([z] * GROUP_HEADS, axis=0)
    fold4 = lambda z: sum(z[j * c_len:(j + 1) * c_len] for j in range(GROUP_HEADS))
    lw, kd, b, v, kk, r = [[blk[k] for blk in blocks] for k in range(6)]

    tri3 = [tri_ref[1 if rev else 0] for rev in revs]
    parts = each(lambda z: jnp.concatenate(_split3(z), axis=0), lw)
    c = each(lambda t, p: jnp.dot(t, p, preferred_element_type=F32), tri3, parts)
    ctot = each(lambda c_, rev: c_[0:1] if rev else c_[c_len - 1:c_len], c, revs)
    en = each(lambda c_: jnp.exp(-c_), c)
    khat = each(lambda kk_, c_, lw_: kk_ * jnp.exp(c_ - lw_), kk, c, lw)
    rhat = each(lambda r_, c_: r_ * jnp.exp(c_), r, c)
    bch = each(jnp.multiply, b, en)
    kch = each(jnp.multiply, kd, en)
    et = each(jnp.exp, ctot)
    btil = each(jnp.multiply, bch, et)
    ktil = each(jnp.multiply, kch, et)
    kh4, rh4, v4 = each(stack4, khat), each(stack4, rhat), each(stack4, v)
    sc = each(lambda k_, r_, b_, c_: _dotb(jnp.concatenate([k_, r_], axis=0),
                                           jnp.concatenate([tile4(b_), tile4(c_)], axis=0), NT),
              kh4, rh4, bch, kch)
    a_bk = each(lambda s, m_: s[0:n4, 0:n4] * m_, sc, mask(0))
    a_kk = each(lambda s, m_: s[0:n4, n4:2 * n4] * m_, sc, mask(0))
    b_br = each(lambda s, m_: s[n4:2 * n4, 0:n4] * m_, sc, mask(1))
    b_kr = each(lambda s, m_: s[n4:2 * n4, n4:2 * n4] * m_, sc, mask(1))
    avp = each(lambda a_, b_, v_: _dotb(jnp.concatenate([a_, b_], axis=0), v_), a_kk, b_kr, v4)
    av4 = [z[0:n4] for z in avp]
    p2 = [z[n4:2 * n4] for z in avp]
    ad = each(jnp.multiply, a_bk, mask(2))
    a2 = each(_dotb, ad, ad)
    times_and_square = lambda x_, p_: _dotb(jnp.concatenate([x_, p_], axis=0), p_)
    x = [eye - ad_ for ad_ in ad]
    both = each(times_and_square, x, a2)
    x = each(lambda x_, z: x_ + z[0:n4], x, both)
    a4 = [z[n4:2 * n4] for z in both]
    both = each(times_and_square, x, a4)
    x = each(lambda x_, z: x_ + z[0:n4], x, both)
    a8 = [z[n4:2 * n4] for z in both]
    dinv = each(lambda x_, a8_: x_ + _dotb(x_, a8_), x, a8)
    t1 = each(lambda d_, a_, m_: _dotb(d_, a_ * m_), dinv, a_bk, mask(3))
    b1 = each(lambda d_, t_: d_ - _dotb(t_, d_), dinv, t1)
    t2 = each(lambda b_, a_, m_: _dotb(b_, a_ * m_), b1, a_bk, mask(4))
    tinv = each(lambda b_, t_: b_ - _dotb(t_, b_), b1, t2)
    wu = each(lambda t_, k_, a_: _dotb(t_, jnp.concatenate([k_, a_], axis=1)), tinv, kh4, av4)
    p1 = each(_dotb, b_br, wu)
    q = GROUP
    wy = each(lambda r_, p_: fold4(r_ - p_[:, 0:q]), rh4, p1)
    uy = each(lambda p2_, p_: fold4(p2_ - p_[:, q:2 * q]), p2, p1)
    w1 = each(lambda w_: fold4(w_[:, 0:q]), wu)
    u = each(lambda w_: fold4(w_[:, q:2 * q]), wu)
    om = each(lambda k_, b_, v_, u_, w_: _dotb(
        jnp.concatenate([k_, b_], axis=0),
        jnp.concatenate([jnp.concatenate([v_, jnp.zeros_like(v_)], axis=1),
                         jnp.concatenate([-u_, -w_], axis=1)], axis=0), TN), ktil, btil, v, u, w1)
    o = [bd * z[:, 0:q] for z in om]
    m = each(lambda e_, z: eye * e_ + bd * z[:, q:2 * q], et, om)
    return list(zip(wy, uy, m, o))


def _scan_kernel(lwf, kdf, bf, vf, kkf, rf, lwb, kdb, bb, vb, kkb, rb, msk_ref, tri_ref, hm_ref,
                 yf_ref, yb_ref, st_ref):
    @pl.when(pl.program_id(1) == 0)
    def _():
        st_ref[...] = jnp.zeros_like(st_ref)

    hm = hm_ref[...]
    n_batch = lwf.shape[0]
    n_chunks = lwf.shape[1] // CHUNK
    dirs = ((False, (lwf, kdf, bf, vf, kkf, rf), yf_ref), (True, (lwb, kdb, bb, vb, kkb, rb), yb_ref))
    lane_groups = [slice(g * GROUP, (g + 1) * GROUP) for g in range(lwf.shape[2] // GROUP)]
    seqs = [(rev, refs, y_ref, bi, lg) for rev, refs, y_ref in dirs for bi in range(n_batch) for lg in lane_groups]
    order = lambda rev: list(reversed(range(n_chunks))) if rev else list(range(n_chunks))
    sl = lambda rev, k: slice(order(rev)[k] * CHUNK, (order(rev)[k] + 1) * CHUNK)
    blocks = [tuple(ref[bi, sl(rev, k), lg] for ref in refs) + (rev,)
              for k in range(n_chunks) for rev, refs, _, bi, lg in seqs]
    terms = _chunk_terms(blocks, msk_ref, tri_ref, hm)
    st = [st_ref[si] for si in range(len(seqs))]
    c_len = CHUNK
    for k in range(n_chunks):
        lhs = [jnp.concatenate(terms[k * len(seqs) + si][0::2], axis=0) for si in range(len(seqs))]
        lp = [_split3(z) for z in lhs]
        sp = [_split3(z) for z in st]
        res = [jnp.dot(jnp.concatenate([l[0], l[1], l[0]], axis=1), jnp.concatenate([s[0], s[0], s[1]], axis=0),
                       preferred_element_type=F32) for l, s in zip(lp, sp)]
        n_l = c_len + GROUP
        for si, (rev, _, y_ref, bi, lg) in enumerate(seqs):
            _, uy, _, o = terms[k * len(seqs) + si]
            y_ref[bi, sl(rev, k), lg] = res[si][0:c_len] + uy
            st[si] = res[si][c_len:n_l] + o
    for si in range(len(seqs)):
        st_ref[si] = st[si]


def _scan(feats, n_ctx_tiles):
    lw0, lw1, kd0, kd1, b0, b1, v, kk, r = feats
    nb, s, d = v.shape
    t = TOK_TILE
    nt = s // t
    msk, tri3, hm = _scan_masks()
    lanes = SCAN_GROUPS * GROUP
    fwd = pl.BlockSpec((nb, t, lanes), lambda q, i: (0, i, q))
    bwd_idx = lambda i: jnp.where(i < n_ctx_tiles, n_ctx_tiles - 1 - i, nt - 1 - (i - n_ctx_tiles))
    bwd = pl.BlockSpec((nb, t, lanes), lambda q, i: (0, bwd_idx(i), q))
    return pl.pallas_call(
        _scan_kernel,
        grid=(d // lanes, nt),
        in_specs=[fwd] * 6 + [bwd] * 6 + [pl.BlockSpec(msk.shape, lambda q, i: (0, 0, 0)),
                                          pl.BlockSpec(tri3.shape, lambda q, i: (0, 0, 0)),
                                          pl.BlockSpec(hm.shape, lambda q, i: (0, 0))],
        out_specs=[fwd, bwd],
        out_shape=[jax.ShapeDtypeStruct((nb, s, d), F32)] * 2,
        scratch_shapes=[pltpu.VMEM((2 * nb * SCAN_GROUPS, GROUP, GROUP), F32)],
        compiler_params=_params("parallel", "arbitrary"),
        name="scan",
    )(lw0, kd0, b0, v, kk, r, lw1, kd1, b1, v, kk, r, msk, tri3, hm)


def _readout_kernel(yf_ref, yb_ref, r_ref, kd0_ref, kd1_ref, v_ref, g_ref, x_ref, mod_ref, vec_ref,
                    wo_ref, hd_ref, hdt_ref, rwt_ref, rb_ref, u_ref,
                    x3_ref, h2_ref, pos_ref, gd_ref, off_ref, *, tiles_per_moe, n_real_tiles):
    i = pl.program_id(0)

    @pl.when(i >= n_real_tiles)
    def _():
        _route_padding(h2_ref, pos_ref, gd_ref)

    def rows_chain(rows, mod, vec):
        headsum = lambda z: _headsum(z, hd_ref[...], hdt_ref[...])
        inv_k = 1.0 / RWKV_HEAD
        y = yf_ref[0, rows, :] + yb_ref[0, rows, :]
        bsum = headsum(r_ref[0, rows, :] * (kd0_ref[0, rows, :] + kd1_ref[0, rows, :]) * vec[2:3])
        yield
        yc = y - headsum(y) * inv_k
        yield
        var = headsum(yc * yc) * inv_k
        yield
        yn = yc * lax.rsqrt(var + GN_EPS) * vec[0:1] + vec[1:2]
        out = (yn + bsum * v_ref[0, rows, :]) * g_ref[0, rows, :]
        x3 = x_ref[rows, :] + mod[2:3] * _dotb(out, wo_ref[...])
        x3_ref[rows, :] = x3
        yield
        return (yield from _route_rows(x3, mod, vec[3:4], rwt_ref, rb_ref, h2_ref, rows))

    @pl.when(i < n_real_tiles)
    def _():
        vec = vec_ref[...]
        mod = mod_ref[0]
        parts = _interleave(*[rows_chain(rows, mod, vec) for rows in _row_splits(x_ref.shape[0], TILE_SPLIT)])
        _route_finish(parts, u_ref, off_ref, pos_ref, gd_ref, i % tiles_per_moe == 0)


def _readout(yf, yb, r, kd0, kd1, v, g, x2d, mod, vec, wo, hd, hdt, rwt, rb, n_ctx_tiles, moe_tile,
             moe_span, tokens_per_batch):
    ntok, d = x2d.shape
    t = TOK_TILE
    n_e = rwt.shape[0]
    tpb = tokens_per_batch // t
    n_real = ntok // t
    n_pad = _padded_tokens(ntok, moe_span)
    route_specs, route_shapes = _route_outputs(n_pad, d, n_e, moe_tile)
    real = lambda i: jnp.minimum(i, n_real - 1)
    seq = pl.BlockSpec((1, t, d), lambda i: (real(i) // tpb, n_ctx_tiles + real(i) % tpb, 0))
    tokspec = pl.BlockSpec((t, d), lambda i: (real(i), 0))
    full = lambda a: pl.BlockSpec(a.shape, lambda i: (0,) * a.ndim)
    u = _route_consts(t)
    kern = functools.partial(_readout_kernel, tiles_per_moe=moe_tile // t, n_real_tiles=n_real)
    return pl.pallas_call(
        kern,
        grid=(n_pad // t,),
        in_specs=[seq] * 7 + [tokspec, pl.BlockSpec((1, 8, d), lambda i: (real(i) // tpb, 0, 0)),
                              full(vec), full(wo), full(hd), full(hdt), full(rwt), full(rb), full(u)],
        out_specs=[tokspec] + route_specs,
        out_shape=[jax.ShapeDtypeStruct((ntok, d), F32)] + route_shapes,
        scratch_shapes=[pltpu.VMEM((n_e, 128), F32)],
        compiler_params=_params("arbitrary"),
        name="readout",
    )(yf, yb, r, kd0, kd1, v, g, x2d, mod, vec, wo, hd, hdt, rwt, rb, u)


def _final_kernel(x_ref, ft_ref, mod_ref, g_ref, o_ref):
    x = x_ref[...] + mod_ref[0][5:6] * jnp.transpose(ft_ref[...])
    o_ref[...] = _rms(x) * g_ref[0:1]


def _final(x2d, ft, mod, gvec, tokens_per_batch):
    ntok, d = x2d.shape
    t = TOK_TILE
    tpb = tokens_per_batch // t
    return pl.pallas_call(
        _final_kernel,
        grid=(ntok // t,),
        in_specs=[pl.BlockSpec((t, d), lambda i: (i, 0)),
                  pl.BlockSpec((d, t), lambda i: (0, i)),
                  pl.BlockSpec((1, 8, d), lambda i: (i // tpb, 0, 0)),
                  pl.BlockSpec((8, d), lambda i: (0, 0))],
        out_specs=pl.BlockSpec((t, d), lambda i: (i, 0)),
        out_shape=jax.ShapeDtypeStruct((ntok, d), F32),
        compiler_params=_params("parallel"),
        name="final",
    )(x2d, ft, mod, gvec)


def _rows8(rows, d):
    n = -(-len(rows) // 8) * 8
    out = jnp.zeros((n, d), F32)
    return out.at[:len(rows)].set(jnp.stack([jnp.asarray(r, F32) for r in rows]))


def _pad_lora(w, total):
    two, r, d = w.shape
    out = jnp.zeros((two, total, d), w.dtype)
    for dd in range(two):
        out = out.at[dd, dd * r:(dd + 1) * r].set(w[dd])
    return out


def kernel(x, c, ctx, c_ctx, norm_g, ada_w, ada_b, pool_w, pool_ls, rwkv_mu, rwkv_w_rkv, rwkv_w0, rwkv_w1, rwkv_w2, rwkv_a0, rwkv_a1, rwkv_a2, rwkv_g1, rwkv_g2, rwkv_k_k, rwkv_k_a, rwkv_r_k, rwkv_ln_g, rwkv_ln_b, rwkv_w_o, moe_router_w, moe_router_b, moe_w_gu, moe_b_gu, moe_w_down, moe_b_down, final_g):
    nb, seq, d = x.shape
    n_ctx = ctx.shape[1]
    depth = norm_g.shape[0]
    n_e = moe_router_w.shape[2]
    t = TOK_TILE
    assert depth == 2 and nb <= 7 and seq % t == 0 and n_ctx % t == 0 and t % GRID_W == 0 and d % GROUP == 0
    n_lat, n_ctxtok = nb * seq, nb * n_ctx
    moe_tile_lat = min(MOE_TILE, n_lat)
    moe_tile_ctx = min(MOE_TILE, n_ctxtok)
    group_lat = min(MOE_GROUP, -(-n_lat // moe_tile_lat))
    group_ctx = min(MOE_GROUP, -(-n_ctxtok // moe_tile_ctx))
    n_ctx_tiles = n_ctx // t
    seq_total = n_ctx + seq

    cond8 = jnp.zeros((8, d), F32).at[:nb].set(c).at[nb].set(c_ctx)
    ada = _ada(cond8, ada_w, ada_b).reshape(depth, 8, N_MOD, d)
    pad8 = lambda m: jnp.pad(m, ((0, 0), (0, 8 - N_MOD), (0, 0)))
    mod_lat = [pad8(ada[i, :nb]) for i in range(depth)]
    mod_ctx = [pad8(jnp.broadcast_to(ada[i, nb][None], (nb, N_MOD, d))) for i in range(depth)]

    wgu_all = _expert_weights_bf16(moe_w_gu, transpose=False)
    wdt_all = _expert_weights_bf16(moe_w_down, transpose=True)

    def expert_params(i):
        return i, wgu_all, moe_b_gu[i].reshape(n_e, 1, -1), wdt_all, moe_b_down[i].T

    def router_params(i):
        return moe_router_w[i].T, moe_router_b[i].reshape(n_e, 1)

    x2d = x.reshape(n_lat, d)
    ctx2d = ctx.reshape(n_ctxtok, d)

    vec0 = _rows8([norm_g[0, 0], norm_g[0, 1], pool_ls[0]], d)
    pw = pool_w[0].astype(BF16)
    rwt0, rb0 = router_params(0)
    ex0 = expert_params(0)
    x1, h2, pos, gd = _pool_layer(x2d, mod_lat[0], vec0, pw, rwt0, rb0, GRID_W, moe_tile_lat,
                                  group_lat * moe_tile_lat, seq)
    ft_lat = _moe(h2, pos, gd, *ex0, group_lat)
    c1, h2c, posc, gdc = _pool_layer(ctx2d, mod_ctx[0], vec0, pw, rwt0, rb0, n_ctx, moe_tile_ctx,
                                     group_ctx * moe_tile_ctx, n_ctx)
    ft_ctx = _moe(h2c, posc, gdc, *ex0, group_ctx)

    vec1 = _rows8([norm_g[1, 0]], d)
    x2, h_all = _respre(x1, ft_lat, mod_lat[0], mod_lat[1], vec1, None, seq, n_ctx_tiles, seq_total)
    _, h_all = _respre(c1, ft_ctx, mod_ctx[0], mod_ctx[1], vec1, h_all, n_ctx, 0, seq_total)

    fv = _rows8([rwkv_mu[0, m] for m in range(6)]
                + [rwkv_w0[0, 0], rwkv_w0[0, 1], rwkv_a0[0, 0], rwkv_a0[0, 1], rwkv_k_k[0], rwkv_k_a[0]], d)
    lora = rwkv_w1.shape[3]
    cat2 = lambda w: jnp.concatenate([w[0, 0], w[0, 1]], axis=1).astype(BF16)
    hd, hdt = _head_indicators(d)
    feats = _features(h_all, fv, rwkv_w_rkv[0].astype(BF16),
                      cat2(rwkv_w1), _pad_lora(rwkv_w2[0], 2 * lora).astype(BF16),
                      cat2(rwkv_a1), _pad_lora(rwkv_a2[0], 2 * rwkv_a1.shape[3]).astype(BF16),
                      rwkv_g1[0].astype(BF16), rwkv_g2[0].astype(BF16), hd, hdt, n_ctx_tiles)
    lw0, lw1, kd0, kd1, b0, b1, v, kk, r, g = feats
    yf, yb = _scan((lw0, lw1, kd0, kd1, b0, b1, v, kk, r), n_ctx_tiles)

    vec_ro = _rows8([rwkv_ln_g[0], rwkv_ln_b[0], rwkv_r_k[0].reshape(-1), norm_g[1, 1]], d)
    rwt1, rb1 = router_params(1)
    x3, h2, pos, gd = _readout(yf, yb, r, kd0, kd1, v, g, x2, mod_lat[1], vec_ro,
                               rwkv_w_o[0].astype(BF16), hd, hdt, rwt1, rb1, n_ctx_tiles, moe_tile_lat,
                               group_lat * moe_tile_lat, seq)
    ft = _moe(h2, pos, gd, *expert_params(1), group_lat)
    out = _final(x3, ft, mod_lat[1], _rows8([final_g], d), seq)
    return out.reshape(nb, seq, d)
```
